```python
import jax, jax.numpy as jnp
from jax import lax
import numpy as np

D_MODEL = 2048
BATCH = 2
SEQ = 4096
DEPTH = 1
DEC_BATCH = 128
DEC_SEQ = 4
PAST_LEN = 8192
PAGE_SIZE = 128

NORM_EPS = 1e-6
RWKV_HEAD_DIM = 64
RWKV_WIDTH = D_MODEL
RWKV_HEADS = RWKV_WIDTH // RWKV_HEAD_DIM
DECAY_LORA = 96
AAA_LORA = 96
GATE_LORA = 256
GN_EPS = 64e-5
MLA_HEADS = 16
Q_LORA = 512
KV_LORA = 512
NOPE_DIM = 128
ROPE_DIM = 64
V_DIM = 128
MLA_WIDTH = MLA_HEADS * V_DIM
MLA_SCALE = (NOPE_DIM + ROPE_DIM) ** -0.5
ROPE_THETA = 10000.0
Q_BLOCK = 128
NEG_INF = -1e30
N_EXPERTS = 32
TOP_K = 4
D_EXPERT = D_MODEL
SWIGLU_ALPHA = 1.702
SWIGLU_LIMIT = 7.0
ROW_BLOCK = 128
RWKV_COLS = 3 * RWKV_WIDTH + DECAY_LORA + AAA_LORA + GATE_LORA
MLA_COLS = Q_LORA + KV_LORA + ROPE_DIM
GATE_COLS = 2 * D_MODEL
IN_COLS = RWKV_COLS + MLA_COLS + GATE_COLS

kernel_name = "rwkv7_mla_gated_hybrid_moe_step"


def split_cols(p, sizes):
    idx = np.cumsum(sizes)[:-1].tolist()
    return jnp.split(p, idx, axis=-1)


def rms_norm(x, w, eps=NORM_EPS):
    xf = x.astype(jnp.float32)
    y = xf * lax.rsqrt(jnp.mean(xf * xf, axis=-1, keepdims=True) + eps)
    return (y * w.astype(jnp.float32)).astype(x.dtype)


def rope_tables(positions):
    inv = ROPE_THETA ** (-jnp.arange(0, ROPE_DIM, 2, dtype=jnp.float32) / ROPE_DIM)
    ang = positions.astype(jnp.float32)[:, None] * inv[None, :]
    return jnp.cos(ang), jnp.sin(ang)


def apply_rope(x, cos, sin):
    half = ROPE_DIM // 2
    x1, x2 = x[..., :half], x[..., half:]
    c, s = cos.astype(x.dtype), sin.astype(x.dtype)
    return jnp.concatenate([x1 * c - x2 * s, x2 * c + x1 * s], axis=-1)


def rwkv_time_mix(p, p_prev, wkv0, mu, w0, w2, a0, a2, g2, k_k, k_a, r_k, ln_w, ln_b):
    B, T, _ = p.shape
    f32 = jnp.float32
    xx = p + (p_prev - p) * mu
    r, k, v, hw, ha, hg = split_cols(xx, [RWKV_WIDTH] * 3 + [DECAY_LORA, AAA_LORA, GATE_LORA])
    w = -jax.nn.softplus(-(w0 + jnp.tanh(hw) @ w2)) - 0.5
    a = jax.nn.sigmoid(a0 + ha @ a2)
    g = jax.nn.sigmoid(hg) @ g2
    heads = lambda t: t.reshape(B, T, RWKV_HEADS, RWKV_HEAD_DIM)
    kk = heads(k * k_k).astype(f32)
    kk = kk * lax.rsqrt(jnp.maximum(jnp.sum(kk * kk, axis=-1, keepdims=True), 1e-24))
    k = k * (1 + (a - 1) * k_a)
    decay = jnp.exp(-jnp.exp(w.astype(f32)))
    a_h = heads(a).astype(f32)
    tm = lambda t: jnp.swapaxes(t, 0, 1)
    seq = (tm(heads(r).astype(f32)), tm(heads(decay)), tm(heads(k).astype(f32)),
           tm(heads(v).astype(f32)), tm(-kk), tm(kk * a_h))

    def step(S, inp):
        r_t, w_t, k_t, v_t, a_t, b_t = inp
        sa = jnp.einsum('bhvk,bhk->bhv', S, a_t)
        S = S * w_t[:, :, None, :] + sa[..., None] * b_t[:, :, None, :] + v_t[..., None] * k_t[:, :, None, :]
        return S, jnp.einsum('bhvk,bhk->bhv', S, r_t)

    S_final, y = lax.scan(step, wkv0.astype(f32), seq)
    y = jnp.swapaxes(y, 0, 1)
    mean = jnp.mean(y, axis=-1, keepdims=True)
    var = jnp.mean(jnp.square(y - mean), axis=-1, keepdims=True)
    yn = ((y - mean) * lax.rsqrt(var + GN_EPS)).reshape(B, T, RWKV_WIDTH)
    yn = (yn * ln_w.astype(f32) + ln_b.astype(f32)).astype(p.dtype)
    bonus = jnp.sum(heads(r) * heads(k) * r_k, axis=-1, keepdims=True) * heads(v)
    o = (yn + bonus.reshape(B, T, RWKV_WIDTH)) * g
    return o, S_final.astype(wkv0.dtype)


def mla_project(pm, positions, q_norm_w, w_uq, kv_norm_w):
    cq, ckv_raw, kr_raw = split_cols(pm, [Q_LORA, KV_LORA, ROPE_DIM])
    q = jnp.einsum('btc,chd->bthd', rms_norm(cq, q_norm_w), w_uq)
    q_nope, q_rope = q[..., :NOPE_DIM], q[..., NOPE_DIM:]
    cos, sin = rope_tables(positions)
    q_rope = apply_rope(q_rope, cos[:, None, :], sin[:, None, :])
    k_rope = apply_rope(kr_raw, cos, sin)
    c_kv = rms_norm(ckv_raw, kv_norm_w)
    return q_nope, q_rope, c_kv, k_rope


def mla_prompt_attention(q_nope, q_rope, c_kv, k_rope, w_uk, w_uv):
    B, T = q_nope.shape[:2]
    k_nope = jnp.einsum('btc,chd->bthd', c_kv, w_uk)
    v = jnp.einsum('btc,chd->bthd', c_kv, w_uv)
    n_blk = T // Q_BLOCK
    qn = jnp.swapaxes(q_nope.reshape(B, n_blk, Q_BLOCK, MLA_HEADS, NOPE_DIM), 0, 1)
    qr = jnp.swapaxes(q_rope.reshape(B, n_blk, Q_BLOCK, MLA_HEADS, ROPE_DIM), 0, 1)
    key_pos = jnp.arange(T)

    def one_block(args):
        blk, qn_b, qr_b = args
        s = (jnp.einsum('bqhd,bkhd->bhqk', qn_b, k_nope)
             + jnp.einsum('bqhd,bkd->bhqk', qr_b, k_rope)).astype(jnp.float32) * MLA_SCALE
        q_pos = blk * Q_BLOCK + jnp.arange(Q_BLOCK)
        s = jnp.where(key_pos[None, :] <= q_pos[:, None], s, NEG_INF)
        pr = jax.nn.softmax(s, axis=-1).astype(v.dtype)
        return jnp.einsum('bhqk,bkhd->bqhd', pr, v)

    o = lax.map(one_block, (jnp.arange(n_blk), qn, qr))
    return jnp.swapaxes(o, 0, 1).reshape(B, T, MLA_WIDTH)


def mla_sample_attention(q_nope, q_rope, c_kv, k_rope, cache_ckv, cache_krope, page_table, w_uk, w_uv):
    Bd, T = q_nope.shape[:2]
    past = page_table.shape[1] * cache_ckv.shape[1]
    q_lat = jnp.einsum('bthd,chd->bthc', q_nope, w_uk)
    key_ok = jnp.concatenate([jnp.ones((T, past), bool), jnp.tril(jnp.ones((T, T), bool))], axis=1)

    def one_seq(args):
        pages, ql, qr, ckv_new, kr_new = args
        ckv = jnp.concatenate([cache_ckv[pages].reshape(past, KV_LORA), ckv_new], axis=0)
        kr = jnp.concatenate([cache_krope[pages].reshape(past, ROPE_DIM), kr_new], axis=0)
        s = (jnp.einsum('thc,kc->htk', ql, ckv)
             + jnp.einsum('thd,kd->htk', qr, kr)).astype(jnp.float32) * MLA_SCALE
        s = jnp.where(key_ok[None], s, NEG_INF)
        pr = jax.nn.softmax(s, axis=-1).astype(ckv.dtype)
        return jnp.einsum('htk,kc->thc', pr, ckv)

    o_lat = lax.map(one_seq, (page_table, q_lat, q_rope, c_kv, k_rope))
    return jnp.einsum('bthc,chd->bthd', o_lat, w_uv).reshape(Bd, T, MLA_WIDTH)


def clamped_swiglu(h):
    h_glu, h_lin = h[..., ::2], h[..., 1::2]
    h_glu = jnp.minimum(h_glu, SWIGLU_LIMIT)
    h_lin = jnp.clip(h_lin, -SWIGLU_LIMIT, SWIGLU_LIMIT)
    return h_glu * jax.nn.sigmoid(SWIGLU_ALPHA * h_glu) * (h_lin + 1)


def moe_ffn(x, router_w, router_b, w1, b1, w2, b2):
    B, T, D = x.shape
    M = B * T
    xf = x.reshape(M, D)
    logits = xf.astype(jnp.float32) @ router_w.astype(jnp.float32) + router_b.astype(jnp.float32)
    top_val, top_idx = lax.top_k(logits, TOP_K)
    gates = jax.nn.softmax(top_val, axis=-1).astype(x.dtype)
    n_assign = M * TOP_K
    flat_e = top_idx.reshape(-1).astype(jnp.int32)
    flat_tok = jnp.repeat(jnp.arange(M, dtype=jnp.int32), TOP_K)
    order = jnp.argsort(flat_e, stable=True)
    sorted_e, sorted_tok, sorted_gate = flat_e[order], flat_tok[order], gates.reshape(-1)[order]
    counts = jnp.bincount(flat_e, length=N_EXPERTS).astype(jnp.int32)
    group_start = jnp.cumsum(counts) - counts
    padded = (counts + ROW_BLOCK - 1) // ROW_BLOCK * ROW_BLOCK
    pad_end = jnp.cumsum(padded)
    pad_start = pad_end - padded
    dest = pad_start[sorted_e] + jnp.arange(n_assign, dtype=jnp.int32) - group_start[sorted_e]
    n_blocks = (n_assign + N_EXPERTS * (ROW_BLOCK - 1)) // ROW_BLOCK
    row_tok = jnp.full((n_blocks * ROW_BLOCK,), M, jnp.int32).at[dest].set(sorted_tok)
    block_start = jnp.arange(n_blocks, dtype=jnp.int32) * ROW_BLOCK
    block_expert = jnp.minimum(jnp.searchsorted(pad_end, block_start, side='right'), N_EXPERTS - 1)
    x_pad = jnp.concatenate([xf, jnp.zeros((1, D), xf.dtype)], axis=0)

    def expert_block(args):
        rows, e = args
        h = x_pad[rows] @ w1[e] + b1[e]
        return clamped_swiglu(h) @ w2[e] + b2[e]

    y_rows = lax.map(expert_block, (row_tok.reshape(n_blocks, ROW_BLOCK), block_expert)).reshape(-1, D)
    y = jax.ops.segment_sum(y_rows[dest] * sorted_gate[:, None], sorted_tok, num_segments=M)
    return y.reshape(B, T, D)


def setup_inputs(seed: int = 0) -> dict:
    key = jax.random.key(seed)
    ks = iter(jax.random.split(key, 40))
    f32 = jnp.float32
    nrm = lambda shape, scale: scale * jax.random.normal(next(ks), shape, f32)
    n_pages = PAST_LEN // PAGE_SIZE
    n_used = DEC_BATCH * n_pages
    n_phys = n_used + (n_used + 3) // 4
    page_table = jax.random.permutation(next(ks), n_phys)[:n_used].reshape(DEC_BATCH, n_pages).astype(jnp.int32)
    return {
        "x_prompt": nrm((BATCH, SEQ, D_MODEL), 1.0),
        "x_sample": nrm((DEC_BATCH, DEC_SEQ, D_MODEL), 1.0),
        "cache_ckv": nrm((n_phys, PAGE_SIZE, KV_LORA), 1.0),
        "cache_krope": nrm((n_phys, PAGE_SIZE, ROPE_DIM), 1.0),
        "state_wkv": nrm((DEC_BATCH, RWKV_HEADS, RWKV_HEAD_DIM, RWKV_HEAD_DIM), 0.3),
        "state_shift": nrm((DEC_BATCH, D_MODEL), 1.0),
        "page_table": page_table,
        "norm_mix_w": 1.0 + nrm((D_MODEL,), 0.1),
        "w_in": nrm((D_MODEL, IN_COLS), D_MODEL ** -0.5),
        "rwkv_mu": jax.random.uniform(next(ks), (RWKV_COLS,), f32),
        "rwkv_w0": nrm((RWKV_WIDTH,), 0.5),
        "rwkv_w2": nrm((DECAY_LORA, RWKV_WIDTH), 0.5 * DECAY_LORA ** -0.5),
        "rwkv_a0": nrm((RWKV_WIDTH,), 0.1),
        "rwkv_a2": nrm((AAA_LORA, RWKV_WIDTH), 0.5 * AAA_LORA ** -0.5),
        "rwkv_g2": nrm((GATE_LORA, RWKV_WIDTH), GATE_LORA ** -0.5),
        "rwkv_k_k": 1.0 + nrm((RWKV_WIDTH,), 0.1),
        "rwkv_k_a": 1.0 + nrm((RWKV_WIDTH,), 0.1),
        "rwkv_r_k": nrm((RWKV_HEADS, RWKV_HEAD_DIM), 0.1),
        "rwkv_ln_w": 1.0 + nrm((RWKV_WIDTH,), 0.1),
        "rwkv_ln_b": nrm((RWKV_WIDTH,), 0.01),
        "mla_q_norm_w": 1.0 + nrm((Q_LORA,), 0.1),
        "mla_w_uq": nrm((Q_LORA, MLA_HEADS, NOPE_DIM + ROPE_DIM), Q_LORA ** -0.5),
        "mla_kv_norm_w": 1.0 + nrm((KV_LORA,), 0.1),
        "mla_w_uk": nrm((KV_LORA, MLA_HEADS, NOPE_DIM), KV_LORA ** -0.5),
        "mla_w_uv": nrm((KV_LORA, MLA_HEADS, V_DIM), KV_LORA ** -0.5),
        "w_out": nrm((MLA_WIDTH, D_MODEL), MLA_WIDTH ** -0.5),
        "norm_ffn_w": 1.0 + nrm((D_MODEL,), 0.1),
        "router_w": nrm((D_MODEL, N_EXPERTS), D_MODEL ** -0.5),
        "router_b": nrm((N_EXPERTS,), 0.01),
        "expert_w1": nrm((N_EXPERTS, D_MODEL, 2 * D_EXPERT), D_MODEL ** -0.5),
        "expert_b1": nrm((N_EXPERTS, 2 * D_EXPERT), 0.01),
        "expert_w2": nrm((N_EXPERTS, D_EXPERT, D_MODEL), D_EXPERT ** -0.5),
        "expert_b2": nrm((N_EXPERTS, D_MODEL), 0.01),
        "norm_final_w": 1.0 + nrm((D_MODEL,), 0.1),
    }


def reference(x_prompt, x_sample, cache_ckv, cache_krope, state_wkv, state_shift, page_table,
              norm_mix_w, w_in, rwkv_mu, rwkv_w0, rwkv_w2, rwkv_a0, rwkv_a2, rwkv_g2, rwkv_k_k,
              rwkv_k_a, rwkv_r_k, rwkv_ln_w, rwkv_ln_b, mla_q_norm_w, mla_w_uq, mla_kv_norm_w,
              mla_w_uk, mla_w_uv, w_out, norm_ffn_w, router_w, router_b, expert_w1, expert_b1,
              expert_w2, expert_b2, norm_final_w):

    def run_group(x, shift_row, wkv0, positions, attend):
        xn = rms_norm(x, norm_mix_w)
        rows = jnp.concatenate([shift_row[:, None, :].astype(xn.dtype), xn], axis=1)
        p_all = jnp.einsum('btd,dc->btc', rows, w_in)
        p, p_prev = p_all[:, 1:], p_all[:, :-1]
        p_rwkv, p_mla, p_gate = split_cols(p, [RWKV_COLS, MLA_COLS, GATE_COLS])
        o_rwkv, wkv_new = rwkv_time_mix(p_rwkv, p_prev[..., :RWKV_COLS], wkv0, rwkv_mu, rwkv_w0,
                                        rwkv_w2, rwkv_a0, rwkv_a2, rwkv_g2, rwkv_k_k, rwkv_k_a,
                                        rwkv_r_k, rwkv_ln_w, rwkv_ln_b)
        q_nope, q_rope, c_kv, k_rope = mla_project(p_mla, positions, mla_q_norm_w, mla_w_uq, mla_kv_norm_w)
        o_mla = attend(q_nope, q_rope, c_kv, k_rope)
        g_rwkv, g_mla = split_cols(p_gate, [D_MODEL, D_MODEL])
        merged = jax.nn.sigmoid(g_rwkv) * o_rwkv + jax.nn.sigmoid(g_mla) * o_mla
        h = x + merged @ w_out
        h = h + moe_ffn(rms_norm(h, norm_ffn_w), router_w, router_b, expert_w1, expert_b1, expert_w2, expert_b2)
        return rms_norm(h, norm_final_w), c_kv, k_rope, wkv_new, xn[:, -1]

    B, S = x_prompt.shape[:2]
    page = cache_ckv.shape[1]
    y_p, ckv_p, kr_p, wkv_p, shift_p = run_group(
        x_prompt, jnp.zeros((B, D_MODEL), x_prompt.dtype),
        jnp.zeros((B, RWKV_HEADS, RWKV_HEAD_DIM, RWKV_HEAD_DIM), x_prompt.dtype),
        jnp.arange(S),
        lambda qn, qr, ckv, kr: mla_prompt_attention(qn, qr, ckv, kr, mla_w_uk, mla_w_uv))

    past = page_table.shape[1] * page
    T = x_sample.shape[1]
    y_s, ckv_s, kr_s, wkv_s, shift_s = run_group(
        x_sample, state_shift, state_wkv, past + jnp.arange(T),
        lambda qn, qr, ckv, kr: mla_sample_attention(qn, qr, ckv, kr, cache_ckv, cache_krope,
                                                     page_table, mla_w_uk, mla_w_uv))

    return (y_p, y_s,
            ckv_p.reshape(B, S // page, page, KV_LORA), kr_p.reshape(B, S // page, page, ROPE_DIM),
            wkv_p, shift_p,
            ckv_s, kr_s, wkv_s, shift_s)
```

```python
import functools

import numpy as np
import jax
import jax.numpy as jnp
from jax import lax
from jax.experimental import pallas as pl
from jax.experimental.pallas import tpu as pltpu

F32 = jnp.float32
BF16 = jnp.bfloat16

NORM_EPS = 1e-6
GN_EPS = 64e-5
RWKV_HEAD_DIM = 64
NOPE_DIM = 128
ROPE_DIM = 64
V_DIM = 128
QK_PAD = 256
ROPE_THETA = 10000.0
NEG_INF = -1e30
TOP_K = 4
SWIGLU_ALPHA = 1.702
SWIGLU_LIMIT = 7.0
LANES = 128
VMEM_LIMIT = 56 * 1024 * 1024


def _round_up(n, m):
    return (n + m - 1) // m * m


def _pick_tile(n, target, mult):
    if n <= target:
        return n
    t = target // mult * mult
    while n % t:
        t -= mult
    return t


def _cparams(*sem):
    return pltpu.CompilerParams(dimension_semantics=sem, vmem_limit_bytes=VMEM_LIMIT)


def _split3(x):
    hi = x.astype(BF16)
    r1 = x - hi.astype(F32)
    mid = r1.astype(BF16)
    lo = (r1 - mid.astype(F32)).astype(BF16)
    return hi, mid, lo


def _dot_sel(x, sel_bf16):
    hi, mid, lo = _split3(x)
    d = lambda a: jnp.dot(a, sel_bf16, preferred_element_type=F32)
    return d(hi) + d(mid) + d(lo)


def _rms(x, w):
    return x * lax.rsqrt(jnp.mean(x * x, axis=-1, keepdims=True) + NORM_EPS) * w


def _proj_kernel(x_ref, nw_ref, w_ref, o_ref, xn_ref, *, normalize):
    @pl.when(pl.program_id(1) == 0)
    def _():
        x = x_ref[...]
        if normalize:
            x = _rms(x, nw_ref[...])
        xn_ref[...] = x.astype(BF16)

    o_ref[...] = jnp.dot(xn_ref[...], w_ref[...], preferred_element_type=F32)


def _project(x, norm_w, w, *, normalize, tm, tn):
    M, D = x.shape
    N = w.shape[1]
    tm = _pick_tile(M, tm, 8)
    tn = _pick_tile(N, tn, LANES)
    return pl.pallas_call(
        functools.partial(_proj_kernel, normalize=normalize),
        grid=(M // tm, N // tn),
        in_specs=[pl.BlockSpec((tm, D), lambda i, j: (i, 0)),
                  pl.BlockSpec((1, D), lambda i, j: (0, 0)),
                  pl.BlockSpec((D, tn), lambda i, j: (0, j))],
        out_specs=pl.BlockSpec((tm, tn), lambda i, j: (i, j)),
        out_shape=jax.ShapeDtypeStruct((M, N), F32),
        scratch_shapes=[pltpu.VMEM((tm, D), BF16)],
        compiler_params=_cparams("parallel", "arbitrary"),
        name="norm_proj",
    )(x, norm_w, w)


def _rmsnorm_rows_kernel(x_ref, w_ref, o_ref):
    o_ref[...] = _rms(x_ref[...], w_ref[...])


def _rmsnorm_rows(x, w):
    return pl.pallas_call(
        _rmsnorm_rows_kernel,
        out_shape=jax.ShapeDtypeStruct(x.shape, F32),
        name="rmsnorm_rows",
    )(x, w)


def _rwkv_prep_kernel(r_ref, k_ref, v_ref, l_ref, sr_ref, sk_ref, sv_ref, sl_ref,
                      mur_ref, muk_ref, muv_ref, mul_ref, w0_ref, w2_ref, a0_ref, a2_ref, g2_ref,
                      kk_ref, ka_ref, rk_ref, e_ref, et_ref,
                      ro_ref, lw_ref, ko_ref, vo_ref, al_ref, be_ref, go_ref, bo_ref,
                      cr_ref, ck_ref, cv_ref, cl_ref, *, shift, lw_pad, la_pad):
    t = pl.program_id(1)

    @pl.when(t == 0)
    def _():
        cr_ref[...] = sr_ref[...]
        ck_ref[...] = sk_ref[...]
        cv_ref[...] = sv_ref[...]
        cl_ref[...] = sl_ref[...]

    def lerp(p_ref, c_ref, mu_ref):
        p = p_ref[...]
        if shift == 1:
            rolled = pltpu.roll(p, 1, axis=0)
            row = lax.broadcasted_iota(jnp.int32, p.shape, 0)
            prev = jnp.where(row == 0, c_ref[...], rolled)
            c_new = p[p.shape[0] - 1:, :]
        else:
            prev = c_ref[...]
            c_new = p
        xx = p + (prev - p) * mu_ref[...]
        c_ref[...] = c_new
        return xx

    r = lerp(r_ref, cr_ref, mur_ref)
    k = lerp(k_ref, ck_ref, muk_ref)
    v = lerp(v_ref, cv_ref, muv_ref)
    lo = lerp(l_ref, cl_ref, mul_ref)
    hw = lo[:, :lw_pad]
    ha = lo[:, lw_pad:lw_pad + la_pad]
    hg = lo[:, lw_pad + la_pad:]

    mm = lambda a, b_ref: jnp.dot(a.astype(BF16), b_ref[...], preferred_element_type=F32)
    w = -jax.nn.softplus(-(w0_ref[...] + mm(jnp.tanh(hw), w2_ref))) - 0.5
    a = jax.nn.sigmoid(a0_ref[...] + mm(ha, a2_ref))
    g = mm(jax.nn.sigmoid(hg), g2_ref)

    e = e_ref[...]
    et = et_ref[...]
    kk = k * kk_ref[...]
    ssq = _dot_sel(kk * kk, e)
    inv = lax.rsqrt(jnp.maximum(ssq, 1e-24))
    kk = kk * _dot_sel(inv, et)
    k2 = k * (1.0 + (a - 1.0) * ka_ref[...])
    bsum = _dot_sel(_dot_sel(r * k2 * rk_ref[...], e), et)

    ro_ref[...] = r
    lw_ref[...] = -jnp.exp(w)
    ko_ref[...] = k2
    vo_ref[...] = v
    al_ref[...] = -kk
    be_ref[...] = kk * a
    go_ref[...] = g
    bo_ref[...] = bsum * v


def _rwkv_prep(p5, ps, shift5, shifts, consts, *, n_groups, n_tiles, tt, shift, D, lora_w):
    M = p5.shape[0]
    (mur, muk, muv, mul, w0, w2p, a0, a2p, g2, k_k, k_a, r_k, e, et, lw_pad, la_pad) = consts
    row = lambda g, t: (g * n_tiles + t)
    big = lambda c: pl.BlockSpec((tt, D), lambda g, t, c=c: (row(g, t), c))
    sh = lambda c: pl.BlockSpec((None, shift, D), lambda g, t, c=c: (g, 0, c))
    full = lambda a: pl.BlockSpec(a.shape, lambda g, t: (0,) * a.ndim)
    in_specs = [big(0), big(1), big(2),
                pl.BlockSpec((tt, lora_w), lambda g, t: (row(g, t), 0)),
                sh(0), sh(1), sh(2),
                pl.BlockSpec((None, shift, lora_w), lambda g, t: (g, 0, 0)),
                full(mur), full(muk), full(muv), full(mul), full(w0), full(w2p), full(a0),
                full(a2p), full(g2), full(k_k), full(k_a), full(r_k), full(e), full(et)]
    out_spec = pl.BlockSpec((tt, D), lambda g, t: (row(g, t), 0))
    outs = pl.pallas_call(
        functools.partial(_rwkv_prep_kernel, shift=shift, lw_pad=lw_pad, la_pad=la_pad),
        grid=(n_groups, n_tiles),
        in_specs=in_specs,
        out_specs=[out_spec] * 8,
        out_shape=[jax.ShapeDtypeStruct((M, D), F32)] * 8,
        scratch_shapes=[pltpu.VMEM((shift, D), F32)] * 3 + [pltpu.VMEM((shift, lora_w), F32)],
        compiler_params=_cparams("parallel", "arbitrary"),
        name="rwkv_prep",
    )(p5, p5, p5, ps, shift5, shift5, shift5, shifts,
      mur, muk, muv, mul, w0, w2p, a0, a2p, g2, k_k, k_a, r_k, e, et)
    return outs


def _rwkv_scan_kernel(r_ref, w_ref, k_ref, v_ref, a_ref, b_ref, s0_ref, y_ref, sf_ref, s_ref,
                      *, tb, vr):
    tblk = pl.program_id(1)

    @pl.when(tblk == 0)
    def _():
        s_ref[...] = s0_ref[...]

    def step(t, carry):
        r = r_ref[t]
        w = jnp.exp(w_ref[t])
        k = k_ref[t]
        a = a_ref[t]
        b = b_ref[t]
        vt = v_ref[t]
        rows = []
        for i in range(vr):
            s = s_ref[i]
            sa = jnp.sum(s * a, axis=0, keepdims=True)
            s = s * w + sa * b + vt[i:i + 1, :] * k
            s_ref[i] = s
            rows.append(jnp.sum(s * r, axis=0, keepdims=True))
        y_ref[t] = jnp.concatenate(rows, axis=0)
        return carry

    lax.fori_loop(0, tb, step, 0)

    @pl.when(tblk == pl.num_programs(1) - 1)
    def _():
        sf_ref[...] = s_ref[...]


def _rwkv_scan(r, w, k, v, a, b, s0, *, tb):
    T, K, NL = r.shape
    VR = v.shape[1]
    tb = min(tb, T)
    assert T % tb == 0 and NL % LANES == 0
    kspec = pl.BlockSpec((tb, K, LANES), lambda n, t: (t, 0, n))
    vspec = pl.BlockSpec((tb, VR, LANES), lambda n, t: (t, 0, n))
    sspec = pl.BlockSpec((VR, K, LANES), lambda n, t: (0, 0, n))
    return pl.pallas_call(
        functools.partial(_rwkv_scan_kernel, tb=tb, vr=VR),
        grid=(NL // LANES, T // tb),
        in_specs=[kspec, kspec, kspec, vspec, kspec, kspec, sspec],
        out_specs=[vspec, sspec],
        out_shape=[jax.ShapeDtypeStruct((T, VR, NL), F32),
                   jax.ShapeDtypeStruct((VR, K, NL), F32)],
        scratch_shapes=[pltpu.VMEM((VR, K, LANES), F32)],
        compiler_params=_cparams("parallel", "arbitrary"),
        name="rwkv_scan",
    )(r, w, k, v, a, b, s0)


def _mla_prep_kernel(cq_ref, ckv_ref, kr_ref, qtab_ref, ktab_ref, qn_ref, kvn_ref,
                     wqa_ref, wqb_ref, wk_ref, wv_ref,
                     q_ref, k_ref, v_ref, ckv_o_ref, kr_o_ref, cqn_ref, ckk_ref, *, hg, scale):
    @pl.when(pl.program_id(1) == 0)
    def _():
        cqn_ref[...] = _rms(cq_ref[...], qn_ref[...]).astype(BF16)
        ckv = _rms(ckv_ref[...], kvn_ref[...])
        ckv_o_ref[...] = ckv
        kr = kr_ref[...]
        ktab = ktab_ref[...]
        krope = kr[:, :ROPE_DIM] * ktab[:, :ROPE_DIM] + kr[:, ROPE_DIM:] * ktab[:, ROPE_DIM:]
        kr_o_ref[...] = krope
        ckk_ref[...] = jnp.concatenate([ckv, krope], axis=-1).astype(BF16)

    cqn = cqn_ref[...]
    qtab = qtab_ref[...]
    cosf = jnp.concatenate([qtab[:, :QK_PAD]] * hg, axis=-1)
    sinf = jnp.concatenate([qtab[:, QK_PAD:]] * hg, axis=-1)
    qa = jnp.dot(cqn, wqa_ref[...], preferred_element_type=F32)
    qb = jnp.dot(cqn, wqb_ref[...], preferred_element_type=F32)
    q_ref[...] = ((qa * cosf + qb * sinf) * scale).astype(BF16)
    ckk = ckk_ref[...]
    k_ref[...] = jnp.dot(ckk, wk_ref[...], preferred_element_type=F32).astype(BF16)
    v_ref[...] = jnp.dot(ckk[:, :ckk.shape[1] - ROPE_DIM], wv_ref[...],
                         preferred_element_type=F32).astype(BF16)


def _mla_prep(ps, qtab, ktab, qn, kvn, wqa, wqb, wk, wv, *, col_cq, col_ckv, col_kr, qlora, kvlora,
              n_heads, hg, tm, scale):
    M = ps.shape[0]
    tm = _pick_tile(M, tm, 8)
    n_tab = qtab.shape[0] // tm
    grid = (M // tm, n_heads // hg)
    tabspec = lambda a: pl.BlockSpec((tm, a.shape[1]), lambda i, j: (i % n_tab, 0))
    full = lambda a: pl.BlockSpec(a.shape, lambda i, j: (0,) * a.ndim)
    in_specs = [pl.BlockSpec((tm, qlora), lambda i, j: (i, col_cq // qlora)),
                pl.BlockSpec((tm, kvlora), lambda i, j: (i, col_ckv // kvlora)),
                pl.BlockSpec((tm, 2 * ROPE_DIM), lambda i, j: (i, col_kr // (2 * ROPE_DIM))),
                tabspec(qtab), tabspec(ktab), full(qn), full(kvn),
                pl.BlockSpec((qlora, hg * QK_PAD), lambda i, j: (0, j)),
                pl.BlockSpec((qlora, hg * QK_PAD), lambda i, j: (0, j)),
                pl.BlockSpec((kvlora + ROPE_DIM, hg * QK_PAD), lambda i, j: (0, j)),
                pl.BlockSpec((kvlora, hg * V_DIM), lambda i, j: (0, j))]
    out_specs = [pl.BlockSpec((tm, hg * QK_PAD), lambda i, j: (i, j)),
                 pl.BlockSpec((tm, hg * QK_PAD), lambda i, j: (i, j)),
                 pl.BlockSpec((tm, hg * V_DIM), lambda i, j: (i, j)),
                 pl.BlockSpec((tm, kvlora), lambda i, j: (i, 0)),
                 pl.BlockSpec((tm, ROPE_DIM), lambda i, j: (i, 0))]
    out_shape = [jax.ShapeDtypeStruct((M, n_heads * QK_PAD), BF16),
                 jax.ShapeDtypeStruct((M, n_heads * QK_PAD), BF16),
                 jax.ShapeDtypeStruct((M, n_heads * V_DIM), BF16),
                 jax.ShapeDtypeStruct((M, kvlora), F32),
                 jax.ShapeDtypeStruct((M, ROPE_DIM), F32)]
    return pl.pallas_call(
        functools.partial(_mla_prep_kernel, hg=hg, scale=scale),
        grid=grid, in_specs=in_specs, out_specs=out_specs, out_shape=out_shape,
        scratch_shapes=[pltpu.VMEM((tm, qlora), BF16), pltpu.VMEM((tm, kvlora + ROPE_DIM), BF16)],
        compiler_params=_cparams("parallel", "arbitrary"),
        name="mla_prep",
    )(ps, ps, ps, qtab, ktab, qn, kvn, wqa, wqb, wk, wv)


def _flash_kernel(q_ref, k_ref, v_ref, o_ref, *, tq, tk):
    i = pl.program_id(2)
    q = q_ref[...]
    n_sub = tq // tk

    def block(j, carry, masked):
        m, l, acc = carry
        start = pl.multiple_of(j * tk, tk)
        kb = k_ref[pl.ds(start, tk), :]
        vb = v_ref[pl.ds(start, tk), :]
        s = lax.dot_general(q, kb, (((1,), (1,)), ((), ())), preferred_element_type=F32)
        if masked:
            qpos = i * tq + lax.broadcasted_iota(jnp.int32, s.shape, 0)
            kpos = j * tk + lax.broadcasted_iota(jnp.int32, s.shape, 1)
            s = jnp.where(kpos <= qpos, s, NEG_INF)
        m_new = jnp.maximum(m, jnp.max(s, axis=-1, keepdims=True))
        p = jnp.exp(s - m_new)
        corr = jnp.exp(m - m_new)
        l = corr * l + jnp.sum(p, axis=-1, keepdims=True)
        acc = corr * acc + jnp.dot(p.astype(BF16), vb, preferred_element_type=F32)
        return m_new, l, acc

    init = (jnp.full((tq, 1), NEG_INF, F32), jnp.zeros((tq, 1), F32),
            jnp.zeros((tq, v_ref.shape[1]), F32))
    carry = lax.fori_loop(0, i * n_sub, lambda j, c: block(j, c, False), init)
    for d in range(n_sub):
        carry = block(i * n_sub + d, carry, True)
    m, l, acc = carry
    o_ref[...] = (acc / l).astype(o_ref.dtype)


def _flash_attention(q, k, v, *, B, T, n_heads, tq, tk):
    tq = min(tq, T)
    tk = min(tk, tq)
    nq = T // tq
    return pl.pallas_call(
        functools.partial(_flash_kernel, tq=tq, tk=tk),
        grid=(B, n_heads, nq),
        in_specs=[pl.BlockSpec((tq, QK_PAD), lambda b, h, i: (b * nq + i, h)),
                  pl.BlockSpec((T, QK_PAD), lambda b, h, i: (b, h)),
                  pl.BlockSpec((T, V_DIM), lambda b, h, i: (b, h))],
        out_specs=pl.BlockSpec((tq, V_DIM), lambda b, h, i: (b * nq + i, h)),
        out_shape=jax.ShapeDtypeStruct((B * T, n_heads * V_DIM), F32),
        compiler_params=_cparams("parallel", "parallel", "arbitrary"),
        name="mla_flash",
    )(q, k, v)


def _bmm_kernel(x_ref, w_ref, o_ref):
    o_ref[...] = jnp.dot(x_ref[...].astype(BF16), w_ref[...], preferred_element_type=F32)


def _head_matmul(x, w):
    H, R, K = x.shape
    N = w.shape[2]
    return pl.pallas_call(
        _bmm_kernel,
        grid=(H,),
        in_specs=[pl.BlockSpec((None, R, K), lambda h: (h, 0, 0)),
                  pl.BlockSpec((None, K, N), lambda h: (h, 0, 0))],
        out_specs=pl.BlockSpec((None, R, N), lambda h: (h, 0, 0)),
        out_shape=jax.ShapeDtypeStruct((H, R, N), F32),
        compiler_params=_cparams("parallel"),
        name="head_matmul",
    )(x, w)


def _paged_kernel(pt_ref, q_ref, cn_ref, kn_ref, *refs, pg, n_new, kvlora):
    ckv_refs = refs[:pg]
    kr_refs = refs[pg:2 * pg]
    o_ref = refs[2 * pg]
    m_ref, l_ref, acc_ref = refs[2 * pg + 1:]
    g = pl.program_id(1)

    @pl.when(g == 0)
    def _():
        m_ref[...] = jnp.full(m_ref.shape, NEG_INF, F32)
        l_ref[...] = jnp.zeros(l_ref.shape, F32)
        acc_ref[...] = jnp.zeros(acc_ref.shape, F32)

    q = q_ref[...]
    ql = q[:, :kvlora]
    qr = q[:, kvlora:]
    dn = (((1,), (1,)), ((), ()))

    def update(ckv, kr, mask):
        s = (lax.dot_general(ql, ckv, dn, preferred_element_type=F32)
             + lax.dot_general(qr, kr, dn, preferred_element_type=F32))
        if mask is not None:
            s = jnp.where(mask, s, NEG_INF)
        m = m_ref[...]
        m_new = jnp.maximum(m, jnp.max(s, axis=-1, keepdims=True))
        p = jnp.exp(s - m_new)
        corr = jnp.exp(m - m_new)
        l_ref[...] = corr * l_ref[...] + jnp.sum(p, axis=-1, keepdims=True)
        acc_ref[...] = corr * acc_ref[...] + jnp.dot(p.astype(BF16), ckv, preferred_element_type=F32)
        m_ref[...] = m_new

    for c_ref, k_ref in zip(ckv_refs, kr_refs):
        update(c_ref[...].astype(BF16), k_ref[...].astype(BF16), None)

    @pl.when(g == pl.num_programs(1) - 1)
    def _():
        rows = q.shape[0]
        npad = cn_ref.shape[0]
        tq = lax.broadcasted_iota(jnp.int32, (rows, npad), 0) % n_new
        kj = lax.broadcasted_iota(jnp.int32, (rows, npad), 1)
        update(cn_ref[...].astype(BF16), kn_ref[...].astype(BF16), kj <= tq)
        o_ref[...] = acc_ref[...] / l_ref[...]


def _paged_attention(page_table, qs, ckv_new, kr_new, cache_ckv, cache_krope, *, pg, n_new):
    Bd, R, QW = qs.shape
    kvlora = cache_ckv.shape[2]
    page = cache_ckv.shape[1]
    n_pages = page_table.shape[1]
    pg = min(pg, n_pages)
    assert n_pages % pg == 0
    npad = ckv_new.shape[1]
    cspec = lambda i: pl.BlockSpec((None, page, kvlora), lambda b, g, pt, i=i: (pt[b, g * pg + i], 0, 0))
    kspec = lambda i: pl.BlockSpec((None, page, ROPE_DIM), lambda b, g, pt, i=i: (pt[b, g * pg + i], 0, 0))
    in_specs = ([pl.BlockSpec((None, R, QW), lambda b, g, pt: (b, 0, 0)),
                 pl.BlockSpec((None, npad, kvlora), lambda b, g, pt: (b, 0, 0)),
                 pl.BlockSpec((None, npad, ROPE_DIM), lambda b, g, pt: (b, 0, 0))]
                + [cspec(i) for i in range(pg)] + [kspec(i) for i in range(pg)])
    grid_spec = pltpu.PrefetchScalarGridSpec(
        num_scalar_prefetch=1, grid=(Bd, n_pages // pg), in_specs=in_specs,
        out_specs=pl.BlockSpec((None, R, kvlora), lambda b, g, pt: (b, 0, 0)),
        scratch_shapes=[pltpu.VMEM((R, 1), F32), pltpu.VMEM((R, 1), F32), pltpu.VMEM((R, kvlora), F32)])
    return pl.pallas_call(
        functools.partial(_paged_kernel, pg=pg, n_new=n_new, kvlora=kvlora),
        grid_spec=grid_spec,
        out_shape=jax.ShapeDtypeStruct((Bd, R, kvlora), F32),
        compiler_params=_cparams("parallel", "arbitrary"),
        name="mla_paged",
    )(page_table, qs, ckv_new, kr_new, *([cache_ckv] * pg), *([cache_krope] * pg))


def _merge_kernel(x_ref, y_ref, bo_ref, g_ref, ga_ref, gb_ref, om_ref, lnw_ref, lnb_ref,
                  e_ref, et_ref, wo_ref, nf_ref, rw_ref, rb_ref,
                  h_ref, hn_ref, ti_ref, tg_ref, *, n_experts):
    e = e_ref[...]
    et = et_ref[...]
    y = y_ref[...]
    inv_n = 1.0 / RWKV_HEAD_DIM
    mean = _dot_sel(_dot_sel(y, e) * inv_n, et)
    yc = y - mean
    var = _dot_sel(_dot_sel(yc * yc, e) * inv_n, et)
    yn = yc * lax.rsqrt(var + GN_EPS) * lnw_ref[...] + lnb_ref[...]
    o_rwkv = (yn + bo_ref[...]) * g_ref[...]
    merged = jax.nn.sigmoid(ga_ref[...]) * o_rwkv + jax.nn.sigmoid(gb_ref[...]) * om_ref[...]
    h = x_ref[...] + jnp.dot(merged.astype(BF16), wo_ref[...], preferred_element_type=F32)
    h_ref[...] = h
    hn = _rms(h, nf_ref[...])
    hn_ref[...] = hn.astype(BF16)

    rw = rw_ref[...]
    w_hi = rw.astype(BF16)
    w_r1 = rw - w_hi.astype(F32)
    w_mid = w_r1.astype(BF16)
    w_lo = (w_r1 - w_mid.astype(F32)).astype(BF16)
    x_hi, x_mid, x_lo = _split3(hn)
    d = lambda a, b: jnp.dot(a, b, preferred_element_type=F32)
    logits = (d(x_hi, w_hi) + (d(x_hi, w_mid) + d(x_mid, w_hi))
              + (d(x_hi, w_lo) + d(x_mid, w_mid) + d(x_lo, w_hi))) + rb_ref[...]
    lane = lax.broadcasted_iota(jnp.int32, logits.shape, 1)
    work = jnp.where(lane < n_experts, logits, -jnp.inf)
    vals, idxs = [], []
    for _ in range(TOP_K):
        mx = jnp.max(work, axis=-1, keepdims=True)
        ix = jnp.min(jnp.where(work == mx, lane, LANES), axis=-1, keepdims=True)
        vals.append(mx)
        idxs.append(ix)
        work = jnp.where(lane == ix, -jnp.inf, work)
    ex = [jnp.exp(vv - vals[0]) for vv in vals]
    den = ex[0] + ex[1] + ex[2] + ex[3]
    ti = jnp.zeros(logits.shape, jnp.int32)
    tg = jnp.zeros(logits.shape, F32)
    for j in range(TOP_K):
        ti = jnp.where(lane == j, idxs[j], ti)
        tg = jnp.where(lane == j, ex[j] / den, tg)
    ti_ref[...] = ti
    tg_ref[...] = tg


def _merge(x, y, bonus, g, p5, o_mla, lnw, lnb, e, et, wo, nf, rw, rb, *, tm, n_experts):
    M, D = x.shape
    tm = _pick_tile(M, tm, 8)
    rowspec = lambda: pl.BlockSpec((tm, D), lambda i: (i, 0))
    full = lambda a: pl.BlockSpec(a.shape, lambda i: (0,) * a.ndim)
    in_specs = [rowspec(), rowspec(), rowspec(), rowspec(),
                pl.BlockSpec((tm, D), lambda i: (i, 3)), pl.BlockSpec((tm, D), lambda i: (i, 4)),
                rowspec(), full(lnw), full(lnb), full(e), full(et), full(wo), full(nf), full(rw),
                full(rb)]
    out_specs = [rowspec(), rowspec(), pl.BlockSpec((tm, LANES), lambda i: (i, 0)),
                 pl.BlockSpec((tm, LANES), lambda i: (i, 0))]
    out_shape = [jax.ShapeDtypeStruct((M, D), F32), jax.ShapeDtypeStruct((M, D), BF16),
                 jax.ShapeDtypeStruct((M, LANES), jnp.int32), jax.ShapeDtypeStruct((M, LANES), F32)]
    return pl.pallas_call(
        functools.partial(_merge_kernel, n_experts=n_experts),
        grid=(M // tm,), in_specs=in_specs, out_specs=out_specs, out_shape=out_shape,
        compiler_params=_cparams("parallel"),
        name="merge_router",
    )(x, y, bonus, g, p5, p5, o_mla, lnw, lnb, e, et, wo, nf, rw, rb)


def _expert_kernel(se_ref, sh_ref, sr_ref, sm_ref, x_ref, w1g_ref, w1l_ref, b1g_ref, b1l_ref,
                   w2_ref, b2_ref, o_ref):
    s = pl.program_id(0)
    mode = sm_ref[s]

    @pl.when(mode == 0)
    def _():
        o_ref[...] = jnp.zeros(o_ref.shape, F32)

    @pl.when(mode == 1)
    def _():
        x = x_ref[...]
        hg = jnp.dot(x, w1g_ref[...], preferred_element_type=F32) + b1g_ref[...]
        hl = jnp.dot(x, w1l_ref[...], preferred_element_type=F32) + b1l_ref[...]
        hg = jnp.minimum(hg, SWIGLU_LIMIT)
        hl = jnp.clip(hl, -SWIGLU_LIMIT, SWIGLU_LIMIT)
        act = hg * jax.nn.sigmoid(SWIGLU_ALPHA * hg) * (hl + 1.0)
        y = jnp.dot(act.astype(BF16), w2_ref[...], preferred_element_type=F32)
        first = (sh_ref[s] == 0).astype(F32)
        o_ref[...] = y + first * b2_ref[...]


def _experts(sched, xs, w1g, w1l, b1g, b1l, w2, b2, *, bm, n_half):
    se, sh, sr, sm = sched
    NR, D = xs.shape
    E, _, F = w1g.shape
    fh = F // n_half
    S = se.shape[0]
    in_specs = [pl.BlockSpec((bm, D), lambda s, se, sh, sr, sm: (sr[s], 0)),
                pl.BlockSpec((None, D, fh), lambda s, se, sh, sr, sm: (se[s], 0, sh[s])),
                pl.BlockSpec((None, D, fh), lambda s, se, sh, sr, sm: (se[s], 0, sh[s])),
                pl.BlockSpec((None, 1, fh), lambda s, se, sh, sr, sm: (se[s], 0, sh[s])),
                pl.BlockSpec((None, 1, fh), lambda s, se, sh, sr, sm: (se[s], 0, sh[s])),
                pl.BlockSpec((None, fh, D), lambda s, se, sh, sr, sm: (se[s], sh[s], 0)),
                pl.BlockSpec((None, 1, D), lambda s, se, sh, sr, sm: (se[s], 0, 0))]
    grid_spec = pltpu.PrefetchScalarGridSpec(
        num_scalar_prefetch=4, grid=(S,), in_specs=in_specs,
        out_specs=pl.BlockSpec((None, bm, D), lambda s, se, sh, sr, sm: (sh[s], sr[s], 0)))
    return pl.pallas_call(
        _expert_kernel, grid_spec=grid_spec,
        out_shape=jax.ShapeDtypeStruct((n_half, NR, D), F32),
        compiler_params=_cparams("arbitrary"),
        name="moe_experts",
    )(se, sh, sr, sm, xs, w1g, w1l, b1g, b1l, w2, b2)


def _combine_kernel(h_ref, yg_ref, tg_ref, nw_ref, o_ref, *, n_half):
    acc = h_ref[...]
    tg = tg_ref[...]
    for j in range(TOP_K):
        rows = yg_ref[0, j]
        for hh in range(1, n_half):
            rows = rows + yg_ref[hh, j]
        acc = acc + rows * tg[:, j:j + 1]
    o_ref[...] = _rms(acc, nw_ref[...])


def _combine(h, yg, tg, nw, *, tm, n_half):
    M, D = h.shape
    tm = _pick_tile(M, tm, 8)
    return pl.pallas_call(
        functools.partial(_combine_kernel, n_half=n_half),
        grid=(M // tm,),
        in_specs=[pl.BlockSpec((tm, D), lambda i: (i, 0)),
                  pl.BlockSpec((n_half, TOP_K, tm, D), lambda i: (0, 0, i, 0)),
                  pl.BlockSpec((tm, LANES), lambda i: (i, 0)),
                  pl.BlockSpec((1, D), lambda i: (0, 0))],
        out_specs=pl.BlockSpec((tm, D), lambda i: (i, 0)),
        out_shape=jax.ShapeDtypeStruct((M, D), F32),
        compiler_params=_cparams("parallel"),
        name="moe_combine",
    )(h, yg, tg, nw)


def _rope_tables(positions):
    inv = ROPE_THETA ** (-jnp.arange(0, ROPE_DIM, 2, dtype=F32) / ROPE_DIM)
    ang = positions.astype(F32)[:, None] * inv[None, :]
    cos, sin = jnp.cos(ang), jnp.sin(ang)
    cos2 = jnp.concatenate([cos, cos], axis=-1)
    sin2 = jnp.concatenate([sin, sin], axis=-1)
    n = positions.shape[0]
    ones = jnp.ones((n, NOPE_DIM), F32)
    z64 = jnp.zeros((n, QK_PAD - NOPE_DIM - ROPE_DIM), F32)
    z128 = jnp.zeros((n, NOPE_DIM), F32)
    qtab = jnp.concatenate([ones, cos2, z64, z128, sin2, z64], axis=-1)
    ktab = jnp.concatenate([cos2, sin2], axis=-1)
    return qtab, ktab


def _rot_cols(w):
    half = ROPE_DIM // 2
    return jnp.concatenate([-w[..., half:], w[..., :half]], axis=-1)


def _moe_schedule(top_idx, n_experts, bm, n_half, n_blocks):
    M = top_idx.shape[0]
    flat_e = top_idx.reshape(-1)
    onehot = (flat_e[:, None] == jnp.arange(n_experts, dtype=jnp.int32)[None, :]).astype(jnp.int32)
    rank = jnp.take_along_axis(jnp.cumsum(onehot, axis=0) - onehot, flat_e[:, None], axis=1)[:, 0]
    counts = jnp.sum(onehot, axis=0)
    nb = (counts + bm - 1) // bm
    blk_end = jnp.cumsum(nb)
    blk_start = blk_end - nb
    dest = blk_start[flat_e] * bm + rank
    flat_tok = jnp.repeat(jnp.arange(M, dtype=jnp.int32), TOP_K)
    row_tok = jnp.full((n_blocks * bm,), M, jnp.int32).at[dest].set(flat_tok)
    n_real = blk_end[-1]
    steps_end = jnp.cumsum(nb * n_half)
    s = jnp.arange(n_blocks * n_half, dtype=jnp.int32)
    real = s < n_real * n_half
    e_of = jnp.minimum(jnp.searchsorted(steps_end, s, side='right'), n_experts - 1).astype(jnp.int32)
    local = s - (steps_end - nb * n_half)[e_of]
    nb_e = jnp.maximum(nb[e_of], 1)
    h_real = local // nb_e
    r_real = blk_start[e_of] + local % nb_e
    last_e = e_of[jnp.maximum(n_real * n_half - 1, 0)]
    tail = s - n_real * n_half
    n_tail = jnp.maximum(n_blocks - n_real, 1)
    se = jnp.where(real, e_of, last_e).astype(jnp.int32)
    sh = jnp.where(real, h_real, n_half - 1 - tail // n_tail).astype(jnp.int32)
    sr = jnp.where(real, r_real, n_real + tail % n_tail).astype(jnp.int32)
    sm = real.astype(jnp.int32)
    return dest, row_tok, (se, sh, sr, sm)


def kernel(x_prompt, x_sample, cache_ckv, cache_krope, state_wkv, state_shift, page_table, norm_mix_w, w_in, rwkv_mu, rwkv_w0, rwkv_w2, rwkv_a0, rwkv_a2, rwkv_g2, rwkv_k_k, rwkv_k_a, rwkv_r_k, rwkv_ln_w, rwkv_ln_b, mla_q_norm_w, mla_w_uq, mla_kv_norm_w, mla_w_uk, mla_w_uv, w_out, norm_ffn_w, router_w, router_b, expert_w1, expert_b1, expert_w2, expert_b2, norm_final_w):
    B, T, D = x_prompt.shape
    Bd, Td, _ = x_sample.shape
    H = D // RWKV_HEAD_DIM
    HD = RWKV_HEAD_DIM
    dlora, alora, glora = rwkv_w2.shape[0], rwkv_a2.shape[0], rwkv_g2.shape[0]
    qlora, Hm, _ = mla_w_uq.shape
    kvlora = mla_w_uk.shape[0]
    n_experts = router_w.shape[1]
    F = expert_w2.shape[1]
    page = cache_ckv.shape[1]
    past = page_table.shape[1] * page
    scale = float((NOPE_DIM + ROPE_DIM) ** -0.5)
    row2 = lambda a: a.reshape(1, -1).astype(F32)

    sizes = [D, D, D, dlora, alora, glora, qlora, kvlora, ROPE_DIM, D, D]
    offs = np.concatenate([[0], np.cumsum(sizes)]).tolist()
    col = lambda i: w_in[:, offs[i]:offs[i + 1]]
    lw_pad, la_pad = _round_up(dlora, LANES), _round_up(alora, LANES)
    padc = lambda a, n: jnp.pad(a, ((0, 0), (0, n - a.shape[1])))
    w5 = jnp.concatenate([col(0), col(1), col(2), col(9), col(10)], axis=1).astype(BF16)
    lora_w = lw_pad + la_pad + glora
    ws = jnp.concatenate([padc(col(3), lw_pad), padc(col(4), la_pad), col(5), col(6), col(7),
                          col(8), _rot_cols(col(8))], axis=1)
    col_cq, col_ckv, col_kr = lora_w, lora_w + qlora, lora_w + qlora + kvlora
    ws = padc(ws, _round_up(ws.shape[1], LANES)).astype(BF16)
    mu = rwkv_mu
    mur, muk, muv = row2(mu[:D]), row2(mu[D:2 * D]), row2(mu[2 * D:3 * D])
    m0 = 3 * D
    mul = jnp.concatenate([jnp.pad(mu[m0:m0 + dlora], (0, lw_pad - dlora)),
                           jnp.pad(mu[m0 + dlora:m0 + dlora + alora], (0, la_pad - alora)),
                           mu[m0 + dlora + alora:]]).reshape(1, -1)
    w2p = jnp.pad(rwkv_w2, ((0, lw_pad - dlora), (0, 0))).astype(BF16)
    a2p = jnp.pad(rwkv_a2, ((0, la_pad - alora), (0, 0))).astype(BF16)
    head_of = jnp.arange(D, dtype=jnp.int32) // HD
    e_sel = (head_of[:, None] == jnp.arange(LANES, dtype=jnp.int32)[None, :]).astype(BF16)
    et_sel = e_sel.T
    prep_consts = (mur, muk, muv, mul, row2(rwkv_w0), w2p, row2(rwkv_a0), a2p, rwkv_g2.astype(BF16),
                   row2(rwkv_k_k), row2(rwkv_k_a), row2(rwkv_r_k), e_sel, et_sel, lw_pad, la_pad)

    zq = jnp.zeros((qlora, Hm, QK_PAD - NOPE_DIM - ROPE_DIM), F32)
    q_nope_w, q_rope_w = mla_w_uq[..., :NOPE_DIM], mla_w_uq[..., NOPE_DIM:]
    wqa = jnp.concatenate([q_nope_w, q_rope_w, zq], axis=-1).reshape(qlora, Hm * QK_PAD).astype(BF16)
    wqb = jnp.concatenate([jnp.zeros_like(q_nope_w), _rot_cols(q_rope_w), zq],
                          axis=-1).reshape(qlora, Hm * QK_PAD).astype(BF16)
    zk = jnp.zeros((kvlora, Hm, QK_PAD - NOPE_DIM), F32)
    wk_top = jnp.concatenate([mla_w_uk, zk], axis=-1).reshape(kvlora, Hm * QK_PAD)
    sel = jnp.concatenate([jnp.zeros((ROPE_DIM, NOPE_DIM), F32), jnp.eye(ROPE_DIM, dtype=F32),
                           jnp.zeros((ROPE_DIM, QK_PAD - NOPE_DIM - ROPE_DIM), F32)], axis=-1)
    wk = jnp.concatenate([wk_top, jnp.tile(sel, (1, Hm))], axis=0).astype(BF16)
    wv = mla_w_uv.reshape(kvlora, Hm * V_DIM).astype(BF16)
    w_uk_t = jnp.transpose(mla_w_uk, (1, 2, 0)).astype(BF16)
    w_uv_h = jnp.transpose(mla_w_uv, (1, 0, 2)).astype(BF16)
    wo = w_out.astype(BF16)
    rw = jnp.pad(router_w, ((0, 0), (0, LANES - n_experts)))
    rb = jnp.pad(router_b, (0, LANES - n_experts)).reshape(1, LANES)
    w1g = expert_w1[:, :, 0::2].astype(BF16)
    w1l = expert_w1[:, :, 1::2].astype(BF16)
    b1g = expert_b1[:, None, 0::2]
    b1l = expert_b1[:, None, 1::2]
    w2e = expert_w2.astype(BF16)
    b2e = expert_b2[:, None, :]
    nmw = row2(norm_mix_w)

    def token_stage(x2d, shift_rows, n_groups, n_tiles, tt, shift, positions, tm_mla):
        p5 = _project(x2d, nmw, w5, normalize=True, tm=1024, tn=512)
        ps = _project(x2d, nmw, ws, normalize=True, tm=1024, tn=ws.shape[1])
        if shift_rows is None:
            s5 = jnp.zeros((n_groups, shift, w5.shape[1]), F32)
            ss = jnp.zeros((n_groups, shift, ws.shape[1]), F32)
        else:
            s5 = _project(shift_rows, nmw, w5, normalize=False, tm=1024, tn=512)
            ss = _project(shift_rows, nmw, ws, normalize=False, tm=1024, tn=ws.shape[1])
            s5 = s5.reshape(n_groups, shift, -1)
            ss = ss.reshape(n_groups, shift, -1)
        prep = _rwkv_prep(p5, ps, s5, ss, prep_consts, n_groups=n_groups, n_tiles=n_tiles, tt=tt,
                          shift=shift, D=D, lora_w=lora_w)
        qtab, ktab = _rope_tables(positions)
        mla = _mla_prep(ps, qtab, ktab, row2(mla_q_norm_w), row2(mla_kv_norm_w), wqa, wqb, wk, wv,
                        col_cq=col_cq, col_ckv=col_ckv, col_kr=col_kr, qlora=qlora, kvlora=kvlora,
                        n_heads=Hm, hg=min(4, Hm), tm=tm_mla, scale=scale)
        return p5, prep, mla

    xp = x_prompt.reshape(B * T, D)
    tt_p = min(128, T)
    p5_p, prep_p, mla_p = token_stage(xp, None, B, T // tt_p, tt_p, 1, jnp.arange(T), min(512, T))
    r_p, lw_p, k_p, v_p, al_p, be_p, g_p, bo_p = prep_p
    vsplit = 2 if (B * H * 2) % LANES == 0 and B * H < LANES else 1
    vr = HD // vsplit

    def to_kl(a):
        a = a.reshape(B, T, H, HD).transpose(1, 3, 0, 2).reshape(T, HD, B * H)
        return jnp.tile(a, (1, 1, vsplit))

    def to_vl(a):
        a = a.reshape(B, T, H, vsplit, vr).transpose(1, 4, 3, 0, 2)
        return a.reshape(T, vr, vsplit * B * H)

    nl_p = vsplit * B * H
    pad_l = _round_up(nl_p, LANES) - nl_p
    padl = lambda a: jnp.pad(a, ((0, 0), (0, 0), (0, pad_l))) if pad_l else a
    s0_p = jnp.zeros((vr, HD, nl_p + pad_l), F32)
    y_kl, s_kl = _rwkv_scan(padl(to_kl(r_p)), padl(to_kl(lw_p)), padl(to_kl(k_p)), padl(to_vl(v_p)),
                            padl(to_kl(al_p)), padl(to_kl(be_p)), s0_p, tb=64)
    y_p = y_kl[:, :, :nl_p].reshape(T, vr, vsplit, B, H).transpose(3, 0, 4, 2, 1).reshape(B * T, D)
    wkv_p = s_kl[:, :, :nl_p].reshape(vr, HD, vsplit, B, H).transpose(3, 4, 2, 0, 1).reshape(B, H, HD, HD)

    q_p, kf_p, vf_p, ckv_p, kr_p = mla_p
    o_mla_p = _flash_attention(q_p, kf_p, vf_p, B=B, T=T, n_heads=Hm, tq=512, tk=512)

    xs_tm = jnp.swapaxes(x_sample, 0, 1).reshape(Td * Bd, D)
    pos_s = jnp.repeat(past + jnp.arange(Td), Bd)
    p5_s, prep_s, mla_s = token_stage(xs_tm, state_shift, 1, Td, Bd, Bd, pos_s, Td * Bd)
    r_s, lw_s, k_s, v_s, al_s, be_s, g_s, bo_s = prep_s

    def to_kl_s(a):
        return a.reshape(Td, Bd, H, HD).transpose(0, 3, 1, 2).reshape(Td, HD, Bd * H)

    nl_s = Bd * H
    pad_s = _round_up(nl_s, LANES) - nl_s
    pads = lambda a: jnp.pad(a, ((0, 0), (0, 0), (0, pad_s))) if pad_s else a
    s0_s = pads(state_wkv.transpose(2, 3, 0, 1).reshape(HD, HD, nl_s))
    y_sl, s_sl = _rwkv_scan(pads(to_kl_s(r_s)), pads(to_kl_s(lw_s)), pads(to_kl_s(k_s)),
                            pads(to_kl_s(v_s)), pads(to_kl_s(al_s)), pads(to_kl_s(be_s)), s0_s, tb=Td)
    y_s = y_sl[:, :, :nl_s].reshape(Td, HD, Bd, H).transpose(0, 2, 3, 1).reshape(Td * Bd, D)
    wkv_s = s_sl[:, :, :nl_s].reshape(HD, HD, Bd, H).transpose(2, 3, 0, 1)

    q_s, _, _, ckv_s, kr_s = mla_s
    q4 = q_s.reshape(Td, Bd, Hm, QK_PAD)
    qn_h = q4[..., :NOPE_DIM].transpose(2, 1, 0, 3).reshape(Hm, Bd * Td, NOPE_DIM)
    q_lat = _head_matmul(qn_h, w_uk_t)
    q_lat = q_lat.reshape(Hm, Bd, Td, kvlora).transpose(1, 0, 2, 3).reshape(Bd, Hm * Td, kvlora)
    q_rp = q4[..., NOPE_DIM:NOPE_DIM + ROPE_DIM].transpose(1, 2, 0, 3).reshape(Bd, Hm * Td, ROPE_DIM)
    qs = jnp.concatenate([q_lat.astype(BF16), q_rp], axis=-1)
    ckv_s_bt = ckv_s.reshape(Td, Bd, kvlora).transpose(1, 0, 2)
    kr_s_bt = kr_s.reshape(Td, Bd, ROPE_DIM).transpose(1, 0, 2)
    npad = _round_up(Td, 8)
    padn = lambda a: jnp.pad(a, ((0, 0), (0, npad - Td), (0, 0)))
    o_lat = _paged_attention(page_table, qs, padn(ckv_s_bt), padn(kr_s_bt), cache_ckv, cache_krope,
                             pg=8, n_new=Td)
    o_lat_h = o_lat.reshape(Bd, Hm, Td, kvlora).transpose(1, 2, 0, 3).reshape(Hm, Td * Bd, kvlora)
    o_mla_s = _head_matmul(o_lat_h, w_uv_h)
    o_mla_s = o_mla_s.transpose(1, 0, 2).reshape(Td * Bd, Hm * V_DIM)

    mg = functools.partial(_merge, lnw=row2(rwkv_ln_w), lnb=row2(rwkv_ln_b), e=e_sel, et=et_sel, wo=wo,
                           nf=row2(norm_ffn_w), rw=rw, rb=rb, tm=128, n_experts=n_experts)
    h_p, hn_p, ti_p, tg_p = mg(xp, y_p, bo_p, g_p, p5_p, o_mla_p)
    h_s, hn_s, ti_s, tg_s = mg(xs_tm, y_s, bo_s, g_s, p5_s, o_mla_s)
    h_all = jnp.concatenate([h_p, h_s], axis=0)
    hn_all = jnp.concatenate([hn_p, hn_s], axis=0)
    ti_all = jnp.concatenate([ti_p, ti_s], axis=0)[:, :TOP_K]
    tg_all = jnp.concatenate([tg_p, tg_s], axis=0)
    M = h_all.shape[0]
    bm = 256
    n_half = 2 if F % (2 * LANES) == 0 else 1
    n_blocks = (M * TOP_K + n_experts * (bm - 1)) // bm
    dest, row_tok, sched = _moe_schedule(ti_all, n_experts, bm, n_half, n_blocks)
    hn_pad = jnp.concatenate([hn_all, jnp.zeros((1, D), BF16)], axis=0)
    xs_rows = hn_pad[row_tok]
    y_halves = _experts(sched, xs_rows, w1g, w1l, b1g, b1l, w2e, b2e, bm=bm, n_half=n_half)
    yg = y_halves[:, dest].reshape(n_half, M, TOP_K, D).transpose(0, 2, 1, 3)
    y_all = _combine(h_all, yg, tg_all, row2(norm_final_w), tm=128, n_half=n_half)

    y_prompt = y_all[:B * T].reshape(B, T, D)
    y_sample = y_all[B * T:].reshape(Td, Bd, D).transpose(1, 0, 2)
    last = jnp.concatenate([x_prompt[:, -1], x_sample[:, -1]], axis=0)
    shift_out = _rmsnorm_rows(last, nmw)
    return (y_prompt, y_sample,
            ckv_p.reshape(B, T // page, page, kvlora), kr_p.reshape(B, T // page, page, ROPE_DIM),
            wkv_p, shift_out[:B],
            ckv_s_bt, kr_s_bt, wkv_s, shift_out[B:])
```

```python
import functools

import numpy as np
import jax
import jax.numpy as jnp
from jax import lax
from jax.experimental import pallas as pl
from jax.experimental.pallas import tpu as pltpu

F32 = jnp.float32
BF16 = jnp.bfloat16

NORM_EPS = 1e-6
GN_EPS = 64e-5
RWKV_HEAD_DIM = 64
NOPE_DIM = 128
ROPE_DIM = 64
V_DIM = 128
QK_PAD = 256
ROPE_THETA = 10000.0
NEG_INF = -1e30
TOP_K = 4
SWIGLU_ALPHA = 1.702
SWIGLU_LIMIT = 7.0
LANES = 128
VMEM_LIMIT = 56 * 1024 * 1024


def _round_up(n, m):
    return (n + m - 1) // m * m


def _pick_tile(n, target, mult):
    if n <= target:
        return n
    t = target // mult * mult
    while n % t:
        t -= mult
    return t


def _cparams(*sem):
    return pltpu.CompilerParams(dimension_semantics=sem, vmem_limit_bytes=VMEM_LIMIT)


def _split3(x):
    hi = x.astype(BF16)
    r1 = x - hi.astype(F32)
    mid = r1.astype(BF16)
    lo = (r1 - mid.astype(F32)).astype(BF16)
    return hi, mid, lo


def _dot_sel(x, sel_bf16):
    hi, mid, lo = _split3(x)
    d = lambda a: jnp.dot(a, sel_bf16, preferred_element_type=F32)
    return d(hi) + d(mid) + d(lo)


def _rms(x, w):
    return x * lax.rsqrt(jnp.mean(x * x, axis=-1, keepdims=True) + NORM_EPS) * w


def _proj_kernel(x_ref, nw_ref, w_ref, o_ref, xn_ref, *, normalize):
    @pl.when(pl.program_id(1) == 0)
    def _():
        x = x_ref[...]
        if normalize:
            x = _rms(x, nw_ref[...])
        xn_ref[...] = x.astype(BF16)

    o_ref[...] = jnp.dot(xn_ref[...], w_ref[...], preferred_element_type=F32)


def _project(x, norm_w, w, *, normalize, tm, tn):
    M, D = x.shape
    N = w.shape[1]
    tm = _pick_tile(M, tm, 8)
    tn = _pick_tile(N, tn, LANES)
    return pl.pallas_call(
        functools.partial(_proj_kernel, normalize=normalize),
        grid=(M // tm, N // tn),
        in_specs=[pl.BlockSpec((tm, D), lambda i, j: (i, 0)),
                  pl.BlockSpec((1, D), lambda i, j: (0, 0)),
                  pl.BlockSpec((D, tn), lambda i, j: (0, j))],
        out_specs=pl.BlockSpec((tm, tn), lambda i, j: (i, j)),
        out_shape=jax.ShapeDtypeStruct((M, N), F32),
        scratch_shapes=[pltpu.VMEM((tm, D), BF16)],
        compiler_params=_cparams("parallel", "arbitrary"),
        name="norm_proj",
    )(x, norm_w, w)


def _rmsnorm_rows_kernel(x_ref, w_ref, o_ref):
    o_ref[...] = _rms(x_ref[...], w_ref[...])


def _rmsnorm_rows(x, w):
    return pl.pallas_call(
        _rmsnorm_rows_kernel,
        out_shape=jax.ShapeDtypeStruct(x.shape, F32),
        name="rmsnorm_rows",
    )(x, w)


def _rwkv_prep_kernel(r_ref, k_ref, v_ref, l_ref, sr_ref, sk_ref, sv_ref, sl_ref,
                      mur_ref, muk_ref, muv_ref, mul_ref, w0_ref, w2_ref, a0_ref, a2_ref, g2_ref,
                      kk_ref, ka_ref, rk_ref, e_ref, et_ref,
                      ro_ref, lw_ref, ko_ref, vo_ref, al_ref, be_ref, go_ref, bo_ref,
                      cr_ref, ck_ref, cv_ref, cl_ref, *, shift, lw_pad, la_pad):
    t = pl.program_id(1)

    @pl.when(t == 0)
    def _():
        cr_ref[...] = sr_ref[...]
        ck_ref[...] = sk_ref[...]
        cv_ref[...] = sv_ref[...]
        cl_ref[...] = sl_ref[...]

    def lerp(p_ref, c_ref, mu_ref):
        p = p_ref[...]
        if shift == 1:
            rolled = pltpu.roll(p, 1, axis=0)
            row = lax.broadcasted_iota(jnp.int32, p.shape, 0)
            prev = jnp.where(row == 0, c_ref[...], rolled)
            c_new = p[p.shape[0] - 1:, :]
        else:
            prev = c_ref[...]
            c_new = p
        xx = p + (prev - p) * mu_ref[...]
        c_ref[...] = c_new
        return xx

    r = lerp(r_ref, cr_ref, mur_ref)
    k = lerp(k_ref, ck_ref, muk_ref)
    v = lerp(v_ref, cv_ref, muv_ref)
    lo = lerp(l_ref, cl_ref, mul_ref)
    hw = lo[:, :lw_pad]
    ha = lo[:, lw_pad:lw_pad + la_pad]
    hg = lo[:, lw_pad + la_pad:]

    mm = lambda a, b_ref: jnp.dot(a.astype(BF16), b_ref[...], preferred_element_type=F32)
    w = -jax.nn.softplus(-(w0_ref[...] + mm(jnp.tanh(hw), w2_ref))) - 0.5
    a = jax.nn.sigmoid(a0_ref[...] + mm(ha, a2_ref))
    g = mm(jax.nn.sigmoid(hg), g2_ref)

    e = e_ref[...]
    et = et_ref[...]
    kk = k * kk_ref[...]
    ssq = _dot_sel(kk * kk, e)
    inv = lax.rsqrt(jnp.maximum(ssq, 1e-24))
    kk = kk * _dot_sel(inv, et)
    k2 = k * (1.0 + (a - 1.0) * ka_ref[...])
    bsum = _dot_sel(_dot_sel(r * k2 * rk_ref[...], e), et)

    ro_ref[...] = r
    lw_ref[...] = -jnp.exp(w)
    ko_ref[...] = k2
    vo_ref[...] = v
    al_ref[...] = -kk
    be_ref[...] = kk * a
    go_ref[...] = g
    bo_ref[...] = bsum * v


def _rwkv_prep(p5, ps, shift5, shifts, consts, *, n_groups, n_tiles, tt, shift, D, lora_w):
    M = p5.shape[0]
    (mur, muk, muv, mul, w0, w2p, a0, a2p, g2, k_k, k_a, r_k, e, et, lw_pad, la_pad) = consts
    row = lambda g, t: (g * n_tiles + t)
    big = lambda c: pl.BlockSpec((tt, D), lambda g, t, c=c: (row(g, t), c))
    sh = lambda c: pl.BlockSpec((None, shift, D), lambda g, t, c=c: (g, 0, c))
    full = lambda a: pl.BlockSpec(a.shape, lambda g, t: (0,) * a.ndim)
    in_specs = [big(0), big(1), big(2),
                pl.BlockSpec((tt, lora_w), lambda g, t: (row(g, t), 0)),
                sh(0), sh(1), sh(2),
                pl.BlockSpec((None, shift, lora_w), lambda g, t: (g, 0, 0)),
                full(mur), full(muk), full(muv), full(mul), full(w0), full(w2p), full(a0),
                full(a2p), full(g2), full(k_k), full(k_a), full(r_k), full(e), full(et)]
    out_spec = pl.BlockSpec((tt, D), lambda g, t: (row(g, t), 0))
    outs = pl.pallas_call(
        functools.partial(_rwkv_prep_kernel, shift=shift, lw_pad=lw_pad, la_pad=la_pad),
        grid=(n_groups, n_tiles),
        in_specs=in_specs,
        out_specs=[out_spec] * 8,
        out_shape=[jax.ShapeDtypeStruct((M, D), F32)] * 8,
        scratch_shapes=[pltpu.VMEM((shift, D), F32)] * 3 + [pltpu.VMEM((shift, lora_w), F32)],
        compiler_params=_cparams("parallel", "arbitrary"),
        name="rwkv_prep",
    )(p5, p5, p5, ps, shift5, shift5, shift5, shifts,
      mur, muk, muv, mul, w0, w2p, a0, a2p, g2, k_k, k_a, r_k, e, et)
    return outs


def _rwkv_scan_kernel(r_ref, w_ref, k_ref, v_ref, a_ref, b_ref, s0_ref, y_ref, sf_ref, s_ref,
                      *, tb, vr):
    tblk = pl.program_id(1)

    @pl.when(tblk == 0)
    def _():
        s_ref[...] = s0_ref[...]

    def step(t, carry):
        r = r_ref[t]
        w = jnp.exp(w_ref[t])
        k = k_ref[t]
        a = a_ref[t]
        b = b_ref[t]
        vt = v_ref[t]
        rows = []
        for i in range(vr):
            s = s_ref[i]
            sa = jnp.sum(s * a, axis=0, keepdims=True)
            s = s * w + sa * b + vt[i:i + 1, :] * k
            s_ref[i] = s
            rows.append(jnp.sum(s * r, axis=0, keepdims=True))
        y_ref[t] = jnp.concatenate(rows, axis=0)
        return carry

    lax.fori_loop(0, tb, step, 0)

    @pl.when(tblk == pl.num_programs(1) - 1)
    def _():
        sf_ref[...] = s_ref[...]


def _rwkv_scan(r, w, k, v, a, b, s0, *, tb):
    T, K, NL = r.shape
    VR = v.shape[1]
    tb = min(tb, T)
    assert T % tb == 0 and NL % LANES == 0
    kspec = pl.BlockSpec((tb, K, LANES), lambda n, t: (t, 0, n))
    vspec = pl.BlockSpec((tb, VR, LANES), lambda n, t: (t, 0, n))
    sspec = pl.BlockSpec((VR, K, LANES), lambda n, t: (0, 0, n))
    return pl.pallas_call(
        functools.partial(_rwkv_scan_kernel, tb=tb, vr=VR),
        grid=(NL // LANES, T // tb),
        in_specs=[kspec, kspec, kspec, vspec, kspec, kspec, sspec],
        out_specs=[vspec, sspec],
        out_shape=[jax.ShapeDtypeStruct((T, VR, NL), F32),
                   jax.ShapeDtypeStruct((VR, K, NL), F32)],
        scratch_shapes=[pltpu.VMEM((VR, K, LANES), F32)],
        compiler_params=_cparams("parallel", "arbitrary"),
        name="rwkv_scan",
    )(r, w, k, v, a, b, s0)


def _mla_prep_kernel(cq_ref, ckv_ref, kr_ref, qtab_ref, ktab_ref, qn_ref, kvn_ref,
                     wqa_ref, wqb_ref, wk_ref, wv_ref,
                     q_ref, k_ref, v_ref, ckv_o_ref, kr_o_ref, cqn_ref, ckk_ref, *, hg, scale):
    @pl.when(pl.program_id(1) == 0)
    def _():
        cqn_ref[...] = _rms(cq_ref[...], qn_ref[...]).astype(BF16)
        ckv = _rms(ckv_ref[...], kvn_ref[...])
        ckv_o_ref[...] = ckv
        kr = kr_ref[...]
        ktab = ktab_ref[...]
        krope = kr[:, :ROPE_DIM] * ktab[:, :ROPE_DIM] + kr[:, ROPE_DIM:] * ktab[:, ROPE_DIM:]
        kr_o_ref[...] = krope
        ckk_ref[...] = jnp.concatenate([ckv, krope], axis=-1).astype(BF16)

    cqn = cqn_ref[...]
    qtab = qtab_ref[...]
    cosf = jnp.concatenate([qtab[:, :QK_PAD]] * hg, axis=-1)
    sinf = jnp.concatenate([qtab[:, QK_PAD:]] * hg, axis=-1)
    qa = jnp.dot(cqn, wqa_ref[...], preferred_element_type=F32)
    qb = jnp.dot(cqn, wqb_ref[...], preferred_element_type=F32)
    q_ref[...] = ((qa * cosf + qb * sinf) * scale).astype(BF16)
    ckk = ckk_ref[...]
    k_ref[...] = jnp.dot(ckk, wk_ref[...], preferred_element_type=F32).astype(BF16)
    v_ref[...] = jnp.dot(ckk[:, :ckk.shape[1] - ROPE_DIM], wv_ref[...],
                         preferred_element_type=F32).astype(BF16)


def _mla_prep(ps, qtab, ktab, qn, kvn, wqa, wqb, wk, wv, *, col_cq, col_ckv, col_kr, qlora, kvlora,
              n_heads, hg, tm, scale):
    M = ps.shape[0]
    tm = _pick_tile(M, tm, 8)
    n_tab = qtab.shape[0] // tm
    grid = (M // tm, n_heads // hg)
    tabspec = lambda a: pl.BlockSpec((tm, a.shape[1]), lambda i, j: (i % n_tab, 0))
    full = lambda a: pl.BlockSpec(a.shape, lambda i, j: (0,) * a.ndim)
    in_specs = [pl.BlockSpec((tm, qlora), lambda i, j: (i, col_cq // qlora)),
                pl.BlockSpec((tm, kvlora), lambda i, j: (i, col_ckv // kvlora)),
                pl.BlockSpec((tm, 2 * ROPE_DIM), lambda i, j: (i, col_kr // (2 * ROPE_DIM))),
                tabspec(qtab), tabspec(ktab), full(qn), full(kvn),
                pl.BlockSpec((qlora, hg * QK_PAD), lambda i, j: (0, j)),
                pl.BlockSpec((qlora, hg * QK_PAD), lambda i, j: (0, j)),
                pl.BlockSpec((kvlora + ROPE_DIM, hg * QK_PAD), lambda i, j: (0, j)),
                pl.BlockSpec((kvlora, hg * V_DIM), lambda i, j: (0, j))]
    out_specs = [pl.BlockSpec((tm, hg * QK_PAD), lambda i, j: (i, j)),
                 pl.BlockSpec((tm, hg * QK_PAD), lambda i, j: (i, j)),
                 pl.BlockSpec((tm, hg * V_DIM), lambda i, j: (i, j)),
                 pl.BlockSpec((tm, kvlora), lambda i, j: (i, 0)),
                 pl.BlockSpec((tm, ROPE_DIM), lambda i, j: (i, 0))]
    out_shape = [jax.ShapeDtypeStruct((M, n_heads * QK_PAD), BF16),
                 jax.ShapeDtypeStruct((M, n_heads * QK_PAD), BF16),
                 jax.ShapeDtypeStruct((M, n_heads * V_DIM), BF16),
                 jax.ShapeDtypeStruct((M, kvlora), F32),
                 jax.ShapeDtypeStruct((M, ROPE_DIM), F32)]
    return pl.pallas_call(
        functools.partial(_mla_prep_kernel, hg=hg, scale=scale),
        grid=grid, in_specs=in_specs, out_specs=out_specs, out_shape=out_shape,
        scratch_shapes=[pltpu.VMEM((tm, qlora), BF16), pltpu.VMEM((tm, kvlora + ROPE_DIM), BF16)],
        compiler_params=_cparams("parallel", "arbitrary"),
        name="mla_prep",
    )(ps, ps, ps, qtab, ktab, qn, kvn, wqa, wqb, wk, wv)


def _flash_kernel(q_ref, k_ref, v_ref, o_ref, *, tq, tk):
    i = pl.program_id(2)
    q = q_ref[...]
    n_sub = tq // tk

    def block(j, carry, masked):
        m, l, acc = carry
        start = pl.multiple_of(j * tk, tk)
        kb = k_ref[pl.ds(start, tk), :]
        vb = v_ref[pl.ds(start, tk), :]
        s = lax.dot_general(q, kb, (((1,), (1,)), ((), ())), preferred_element_type=F32)
        if masked:
            qpos = i * tq + lax.broadcasted_iota(jnp.int32, s.shape, 0)
            kpos = j * tk + lax.broadcasted_iota(jnp.int32, s.shape, 1)
            s = jnp.where(kpos <= qpos, s, NEG_INF)
        m_new = jnp.maximum(m, jnp.max(s, axis=-1, keepdims=True))
        p = jnp.exp(s - m_new)
        corr = jnp.exp(m - m_new)
        l = corr * l + jnp.sum(p, axis=-1, keepdims=True)
        acc = corr * acc + jnp.dot(p.astype(BF16), vb, preferred_element_type=F32)
        return m_new, l, acc

    init = (jnp.full((tq, 1), NEG_INF, F32), jnp.zeros((tq, 1), F32),
            jnp.zeros((tq, v_ref.shape[1]), F32))
    carry = lax.fori_loop(0, i * n_sub, lambda j, c: block(j, c, False), init)
    for d in range(n_sub):
        carry = block(i * n_sub + d, carry, True)
    m, l, acc = carry
    o_ref[...] = (acc / l).astype(o_ref.dtype)


def _flash_attention(q, k, v, *, B, T, n_heads, tq, tk):
    tq = min(tq, T)
    tk = min(tk, tq)
    nq = T // tq
    return pl.pallas_call(
        functools.partial(_flash_kernel, tq=tq, tk=tk),
        grid=(B, n_heads, nq),
        in_specs=[pl.BlockSpec((tq, QK_PAD), lambda b, h, i: (b * nq + i, h)),
                  pl.BlockSpec((T, QK_PAD), lambda b, h, i: (b, h)),
                  pl.BlockSpec((T, V_DIM), lambda b, h, i: (b, h))],
        out_specs=pl.BlockSpec((tq, V_DIM), lambda b, h, i: (b * nq + i, h)),
        out_shape=jax.ShapeDtypeStruct((B * T, n_heads * V_DIM), F32),
        compiler_params=_cparams("parallel", "parallel", "arbitrary"),
        name="mla_flash",
    )(q, k, v)


def _bmm_kernel(x_ref, w_ref, o_ref):
    o_ref[...] = jnp.dot(x_ref[...].astype(BF16), w_ref[...], preferred_element_type=F32)


def _head_matmul(x, w):
    H, R, K = x.shape
    N = w.shape[2]
    return pl.pallas_call(
        _bmm_kernel,
        grid=(H,),
        in_specs=[pl.BlockSpec((None, R, K), lambda h: (h, 0, 0)),
                  pl.BlockSpec((None, K, N), lambda h: (h, 0, 0))],
        out_specs=pl.BlockSpec((None, R, N), lambda h: (h, 0, 0)),
        out_shape=jax.ShapeDtypeStruct((H, R, N), F32),
        compiler_params=_cparams("parallel"),
        name="head_matmul",
    )(x, w)


def _paged_kernel(pt_ref, q_ref, cn_ref, kn_ref, *refs, pg, n_new, kvlora):
    ckv_refs = refs[:pg]
    kr_refs = refs[pg:2 * pg]
    o_ref = refs[2 * pg]
    m_ref, l_ref, acc_ref = refs[2 * pg + 1:]
    g = pl.program_id(1)

    @pl.when(g == 0)
    def _():
        m_ref[...] = jnp.full(m_ref.shape, NEG_INF, F32)
        l_ref[...] = jnp.zeros(l_ref.shape, F32)
        acc_ref[...] = jnp.zeros(acc_ref.shape, F32)

    q = q_ref[...]
    ql = q[:, :kvlora]
    qr = q[:, kvlora:]
    dn = (((1,), (1,)), ((), ()))

    def scores(c_ref, k_ref):
        return (lax.dot_general(ql, c_ref[...].astype(BF16), dn, preferred_element_type=F32)
                + lax.dot_general(qr, k_ref[...].astype(BF16), dn, preferred_element_type=F32))

    def update(s, value_refs):
        m = m_ref[...]
        m_new = jnp.maximum(m, jnp.max(s, axis=-1, keepdims=True))
        p = jnp.exp(s - m_new).astype(BF16)
        corr = jnp.exp(m - m_new)
        l_ref[...] = corr * l_ref[...] + jnp.sum(p.astype(F32), axis=-1, keepdims=True)
        pv = None
        off = 0
        for c_ref in value_refs:
            n = c_ref.shape[0]
            part = jnp.dot(p[:, off:off + n], c_ref[...].astype(BF16), preferred_element_type=F32)
            pv = part if pv is None else pv + part
            off += n
        acc_ref[...] = corr * acc_ref[...] + pv
        m_ref[...] = m_new

    s_all = jnp.concatenate([scores(c, k) for c, k in zip(ckv_refs, kr_refs)], axis=-1)
    update(s_all, ckv_refs)

    @pl.when(g == pl.num_programs(1) - 1)
    def _():
        rows = q.shape[0]
        npad = cn_ref.shape[0]
        tq = lax.broadcasted_iota(jnp.int32, (rows, npad), 0) % n_new
        kj = lax.broadcasted_iota(jnp.int32, (rows, npad), 1)
        update(jnp.where(kj <= tq, scores(cn_ref, kn_ref), NEG_INF), [cn_ref])
        o_ref[...] = acc_ref[...] / l_ref[...]


def _paged_attention(page_table, qs, ckv_new, kr_new, cache_ckv, cache_krope, *, pg, n_new):
    Bd, R, QW = qs.shape
    kvlora = cache_ckv.shape[2]
    page = cache_ckv.shape[1]
    n_pages = page_table.shape[1]
    pg = min(pg, n_pages)
    assert n_pages % pg == 0
    npad = ckv_new.shape[1]
    cspec = lambda i: pl.BlockSpec((None, page, kvlora), lambda b, g, pt, i=i: (pt[b, g * pg + i], 0, 0))
    kspec = lambda i: pl.BlockSpec((None, page, ROPE_DIM), lambda b, g, pt, i=i: (pt[b, g * pg + i], 0, 0))
    in_specs = ([pl.BlockSpec((None, R, QW), lambda b, g, pt: (b, 0, 0)),
                 pl.BlockSpec((None, npad, kvlora), lambda b, g, pt: (b, 0, 0)),
                 pl.BlockSpec((None, npad, ROPE_DIM), lambda b, g, pt: (b, 0, 0))]
                + [cspec(i) for i in range(pg)] + [kspec(i) for i in range(pg)])
    grid_spec = pltpu.PrefetchScalarGridSpec(
        num_scalar_prefetch=1, grid=(Bd, n_pages // pg), in_specs=in_specs,
        out_specs=pl.BlockSpec((None, R, kvlora), lambda b, g, pt: (b, 0, 0)),
        scratch_shapes=[pltpu.VMEM((R, 1), F32), pltpu.VMEM((R, 1), F32), pltpu.VMEM((R, kvlora), F32)])
    return pl.pallas_call(
        functools.partial(_paged_kernel, pg=pg, n_new=n_new, kvlora=kvlora),
        grid_spec=grid_spec,
        out_shape=jax.ShapeDtypeStruct((Bd, R, kvlora), F32),
        compiler_params=_cparams("parallel", "arbitrary"),
        name="mla_paged",
    )(page_table, qs, ckv_new, kr_new, *([cache_ckv] * pg), *([cache_krope] * pg))


def _merge_kernel(x_ref, y_ref, bo_ref, g_ref, ga_ref, gb_ref, om_ref, lnw_ref, lnb_ref,
                  e_ref, et_ref, wo_ref, nf_ref, rw_ref, rb_ref,
                  h_ref, hn_ref, ti_ref, tg_ref, *, n_experts):
    e = e_ref[...]
    et = et_ref[...]
    y = y_ref[...]
    inv_n = 1.0 / RWKV_HEAD_DIM
    mean = _dot_sel(_dot_sel(y, e) * inv_n, et)
    yc = y - mean
    var = _dot_sel(_dot_sel(yc * yc, e) * inv_n, et)
    yn = yc * lax.rsqrt(var + GN_EPS) * lnw_ref[...] + lnb_ref[...]
    o_rwkv = (yn + bo_ref[...]) * g_ref[...]
    merged = jax.nn.sigmoid(ga_ref[...]) * o_rwkv + jax.nn.sigmoid(gb_ref[...]) * om_ref[...]
    h = x_ref[...] + jnp.dot(merged.astype(BF16), wo_ref[...], preferred_element_type=F32)
    h_ref[...] = h
    hn = _rms(h, nf_ref[...])
    hn_ref[...] = hn.astype(BF16)

    rw = rw_ref[...]
    w_hi = rw.astype(BF16)
    w_r1 = rw - w_hi.astype(F32)
    w_mid = w_r1.astype(BF16)
    w_lo = (w_r1 - w_mid.astype(F32)).astype(BF16)
    x_hi, x_mid, x_lo = _split3(hn)
    d = lambda a, b: jnp.dot(a, b, preferred_element_type=F32)
    logits = (d(x_hi, w_hi) + (d(x_hi, w_mid) + d(x_mid, w_hi))
              + (d(x_hi, w_lo) + d(x_mid, w_mid) + d(x_lo, w_hi))) + rb_ref[...]
    lane = lax.broadcasted_iota(jnp.int32, logits.shape, 1)
    work = jnp.where(lane < n_experts, logits, -jnp.inf)
    vals, idxs = [], []
    for _ in range(TOP_K):
        mx = jnp.max(work, axis=-1, keepdims=True)
        ix = jnp.min(jnp.where(work == mx, lane, LANES), axis=-1, keepdims=True)
        vals.append(mx)
        idxs.append(ix)
        work = jnp.where(lane == ix, -jnp.inf, work)
    ex = [jnp.exp(vv - vals[0]) for vv in vals]
    den = ex[0] + ex[1] + ex[2] + ex[3]
    ti = jnp.zeros(logits.shape, jnp.int32)
    tg = jnp.zeros(logits.shape, F32)
    for j in range(TOP_K):
        ti = jnp.where(lane == j, idxs[j], ti)
        tg = jnp.where(lane == j, ex[j] / den, tg)
    ti_ref[...] = ti
    tg_ref[...] = tg


def _merge(x, y, bonus, g, p5, o_mla, lnw, lnb, e, et, wo, nf, rw, rb, *, tm, n_experts):
    M, D = x.shape
    tm = _pick_tile(M, tm, 8)
    rowspec = lambda: pl.BlockSpec((tm, D), lambda i: (i, 0))
    full = lambda a: pl.BlockSpec(a.shape, lambda i: (0,) * a.ndim)
    in_specs = [rowspec(), rowspec(), rowspec(), rowspec(),
                pl.BlockSpec((tm, D), lambda i: (i, 3)), pl.BlockSpec((tm, D), lambda i: (i, 4)),
                rowspec(), full(lnw), full(lnb), full(e), full(et), full(wo), full(nf), full(rw),
                full(rb)]
    out_specs = [rowspec(), rowspec(), pl.BlockSpec((tm, LANES), lambda i: (i, 0)),
                 pl.BlockSpec((tm, LANES), lambda i: (i, 0))]
    out_shape = [jax.ShapeDtypeStruct((M, D), F32), jax.ShapeDtypeStruct((M, D), BF16),
                 jax.ShapeDtypeStruct((M, LANES), jnp.int32), jax.ShapeDtypeStruct((M, LANES), F32)]
    return pl.pallas_call(
        functools.partial(_merge_kernel, n_experts=n_experts),
        grid=(M // tm,), in_specs=in_specs, out_specs=out_specs, out_shape=out_shape,
        compiler_params=_cparams("parallel"),
        name="merge_router",
    )(x, y, bonus, g, p5, p5, o_mla, lnw, lnb, e, et, wo, nf, rw, rb)


def _w1_prep_kernel(w_ref, p_ref, g_ref, l_ref):
    perm = p_ref[...]
    half = perm.shape[0] // 2
    for c in range(w_ref.shape[1] // perm.shape[0]):
        chunk = w_ref[:, c * 2 * half:(c + 1) * 2 * half].astype(BF16)
        both = jnp.dot(chunk, perm, preferred_element_type=F32)
        g_ref[:, c * half:(c + 1) * half] = both[:, :half].astype(BF16)
        l_ref[:, c * half:(c + 1) * half] = both[:, half:].astype(BF16)


def _w1_prep(w1, *, tr, tc):
    E, D, F2 = w1.shape
    pw = 2 * LANES
    tr = _pick_tile(D, tr, 8)
    tc = _pick_tile(F2, tc, pw)
    j = np.arange(LANES)
    perm = np.zeros((pw, pw), np.float32)
    perm[2 * j, j] = 1.0
    perm[2 * j + 1, LANES + j] = 1.0
    out = jax.ShapeDtypeStruct((E, D, F2 // 2), BF16)
    return pl.pallas_call(
        _w1_prep_kernel,
        grid=(E, D // tr, F2 // tc),
        in_specs=[pl.BlockSpec((None, tr, tc), lambda e, i, j: (e, i, j)),
                  pl.BlockSpec((pw, pw), lambda e, i, j: (0, 0))],
        out_specs=[pl.BlockSpec((None, tr, tc // 2), lambda e, i, j: (e, i, j))] * 2,
        out_shape=[out, out],
        compiler_params=_cparams("parallel", "parallel", "parallel"),
        name="moe_w1_prep",
    )(w1, jnp.asarray(perm, BF16))


def _expert_kernel(se_ref, sh_ref, sr_ref, sm_ref, x_ref, w1g_ref, w1l_ref, b1g_ref, b1l_ref,
                   w2_ref, b2_ref, o_ref, w2b_ref):
    s = pl.program_id(0)
    mode = sm_ref[s]

    @pl.when(mode == 0)
    def _():
        o_ref[...] = jnp.zeros(o_ref.shape, F32)

    @pl.when(mode == 2)
    def _():
        w2b_ref[...] = w2_ref[...].astype(BF16)

    @pl.when(mode >= 1)
    def _():
        x = x_ref[...]
        hg = jnp.dot(x, w1g_ref[...], preferred_element_type=F32) + b1g_ref[...]
        hl = jnp.dot(x, w1l_ref[...], preferred_element_type=F32) + b1l_ref[...]
        hg = jnp.minimum(hg, SWIGLU_LIMIT)
        hl = jnp.clip(hl, -SWIGLU_LIMIT, SWIGLU_LIMIT)
        act = hg * jax.nn.sigmoid(SWIGLU_ALPHA * hg) * (hl + 1.0)
        y = jnp.dot(act.astype(BF16), w2b_ref[...], preferred_element_type=F32)
        first = (sh_ref[s] == 0).astype(F32)
        o_ref[...] = y + first * b2_ref[...]


def _experts(sched, xs, w1g, w1l, b1g, b1l, w2, b2, *, bm, n_half):
    se, sh, sr, sm = sched
    NR, D = xs.shape
    E, _, F = w1g.shape
    fh = F // n_half
    S = se.shape[0]
    in_specs = [pl.BlockSpec((bm, D), lambda s, se, sh, sr, sm: (sr[s], 0)),
                pl.BlockSpec((None, D, fh), lambda s, se, sh, sr, sm: (se[s], 0, sh[s])),
                pl.BlockSpec((None, D, fh), lambda s, se, sh, sr, sm: (se[s], 0, sh[s])),
                pl.BlockSpec((None, 1, fh), lambda s, se, sh, sr, sm: (se[s], 0, sh[s])),
                pl.BlockSpec((None, 1, fh), lambda s, se, sh, sr, sm: (se[s], 0, sh[s])),
                pl.BlockSpec((None, fh, D), lambda s, se, sh, sr, sm: (se[s], sh[s], 0)),
                pl.BlockSpec((None, 1, D), lambda s, se, sh, sr, sm: (se[s], 0, 0))]
    grid_spec = pltpu.PrefetchScalarGridSpec(
        num_scalar_prefetch=4, grid=(S,), in_specs=in_specs,
        out_specs=pl.BlockSpec((None, bm, D), lambda s, se, sh, sr, sm: (sh[s], sr[s], 0)),
        scratch_shapes=[pltpu.VMEM((fh, D), BF16)])
    return pl.pallas_call(
        _expert_kernel, grid_spec=grid_spec,
        out_shape=jax.ShapeDtypeStruct((n_half, NR, D), F32),
        compiler_params=_cparams("arbitrary"),
        name="moe_experts",
    )(se, sh, sr, sm, xs, w1g, w1l, b1g, b1l, w2, b2)


def _combine_kernel(h_ref, yg_ref, tg_ref, nw_ref, o_ref, *, n_half):
    acc = h_ref[...]
    tg = tg_ref[...]
    for j in range(TOP_K):
        rows = yg_ref[0, j]
        for hh in range(1, n_half):
            rows = rows + yg_ref[hh, j]
        acc = acc + rows * tg[:, j:j + 1]
    o_ref[...] = _rms(acc, nw_ref[...])


def _combine(h, yg, tg, nw, *, tm, n_half):
    M, D = h.shape
    tm = _pick_tile(M, tm, 8)
    return pl.pallas_call(
        functools.partial(_combine_kernel, n_half=n_half),
        grid=(M // tm,),
        in_specs=[pl.BlockSpec((tm, D), lambda i: (i, 0)),
                  pl.BlockSpec((n_half, TOP_K, tm, D), lambda i: (0, 0, i, 0)),
                  pl.BlockSpec((tm, LANES), lambda i: (i, 0)),
                  pl.BlockSpec((1, D), lambda i: (0, 0))],
        out_specs=pl.BlockSpec((tm, D), lambda i: (i, 0)),
        out_shape=jax.ShapeDtypeStruct((M, D), F32),
        compiler_params=_cparams("parallel"),
        name="moe_combine",
    )(h, yg, tg, nw)


def _rope_tables(positions):
    inv = ROPE_THETA ** (-jnp.arange(0, ROPE_DIM, 2, dtype=F32) / ROPE_DIM)
    ang = positions.astype(F32)[:, None] * inv[None, :]
    cos, sin = jnp.cos(ang), jnp.sin(ang)
    cos2 = jnp.concatenate([cos, cos], axis=-1)
    sin2 = jnp.concatenate([sin, sin], axis=-1)
    n = positions.shape[0]
    ones = jnp.ones((n, NOPE_DIM), F32)
    z64 = jnp.zeros((n, QK_PAD - NOPE_DIM - ROPE_DIM), F32)
    z128 = jnp.zeros((n, NOPE_DIM), F32)
    qtab = jnp.concatenate([ones, cos2, z64, z128, sin2, z64], axis=-1)
    ktab = jnp.concatenate([cos2, sin2], axis=-1)
    return qtab, ktab


def _rot_cols(w):
    half = ROPE_DIM // 2
    return jnp.concatenate([-w[..., half:], w[..., :half]], axis=-1)


def _moe_schedule(top_idx, n_experts, bm, n_half, n_blocks):
    M = top_idx.shape[0]
    flat_e = top_idx.reshape(-1)
    onehot = (flat_e[:, None] == jnp.arange(n_experts, dtype=jnp.int32)[None, :]).astype(jnp.int32)
    rank = jnp.take_along_axis(jnp.cumsum(onehot, axis=0) - onehot, flat_e[:, None], axis=1)[:, 0]
    counts = jnp.sum(onehot, axis=0)
    nb = (counts + bm - 1) // bm
    blk_end = jnp.cumsum(nb)
    blk_start = blk_end - nb
    dest = blk_start[flat_e] * bm + rank
    flat_tok = jnp.repeat(jnp.arange(M, dtype=jnp.int32), TOP_K)
    row_tok = jnp.full((n_blocks * bm,), M, jnp.int32).at[dest].set(flat_tok)
    n_real = blk_end[-1]
    steps_end = jnp.cumsum(nb * n_half)
    s = jnp.arange(n_blocks * n_half, dtype=jnp.int32)
    real = s < n_real * n_half
    e_of = jnp.minimum(jnp.searchsorted(steps_end, s, side='right'), n_experts - 1).astype(jnp.int32)
    local = s - (steps_end - nb * n_half)[e_of]
    nb_e = jnp.maximum(nb[e_of], 1)
    h_real = local // nb_e
    r_real = blk_start[e_of] + local % nb_e
    last_e = e_of[jnp.maximum(n_real * n_half - 1, 0)]
    tail = s - n_real * n_half
    n_tail = jnp.maximum(n_blocks - n_real, 1)
    se = jnp.where(real, e_of, last_e).astype(jnp.int32)
    sh = jnp.where(real, h_real, n_half - 1 - tail // n_tail).astype(jnp.int32)
    sr = jnp.where(real, r_real, n_real + tail % n_tail).astype(jnp.int32)
    sm = jnp.where(real, jnp.where(local % nb_e == 0, 2, 1), 0).astype(jnp.int32)
    return dest, row_tok, (se, sh, sr, sm)


def kernel(x_prompt, x_sample, cache_ckv, cache_krope, state_wkv, state_shift, page_table, norm_mix_w, w_in, rwkv_mu, rwkv_w0, rwkv_w2, rwkv_a0, rwkv_a2, rwkv_g2, rwkv_k_k, rwkv_k_a, rwkv_r_k, rwkv_ln_w, rwkv_ln_b, mla_q_norm_w, mla_w_uq, mla_kv_norm_w, mla_w_uk, mla_w_uv, w_out, norm_ffn_w, router_w, router_b, expert_w1, expert_b1, expert_w2, expert_b2, norm_final_w):
    B, T, D = x_prompt.shape
    Bd, Td, _ = x_sample.shape
    H = D // RWKV_HEAD_DIM
    HD = RWKV_HEAD_DIM
    dlora, alora, glora = rwkv_w2.shape[0], rwkv_a2.shape[0], rwkv_g2.shape[0]
    qlora, Hm, _ = mla_w_uq.shape
    kvlora = mla_w_uk.shape[0]
    n_experts = router_w.shape[1]
    F = expert_w2.shape[1]
    page = cache_ckv.shape[1]
    past = page_table.shape[1] * page
    scale = float((NOPE_DIM + ROPE_DIM) ** -0.5)
    row2 = lambda a: a.reshape(1, -1).astype(F32)

    sizes = [D, D, D, dlora, alora, glora, qlora, kvlora, ROPE_DIM, D, D]
    offs = np.concatenate([[0], np.cumsum(sizes)]).tolist()
    col = lambda i: w_in[:, offs[i]:offs[i + 1]]
    lw_pad, la_pad = _round_up(dlora, LANES), _round_up(alora, LANES)
    padc = lambda a, n: jnp.pad(a, ((0, 0), (0, n - a.shape[1])))
    w5 = jnp.concatenate([col(0), col(1), col(2), col(9), col(10)], axis=1).astype(BF16)
    lora_w = lw_pad + la_pad + glora
    ws = jnp.concatenate([padc(col(3), lw_pad), padc(col(4), la_pad), col(5), col(6), col(7),
                          col(8), _rot_cols(col(8))], axis=1)
    col_cq, col_ckv, col_kr = lora_w, lora_w + qlora, lora_w + qlora + kvlora
    ws = padc(ws, _round_up(ws.shape[1], LANES)).astype(BF16)
    mu = rwkv_mu
    mur, muk, muv = row2(mu[:D]), row2(mu[D:2 * D]), row2(mu[2 * D:3 * D])
    m0 = 3 * D
    mul = jnp.concatenate([jnp.pad(mu[m0:m0 + dlora], (0, lw_pad - dlora)),
                           jnp.pad(mu[m0 + dlora:m0 + dlora + alora], (0, la_pad - alora)),
                           mu[m0 + dlora + alora:]]).reshape(1, -1)
    w2p = jnp.pad(rwkv_w2, ((0, lw_pad - dlora), (0, 0))).astype(BF16)
    a2p = jnp.pad(rwkv_a2, ((0, la_pad - alora), (0, 0))).astype(BF16)
    head_of = jnp.arange(D, dtype=jnp.int32) // HD
    e_sel = (head_of[:, None] == jnp.arange(LANES, dtype=jnp.int32)[None, :]).astype(BF16)
    et_sel = e_sel.T
    prep_consts = (mur, muk, muv, mul, row2(rwkv_w0), w2p, row2(rwkv_a0), a2p, rwkv_g2.astype(BF16),
                   row2(rwkv_k_k), row2(rwkv_k_a), row2(rwkv_r_k), e_sel, et_sel, lw_pad, la_pad)

    zq = jnp.zeros((qlora, Hm, QK_PAD - NOPE_DIM - ROPE_DIM), F32)
    q_nope_w, q_rope_w = mla_w_uq[..., :NOPE_DIM], mla_w_uq[..., NOPE_DIM:]
    wqa = jnp.concatenate([q_nope_w, q_rope_w, zq], axis=-1).reshape(qlora, Hm * QK_PAD).astype(BF16)
    wqb = jnp.concatenate([jnp.zeros_like(q_nope_w), _rot_cols(q_rope_w), zq],
                          axis=-1).reshape(qlora, Hm * QK_PAD).astype(BF16)
    zk = jnp.zeros((kvlora, Hm, QK_PAD - NOPE_DIM), F32)
    wk_top = jnp.concatenate([mla_w_uk, zk], axis=-1).reshape(kvlora, Hm * QK_PAD)
    sel = jnp.concatenate([jnp.zeros((ROPE_DIM, NOPE_DIM), F32), jnp.eye(ROPE_DIM, dtype=F32),
                           jnp.zeros((ROPE_DIM, QK_PAD - NOPE_DIM - ROPE_DIM), F32)], axis=-1)
    wk = jnp.concatenate([wk_top, jnp.tile(sel, (1, Hm))], axis=0).astype(BF16)
    wv = mla_w_uv.reshape(kvlora, Hm * V_DIM).astype(BF16)
    w_uk_t = jnp.transpose(mla_w_uk, (1, 2, 0)).astype(BF16)
    w_uv_h = jnp.transpose(mla_w_uv, (1, 0, 2)).astype(BF16)
    wo = w_out.astype(BF16)
    rw = jnp.pad(router_w, ((0, 0), (0, LANES - n_experts)))
    rb = jnp.pad(router_b, (0, LANES - n_experts)).reshape(1, LANES)
    w1g, w1l = _w1_prep(expert_w1, tr=512, tc=2048)
    b1g = expert_b1[:, None, 0::2]
    b1l = expert_b1[:, None, 1::2]
    w2e = expert_w2
    b2e = expert_b2[:, None, :]
    nmw = row2(norm_mix_w)

    def token_stage(x2d, shift_rows, n_groups, n_tiles, tt, shift, positions, tm_mla):
        p5 = _project(x2d, nmw, w5, normalize=True, tm=1024, tn=512)
        ps = _project(x2d, nmw, ws, normalize=True, tm=1024, tn=ws.shape[1])
        if shift_rows is None:
            s5 = jnp.zeros((n_groups, shift, w5.shape[1]), F32)
            ss = jnp.zeros((n_groups, shift, ws.shape[1]), F32)
        else:
            s5 = _project(shift_rows, nmw, w5, normalize=False, tm=1024, tn=512)
            ss = _project(shift_rows, nmw, ws, normalize=False, tm=1024, tn=ws.shape[1])
            s5 = s5.reshape(n_groups, shift, -1)
            ss = ss.reshape(n_groups, shift, -1)
        prep = _rwkv_prep(p5, ps, s5, ss, prep_consts, n_groups=n_groups, n_tiles=n_tiles, tt=tt,
                          shift=shift, D=D, lora_w=lora_w)
        qtab, ktab = _rope_tables(positions)
        mla = _mla_prep(ps, qtab, ktab, row2(mla_q_norm_w), row2(mla_kv_norm_w), wqa, wqb, wk, wv,
                        col_cq=col_cq, col_ckv=col_ckv, col_kr=col_kr, qlora=qlora, kvlora=kvlora,
                        n_heads=Hm, hg=min(4, Hm), tm=tm_mla, scale=scale)
        return p5, prep, mla

    xp = x_prompt.reshape(B * T, D)
    tt_p = min(128, T)
    p5_p, prep_p, mla_p = token_stage(xp, None, B, T // tt_p, tt_p, 1, jnp.arange(T), min(512, T))
    r_p, lw_p, k_p, v_p, al_p, be_p, g_p, bo_p = prep_p
    vsplit = 2 if (B * H * 2) % LANES == 0 and B * H < LANES else 1
    vr = HD // vsplit

    def to_kl(a):
        a = a.reshape(B, T, H, HD).transpose(1, 3, 0, 2).reshape(T, HD, B * H)
        return jnp.tile(a, (1, 1, vsplit))

    def to_vl(a):
        a = a.reshape(B, T, H, vsplit, vr).transpose(1, 4, 3, 0, 2)
        return a.reshape(T, vr, vsplit * B * H)

    nl_p = vsplit * B * H
    pad_l = _round_up(nl_p, LANES) - nl_p
    padl = lambda a: jnp.pad(a, ((0, 0), (0, 0), (0, pad_l))) if pad_l else a
    s0_p = jnp.zeros((vr, HD, nl_p + pad_l), F32)
    y_kl, s_kl = _rwkv_scan(padl(to_kl(r_p)), padl(to_kl(lw_p)), padl(to_kl(k_p)), padl(to_vl(v_p)),
                            padl(to_kl(al_p)), padl(to_kl(be_p)), s0_p, tb=64)
    y_p = y_kl[:, :, :nl_p].reshape(T, vr, vsplit, B, H).transpose(3, 0, 4, 2, 1).reshape(B * T, D)
    wkv_p = s_kl[:, :, :nl_p].reshape(vr, HD, vsplit, B, H).transpose(3, 4, 2, 0, 1).reshape(B, H, HD, HD)

    q_p, kf_p, vf_p, ckv_p, kr_p = mla_p
    o_mla_p = _flash_attention(q_p, kf_p, vf_p, B=B, T=T, n_heads=Hm, tq=512, tk=512)

    xs_tm = jnp.swapaxes(x_sample, 0, 1).reshape(Td * Bd, D)
    pos_s = jnp.repeat(past + jnp.arange(Td), Bd)
    p5_s, prep_s, mla_s = token_stage(xs_tm, state_shift, 1, Td, Bd, Bd, pos_s, Td * Bd)
    r_s, lw_s, k_s, v_s, al_s, be_s, g_s, bo_s = prep_s

    def to_kl_s(a):
        return a.reshape(Td, Bd, H, HD).transpose(0, 3, 1, 2).reshape(Td, HD, Bd * H)

    nl_s = Bd * H
    pad_s = _round_up(nl_s, LANES) - nl_s
    pads = lambda a: jnp.pad(a, ((0, 0), (0, 0), (0, pad_s))) if pad_s else a
    s0_s = pads(state_wkv.transpose(2, 3, 0, 1).reshape(HD, HD, nl_s))
    y_sl, s_sl = _rwkv_scan(pads(to_kl_s(r_s)), pads(to_kl_s(lw_s)), pads(to_kl_s(k_s)),
                            pads(to_kl_s(v_s)), pads(to_kl_s(al_s)), pads(to_kl_s(be_s)), s0_s, tb=Td)
    y_s = y_sl[:, :, :nl_s].reshape(Td, HD, Bd, H).transpose(0, 2, 3, 1).reshape(Td * Bd, D)
    wkv_s = s_sl[:, :, :nl_s].reshape(HD, HD, Bd, H).transpose(2, 3, 0, 1)

    q_s, _, _, ckv_s, kr_s = mla_s
    q4 = q_s.reshape(Td, Bd, Hm, QK_PAD)
    qn_h = q4[..., :NOPE_DIM].transpose(2, 1, 0, 3).reshape(Hm, Bd * Td, NOPE_DIM)
    q_lat = _head_matmul(qn_h, w_uk_t)
    q_lat = q_lat.reshape(Hm, Bd, Td, kvlora).transpose(1, 0, 2, 3).reshape(Bd, Hm * Td, kvlora)
    q_rp = q4[..., NOPE_DIM:NOPE_DIM + ROPE_DIM].transpose(1, 2, 0, 3).reshape(Bd, Hm * Td, ROPE_DIM)
    qs = jnp.concatenate([q_lat.astype(BF16), q_rp], axis=-1)
    ckv_s_bt = ckv_s.reshape(Td, Bd, kvlora).transpose(1, 0, 2)
    kr_s_bt = kr_s.reshape(Td, Bd, ROPE_DIM).transpose(1, 0, 2)
    npad = _round_up(Td, 8)
    padn = lambda a: jnp.pad(a, ((0, 0), (0, npad - Td), (0, 0)))
    o_lat = _paged_attention(page_table, qs, padn(ckv_s_bt), padn(kr_s_bt), cache_ckv, cache_krope,
                             pg=16, n_new=Td)
    o_lat_h = o_lat.reshape(Bd, Hm, Td, kvlora).transpose(1, 2, 0, 3).reshape(Hm, Td * Bd, kvlora)
    o_mla_s = _head_matmul(o_lat_h, w_uv_h)
    o_mla_s = o_mla_s.transpose(1, 0, 2).reshape(Td * Bd, Hm * V_DIM)

    mg = functools.partial(_merge, lnw=row2(rwkv_ln_w), lnb=row2(rwkv_ln_b), e=e_sel, et=et_sel, wo=wo,
                           nf=row2(norm_ffn_w), rw=rw, rb=rb, tm=128, n_experts=n_experts)
    h_p, hn_p, ti_p, tg_p = mg(xp, y_p, bo_p, g_p, p5_p, o_mla_p)
    h_s, hn_s, ti_s, tg_s = mg(xs_tm, y_s, bo_s, g_s, p5_s, o_mla_s)
    h_all = jnp.concatenate([h_p, h_s], axis=0)
    hn_all = jnp.concatenate([hn_p, hn_s], axis=0)
    ti_all = jnp.concatenate([ti_p, ti_s], axis=0)[:, :TOP_K]
    tg_all = jnp.concatenate([tg_p, tg_s], axis=0)
    M = h_all.shape[0]
    bm = 256
    n_half = 2 if F % (2 * LANES) == 0 else 1
    n_blocks = (M * TOP_K + n_experts * (bm - 1)) // bm
    dest, row_tok, sched = _moe_schedule(ti_all, n_experts, bm, n_half, n_blocks)
    hn_pad = jnp.concatenate([hn_all, jnp.zeros((1, D), BF16)], axis=0)
    xs_rows = hn_pad[row_tok]
    y_halves = _experts(sched, xs_rows, w1g, w1l, b1g, b1l, w2e, b2e, bm=bm, n_half=n_half)
    dest_slot_major = dest.reshape(M, TOP_K).T.reshape(-1)
    yg = y_halves[:, dest_slot_major].reshape(n_half, TOP_K, M, D)
    y_all = _combine(h_all, yg, tg_all, row2(norm_final_w), tm=128, n_half=n_half)

    y_prompt = y_all[:B * T].reshape(B, T, D)
    y_sample = y_all[B * T:].reshape(Td, Bd, D).transpose(1, 0, 2)
    last = jnp.concatenate([x_prompt[:, -1], x_sample[:, -1]], axis=0)
    shift_out = _rmsnorm_rows(last, nmw)
    return (y_prompt, y_sample,
            ckv_p.reshape(B, T // page, page, kvlora), kr_p.reshape(B, T // page, page, ROPE_DIM),
            wkv_p, shift_out[:B],
            ckv_s_bt, kr_s_bt, wkv_s, shift_out[B:])
```

```python
import functools

import numpy as np
import jax
import jax.numpy as jnp
from jax import lax
from jax.experimental import pallas as pl
from jax.experimental.pallas import tpu as pltpu

F32 = jnp.float32
BF16 = jnp.bfloat16

NORM_EPS = 1e-6
GN_EPS = 64e-5
RWKV_HEAD_DIM = 64
NOPE_DIM = 128
ROPE_DIM = 64
V_DIM = 128
QK_PAD = 256
ROPE_THETA = 10000.0
NEG_INF = -1e30
TOP_K = 4
SWIGLU_ALPHA = 1.702
SWIGLU_LIMIT = 7.0
LANES = 128
VMEM_LIMIT = 56 * 1024 * 1024


def _round_up(n, m):
    return (n + m - 1) // m * m


def _pick_tile(n, target, mult):
    if n <= target:
        return n
    t = target // mult * mult
    while n % t:
        t -= mult
    return t


def _cparams(*sem):
    return pltpu.CompilerParams(dimension_semantics=sem, vmem_limit_bytes=VMEM_LIMIT)


def _split3(x):
    hi = x.astype(BF16)
    r1 = x - hi.astype(F32)
    mid = r1.astype(BF16)
    lo = (r1 - mid.astype(F32)).astype(BF16)
    return hi, mid, lo


def _dot_sel(x, sel_bf16):
    hi, mid, lo = _split3(x)
    d = lambda a: jnp.dot(a, sel_bf16, preferred_element_type=F32)
    return d(hi) + d(mid) + d(lo)


def _tree_sum(xs):
    xs = list(xs)
    while len(xs) > 1:
        xs = [xs[i] + xs[i + 1] for i in range(0, len(xs) - 1, 2)] + ([xs[-1]] if len(xs) % 2 else [])
    return xs[0]


def _rms(x, w):
    return x * lax.rsqrt(jnp.mean(x * x, axis=-1, keepdims=True) + NORM_EPS) * w


def _proj_kernel(x_ref, nw_ref, w_ref, o_ref, xn_ref, *, normalize):
    @pl.when(pl.program_id(1) == 0)
    def _():
        x = x_ref[...]
        if normalize:
            x = _rms(x, nw_ref[...])
        xn_ref[...] = x.astype(BF16)

    o_ref[...] = jnp.dot(xn_ref[...], w_ref[...], preferred_element_type=F32)


def _project(x, norm_w, w, *, normalize, tm, tn):
    M, D = x.shape
    N = w.shape[1]
    tm = _pick_tile(M, tm, 8)
    tn = _pick_tile(N, tn, LANES)
    return pl.pallas_call(
        functools.partial(_proj_kernel, normalize=normalize),
        grid=(M // tm, N // tn),
        in_specs=[pl.BlockSpec((tm, D), lambda i, j: (i, 0)),
                  pl.BlockSpec((1, D), lambda i, j: (0, 0)),
                  pl.BlockSpec((D, tn), lambda i, j: (0, j))],
        out_specs=pl.BlockSpec((tm, tn), lambda i, j: (i, j)),
        out_shape=jax.ShapeDtypeStruct((M, N), F32),
        scratch_shapes=[pltpu.VMEM((tm, D), BF16)],
        compiler_params=_cparams("parallel", "arbitrary"),
        name="norm_proj",
    )(x, norm_w, w)


def _rmsnorm_rows_kernel(x_ref, w_ref, o_ref):
    o_ref[...] = _rms(x_ref[...], w_ref[...])


def _rmsnorm_rows(x, w):
    return pl.pallas_call(
        _rmsnorm_rows_kernel,
        out_shape=jax.ShapeDtypeStruct(x.shape, F32),
        name="rmsnorm_rows",
    )(x, w)


def _rwkv_prep_kernel(r_ref, k_ref, v_ref, l_ref, sr_ref, sk_ref, sv_ref, sl_ref,
                      mur_ref, muk_ref, muv_ref, mul_ref, w0_ref, w2_ref, a0_ref, a2_ref, g2_ref,
                      kk_ref, ka_ref, rk_ref, e_ref, et_ref,
                      ro_ref, lw_ref, ko_ref, vo_ref, al_ref, be_ref, go_ref, bo_ref,
                      cr_ref, ck_ref, cv_ref, cl_ref, *, shift, lw_pad, la_pad):
    t = pl.program_id(1)

    @pl.when(t == 0)
    def _():
        cr_ref[...] = sr_ref[...]
        ck_ref[...] = sk_ref[...]
        cv_ref[...] = sv_ref[...]
        cl_ref[...] = sl_ref[...]

    def lerp(p_ref, c_ref, mu_ref):
        p = p_ref[...]
        if shift == 1:
            rolled = pltpu.roll(p, 1, axis=0)
            row = lax.broadcasted_iota(jnp.int32, p.shape, 0)
            prev = jnp.where(row == 0, c_ref[...], rolled)
            c_new = p[p.shape[0] - 1:, :]
        else:
            prev = c_ref[...]
            c_new = p
        xx = p + (prev - p) * mu_ref[...]
        c_ref[...] = c_new
        return xx

    r = lerp(r_ref, cr_ref, mur_ref)
    k = lerp(k_ref, ck_ref, muk_ref)
    v = lerp(v_ref, cv_ref, muv_ref)
    lo = lerp(l_ref, cl_ref, mul_ref)
    hw = lo[:, :lw_pad]
    ha = lo[:, lw_pad:lw_pad + la_pad]
    hg = lo[:, lw_pad + la_pad:]

    mm = lambda a, b_ref: jnp.dot(a.astype(BF16), b_ref[...], preferred_element_type=F32)
    w = -jax.nn.softplus(-(w0_ref[...] + mm(jnp.tanh(hw), w2_ref))) - 0.5
    a = jax.nn.sigmoid(a0_ref[...] + mm(ha, a2_ref))
    g = mm(jax.nn.sigmoid(hg), g2_ref)

    e = e_ref[...]
    et = et_ref[...]
    kk = k * kk_ref[...]
    ssq = _dot_sel(kk * kk, e)
    inv = lax.rsqrt(jnp.maximum(ssq, 1e-24))
    kk = kk * _dot_sel(inv, et)
    k2 = k * (1.0 + (a - 1.0) * ka_ref[...])
    bsum = _dot_sel(_dot_sel(r * k2 * rk_ref[...], e), et)

    ro_ref[...] = r
    lw_ref[...] = -jnp.exp(w)
    ko_ref[...] = k2
    vo_ref[...] = v
    al_ref[...] = -kk
    be_ref[...] = kk * a
    go_ref[...] = g
    bo_ref[...] = bsum * v


def _rwkv_prep(p5, ps, shift5, shifts, consts, *, n_groups, n_tiles, tt, shift, D, lora_w):
    M = p5.shape[0]
    (mur, muk, muv, mul, w0, w2p, a0, a2p, g2, k_k, k_a, r_k, e, et, lw_pad, la_pad) = consts
    row = lambda g, t: (g * n_tiles + t)
    big = lambda c: pl.BlockSpec((tt, D), lambda g, t, c=c: (row(g, t), c))
    sh = lambda c: pl.BlockSpec((None, shift, D), lambda g, t, c=c: (g, 0, c))
    full = lambda a: pl.BlockSpec(a.shape, lambda g, t: (0,) * a.ndim)
    in_specs = [big(0), big(1), big(2),
                pl.BlockSpec((tt, lora_w), lambda g, t: (row(g, t), 0)),
                sh(0), sh(1), sh(2),
                pl.BlockSpec((None, shift, lora_w), lambda g, t: (g, 0, 0)),
                full(mur), full(muk), full(muv), full(mul), full(w0), full(w2p), full(a0),
                full(a2p), full(g2), full(k_k), full(k_a), full(r_k), full(e), full(et)]
    out_spec = pl.BlockSpec((tt, D), lambda g, t: (row(g, t), 0))
    outs = pl.pallas_call(
        functools.partial(_rwkv_prep_kernel, shift=shift, lw_pad=lw_pad, la_pad=la_pad),
        grid=(n_groups, n_tiles),
        in_specs=in_specs,
        out_specs=[out_spec] * 8,
        out_shape=[jax.ShapeDtypeStruct((M, D), F32)] * 8,
        scratch_shapes=[pltpu.VMEM((shift, D), F32)] * 3 + [pltpu.VMEM((shift, lora_w), F32)],
        compiler_params=_cparams("parallel", "arbitrary"),
        name="rwkv_prep",
    )(p5, p5, p5, ps, shift5, shift5, shift5, shifts,
      mur, muk, muv, mul, w0, w2p, a0, a2p, g2, k_k, k_a, r_k, e, et)
    return outs


def _rwkv_scan_kernel(r_ref, w_ref, k_ref, v_ref, a_ref, b_ref, s0_ref, y_ref, sf_ref, s_ref, w_scr,
                      *, tb, vr, nk):
    tblk = pl.program_id(1)

    @pl.when(tblk == 0)
    def _():
        s_ref[...] = s0_ref[...]

    n_acc = 8
    row = lambda ref, t, k: jnp.broadcast_to(ref[t, pl.ds(k, 1), :], (vr, LANES))

    def accumulate(parts, k, term):
        parts[k % n_acc] = term if parts[k % n_acc] is None else parts[k % n_acc] + term

    sa_parts = [None] * n_acc
    for k in range(nk):
        accumulate(sa_parts, k, s_ref[k] * row(a_ref, 0, k))
    sa0 = _tree_sum(sa_parts)

    def step(t, sa):
        w_scr[...] = jnp.exp(w_ref[t])
        vt = v_ref[t]
        tn = jnp.minimum(t + 1, tb - 1)
        y_parts = [None] * n_acc
        sa_parts = [None] * n_acc
        for k in range(nk):
            wk = jnp.broadcast_to(w_scr[pl.ds(k, 1), :], (vr, LANES))
            s = s_ref[k] * wk + sa * row(b_ref, t, k) + vt * row(k_ref, t, k)
            s_ref[k] = s
            accumulate(y_parts, k, s * row(r_ref, t, k))
            accumulate(sa_parts, k, s * row(a_ref, tn, k))
        y_ref[t] = _tree_sum(y_parts)
        return _tree_sum(sa_parts)

    lax.fori_loop(0, tb, step, sa0)

    @pl.when(tblk == pl.num_programs(1) - 1)
    def _():
        sf_ref[...] = s_ref[...]


def _rwkv_scan(r, w, k, v, a, b, s0, *, tb):
    T, K, NL = r.shape
    VR = v.shape[1]
    tb = min(tb, T)
    assert T % tb == 0 and NL % LANES == 0
    kspec = pl.BlockSpec((tb, K, LANES), lambda n, t: (t, 0, n))
    vspec = pl.BlockSpec((tb, VR, LANES), lambda n, t: (t, 0, n))
    sspec = pl.BlockSpec((K, VR, LANES), lambda n, t: (0, 0, n))
    return pl.pallas_call(
        functools.partial(_rwkv_scan_kernel, tb=tb, vr=VR, nk=K),
        grid=(NL // LANES, T // tb),
        in_specs=[kspec, kspec, kspec, vspec, kspec, kspec, sspec],
        out_specs=[vspec, sspec],
        out_shape=[jax.ShapeDtypeStruct((T, VR, NL), F32),
                   jax.ShapeDtypeStruct((K, VR, NL), F32)],
        scratch_shapes=[pltpu.VMEM((K, VR, LANES), F32), pltpu.VMEM((K, LANES), F32)],
        compiler_params=_cparams("parallel", "arbitrary"),
        name="rwkv_scan",
    )(r, w, k, v, a, b, s0)


def _mla_prep_kernel(cq_ref, ckv_ref, kr_ref, qtab_ref, ktab_ref, qn_ref, kvn_ref,
                     wqa_ref, wqb_ref, wk_ref, wv_ref,
                     q_ref, k_ref, v_ref, ckv_o_ref, kr_o_ref, cqn_ref, ckk_ref, *, hg, scale):
    @pl.when(pl.program_id(1) == 0)
    def _():
        cqn_ref[...] = _rms(cq_ref[...], qn_ref[...]).astype(BF16)
        ckv = _rms(ckv_ref[...], kvn_ref[...])
        ckv_o_ref[...] = ckv
        kr = kr_ref[...]
        ktab = ktab_ref[...]
        krope = kr[:, :ROPE_DIM] * ktab[:, :ROPE_DIM] + kr[:, ROPE_DIM:] * ktab[:, ROPE_DIM:]
        kr_o_ref[...] = krope
        ckk_ref[...] = jnp.concatenate([ckv, krope], axis=-1).astype(BF16)

    cqn = cqn_ref[...]
    qtab = qtab_ref[...]
    cosf = jnp.concatenate([qtab[:, :QK_PAD]] * hg, axis=-1)
    sinf = jnp.concatenate([qtab[:, QK_PAD:]] * hg, axis=-1)
    qa = jnp.dot(cqn, wqa_ref[...], preferred_element_type=F32)
    qb = jnp.dot(cqn, wqb_ref[...], preferred_element_type=F32)
    q_ref[...] = ((qa * cosf + qb * sinf) * scale).astype(BF16)
    ckk = ckk_ref[...]
    k_ref[...] = jnp.dot(ckk, wk_ref[...], preferred_element_type=F32).astype(BF16)
    v_ref[...] = jnp.dot(ckk[:, :ckk.shape[1] - ROPE_DIM], wv_ref[...],
                         preferred_element_type=F32).astype(BF16)


def _mla_prep(ps, qtab, ktab, qn, kvn, wqa, wqb, wk, wv, *, col_cq, col_ckv, col_kr, qlora, kvlora,
              n_heads, hg, tm, scale):
    M = ps.shape[0]
    tm = _pick_tile(M, tm, 8)
    n_tab = qtab.shape[0] // tm
    grid = (M // tm, n_heads // hg)
    tabspec = lambda a: pl.BlockSpec((tm, a.shape[1]), lambda i, j: (i % n_tab, 0))
    full = lambda a: pl.BlockSpec(a.shape, lambda i, j: (0,) * a.ndim)
    in_specs = [pl.BlockSpec((tm, qlora), lambda i, j: (i, col_cq // qlora)),
                pl.BlockSpec((tm, kvlora), lambda i, j: (i, col_ckv // kvlora)),
                pl.BlockSpec((tm, 2 * ROPE_DIM), lambda i, j: (i, col_kr // (2 * ROPE_DIM))),
                tabspec(qtab), tabspec(ktab), full(qn), full(kvn),
                pl.BlockSpec((qlora, hg * QK_PAD), lambda i, j: (0, j)),
                pl.BlockSpec((qlora, hg * QK_PAD), lambda i, j: (0, j)),
                pl.BlockSpec((kvlora + ROPE_DIM, hg * QK_PAD), lambda i, j: (0, j)),
                pl.BlockSpec((kvlora, hg * V_DIM), lambda i, j: (0, j))]
    out_specs = [pl.BlockSpec((tm, hg * QK_PAD), lambda i, j: (i, j)),
                 pl.BlockSpec((tm, hg * QK_PAD), lambda i, j: (i, j)),
                 pl.BlockSpec((tm, hg * V_DIM), lambda i, j: (i, j)),
                 pl.BlockSpec((tm, kvlora), lambda i, j: (i, 0)),
                 pl.BlockSpec((tm, ROPE_DIM), lambda i, j: (i, 0))]
    out_shape = [jax.ShapeDtypeStruct((M, n_heads * QK_PAD), BF16),
                 jax.ShapeDtypeStruct((M, n_heads * QK_PAD), BF16),
                 jax.ShapeDtypeStruct((M, n_heads * V_DIM), BF16),
                 jax.ShapeDtypeStruct((M, kvlora), F32),
                 jax.ShapeDtypeStruct((M, ROPE_DIM), F32)]
    return pl.pallas_call(
        functools.partial(_mla_prep_kernel, hg=hg, scale=scale),
        grid=grid, in_specs=in_specs, out_specs=out_specs, out_shape=out_shape,
        scratch_shapes=[pltpu.VMEM((tm, qlora), BF16), pltpu.VMEM((tm, kvlora + ROPE_DIM), BF16)],
        compiler_params=_cparams("parallel", "arbitrary"),
        name="mla_prep",
    )(ps, ps, ps, qtab, ktab, qn, kvn, wqa, wqb, wk, wv)


def _flash_kernel(q_ref, k_ref, v_ref, o_ref, *, tq, tk):
    i = pl.program_id(2)
    q = q_ref[...]
    n_sub = tq // tk

    def block(j, carry, masked):
        m, l, acc = carry
        start = pl.multiple_of(j * tk, tk)
        kb = k_ref[pl.ds(start, tk), :]
        vb = v_ref[pl.ds(start, tk), :]
        s = lax.dot_general(q, kb, (((1,), (1,)), ((), ())), preferred_element_type=F32)
        if masked:
            qpos = i * tq + lax.broadcasted_iota(jnp.int32, s.shape, 0)
            kpos = j * tk + lax.broadcasted_iota(jnp.int32, s.shape, 1)
            s = jnp.where(kpos <= qpos, s, NEG_INF)
        m_new = jnp.maximum(m, jnp.max(s, axis=-1, keepdims=True))
        p = jnp.exp(s - m_new)
        corr = jnp.exp(m - m_new)
        l = corr * l + jnp.sum(p, axis=-1, keepdims=True)
        acc = corr * acc + jnp.dot(p.astype(BF16), vb, preferred_element_type=F32)
        return m_new, l, acc

    init = (jnp.full((tq, 1), NEG_INF, F32), jnp.zeros((tq, 1), F32),
            jnp.zeros((tq, v_ref.shape[1]), F32))
    carry = lax.fori_loop(0, i * n_sub, lambda j, c: block(j, c, False), init)
    for d in range(n_sub):
        carry = block(i * n_sub + d, carry, True)
    m, l, acc = carry
    o_ref[...] = (acc / l).astype(o_ref.dtype)


def _flash_attention(q, k, v, *, B, T, n_heads, tq, tk):
    tq = min(tq, T)
    tk = min(tk, tq)
    nq = T // tq
    return pl.pallas_call(
        functools.partial(_flash_kernel, tq=tq, tk=tk),
        grid=(B, n_heads, nq),
        in_specs=[pl.BlockSpec((tq, QK_PAD), lambda b, h, i: (b * nq + i, h)),
                  pl.BlockSpec((T, QK_PAD), lambda b, h, i: (b, h)),
                  pl.BlockSpec((T, V_DIM), lambda b, h, i: (b, h))],
        out_specs=pl.BlockSpec((tq, V_DIM), lambda b, h, i: (b * nq + i, h)),
        out_shape=jax.ShapeDtypeStruct((B * T, n_heads * V_DIM), F32),
        compiler_params=_cparams("parallel", "parallel", "arbitrary"),
        name="mla_flash",
    )(q, k, v)


def _bmm_kernel(x_ref, w_ref, o_ref):
    o_ref[...] = jnp.dot(x_ref[...].astype(BF16), w_ref[...], preferred_element_type=F32)


def _head_matmul(x, w):
    H, R, K = x.shape
    N = w.shape[2]
    return pl.pallas_call(
        _bmm_kernel,
        grid=(H,),
        in_specs=[pl.BlockSpec((None, R, K), lambda h: (h, 0, 0)),
                  pl.BlockSpec((None, K, N), lambda h: (h, 0, 0))],
        out_specs=pl.BlockSpec((None, R, N), lambda h: (h, 0, 0)),
        out_shape=jax.ShapeDtypeStruct((H, R, N), F32),
        compiler_params=_cparams("parallel"),
        name="head_matmul",
    )(x, w)


def _paged_kernel(pt_ref, q_ref, cn_ref, kn_ref, *refs, pg, n_new, kvlora):
    ckv_refs = refs[:pg]
    kr_refs = refs[pg:2 * pg]
    o_ref = refs[2 * pg]
    m_ref, l_ref, acc_ref = refs[2 * pg + 1:]
    g = pl.program_id(1)

    @pl.when(g == 0)
    def _():
        m_ref[...] = jnp.full(m_ref.shape, NEG_INF, F32)
        l_ref[...] = jnp.zeros(l_ref.shape, F32)
        acc_ref[...] = jnp.zeros(acc_ref.shape, F32)

    q = q_ref[...]
    ql = q[:, :kvlora]
    qr = q[:, kvlora:]
    dn = (((1,), (1,)), ((), ()))

    def scores(c_ref, k_ref, k_is_transposed):
        kb = k_ref[...].astype(BF16)
        if k_is_transposed:
            rope = jnp.dot(qr, kb, preferred_element_type=F32)
        else:
            rope = lax.dot_general(qr, kb, dn, preferred_element_type=F32)
        return lax.dot_general(ql, c_ref[...].astype(BF16), dn, preferred_element_type=F32) + rope

    def update(s, value_refs):
        m = m_ref[...]
        m_new = jnp.maximum(m, jnp.max(s, axis=-1, keepdims=True))
        p = jnp.exp(s - m_new).astype(BF16)
        corr = jnp.exp(m - m_new)
        l_ref[...] = corr * l_ref[...] + jnp.sum(p.astype(F32), axis=-1, keepdims=True)
        pv = None
        off = 0
        for c_ref in value_refs:
            n = c_ref.shape[0]
            part = jnp.dot(p[:, off:off + n], c_ref[...].astype(BF16), preferred_element_type=F32)
            pv = part if pv is None else pv + part
            off += n
        acc_ref[...] = corr * acc_ref[...] + pv
        m_ref[...] = m_new

    s_all = jnp.concatenate([scores(c, k, True) for c, k in zip(ckv_refs, kr_refs)], axis=-1)
    update(s_all, ckv_refs)

    @pl.when(g == pl.num_programs(1) - 1)
    def _():
        rows = q.shape[0]
        npad = cn_ref.shape[0]
        tq = lax.broadcasted_iota(jnp.int32, (rows, npad), 0) % n_new
        kj = lax.broadcasted_iota(jnp.int32, (rows, npad), 1)
        update(jnp.where(kj <= tq, scores(cn_ref, kn_ref, False), NEG_INF), [cn_ref])
        o_ref[...] = acc_ref[...] / l_ref[...]


def _paged_attention(page_table, qs, ckv_new, kr_new, cache_ckv, cache_krope, *, pg, n_new):
    Bd, R, QW = qs.shape
    kvlora = cache_ckv.shape[2]
    page = cache_ckv.shape[1]
    n_pages = page_table.shape[1]
    pg = min(pg, n_pages)
    assert n_pages % pg == 0
    npad = ckv_new.shape[1]
    cspec = lambda i: pl.BlockSpec((None, page, kvlora), lambda b, g, pt, i=i: (pt[b, g * pg + i], 0, 0))
    krope_t = jnp.swapaxes(cache_krope, 1, 2)
    kspec = lambda i: pl.BlockSpec((None, ROPE_DIM, page), lambda b, g, pt, i=i: (pt[b, g * pg + i], 0, 0))
    in_specs = ([pl.BlockSpec((None, R, QW), lambda b, g, pt: (b, 0, 0)),
                 pl.BlockSpec((None, npad, kvlora), lambda b, g, pt: (b, 0, 0)),
                 pl.BlockSpec((None, npad, ROPE_DIM), lambda b, g, pt: (b, 0, 0))]
                + [cspec(i) for i in range(pg)] + [kspec(i) for i in range(pg)])
    grid_spec = pltpu.PrefetchScalarGridSpec(
        num_scalar_prefetch=1, grid=(Bd, n_pages // pg), in_specs=in_specs,
        out_specs=pl.BlockSpec((None, R, kvlora), lambda b, g, pt: (b, 0, 0)),
        scratch_shapes=[pltpu.VMEM((R, 1), F32), pltpu.VMEM((R, 1), F32), pltpu.VMEM((R, kvlora), F32)])
    return pl.pallas_call(
        functools.partial(_paged_kernel, pg=pg, n_new=n_new, kvlora=kvlora),
        grid_spec=grid_spec,
        out_shape=jax.ShapeDtypeStruct((Bd, R, kvlora), F32),
        compiler_params=_cparams("parallel", "arbitrary"),
        name="mla_paged",
    )(page_table, qs, ckv_new, kr_new, *([cache_ckv] * pg), *([krope_t] * pg))


def _merge_kernel(x_ref, y_ref, bo_ref, g_ref, ga_ref, gb_ref, om_ref, lnw_ref, lnb_ref,
                  e_ref, et_ref, wo_ref, nf_ref, rw_ref, rb_ref,
                  h_ref, hn_ref, ti_ref, tg_ref, *, n_experts):
    e = e_ref[...]
    et = et_ref[...]
    y = y_ref[...]
    inv_n = 1.0 / RWKV_HEAD_DIM
    mean = _dot_sel(_dot_sel(y, e) * inv_n, et)
    yc = y - mean
    var = _dot_sel(_dot_sel(yc * yc, e) * inv_n, et)
    yn = yc * lax.rsqrt(var + GN_EPS) * lnw_ref[...] + lnb_ref[...]
    o_rwkv = (yn + bo_ref[...]) * g_ref[...]
    merged = jax.nn.sigmoid(ga_ref[...]) * o_rwkv + jax.nn.sigmoid(gb_ref[...]) * om_ref[...]
    h = x_ref[...] + jnp.dot(merged.astype(BF16), wo_ref[...], preferred_element_type=F32)
    h_ref[...] = h
    hn = _rms(h, nf_ref[...])
    hn_ref[...] = hn.astype(BF16)

    rw = rw_ref[...]
    w_hi = rw.astype(BF16)
    w_r1 = rw - w_hi.astype(F32)
    w_mid = w_r1.astype(BF16)
    w_lo = (w_r1 - w_mid.astype(F32)).astype(BF16)
    x_hi, x_mid, x_lo = _split3(hn)
    d = lambda a, b: jnp.dot(a, b, preferred_element_type=F32)
    logits = (d(x_hi, w_hi) + (d(x_hi, w_mid) + d(x_mid, w_hi))
              + (d(x_hi, w_lo) + d(x_mid, w_mid) + d(x_lo, w_hi))) + rb_ref[...]
    lane = lax.broadcasted_iota(jnp.int32, logits.shape, 1)
    work = jnp.where(lane < n_experts, logits, -jnp.inf)
    vals, idxs = [], []
    for _ in range(TOP_K):
        mx = jnp.max(work, axis=-1, keepdims=True)
        ix = jnp.min(jnp.where(work == mx, lane, LANES), axis=-1, keepdims=True)
        vals.append(mx)
        idxs.append(ix)
        work = jnp.where(lane == ix, -jnp.inf, work)
    ex = [jnp.exp(vv - vals[0]) for vv in vals]
    den = ex[0] + ex[1] + ex[2] + ex[3]
    ti = jnp.zeros(logits.shape, jnp.int32)
    tg = jnp.zeros(logits.shape, F32)
    for j in range(TOP_K):
        ti = jnp.where(lane == j, idxs[j], ti)
        tg = jnp.where(lane == j, ex[j] / den, tg)
    ti_ref[...] = ti
    tg_ref[...] = tg


def _merge(x, y, bonus, g, p5, o_mla, lnw, lnb, e, et, wo, nf, rw, rb, *, tm, n_experts):
    M, D = x.shape
    tm = _pick_tile(M, tm, 8)
    rowspec = lambda: pl.BlockSpec((tm, D), lambda i: (i, 0))
    full = lambda a: pl.BlockSpec(a.shape, lambda i: (0,) * a.ndim)
    in_specs = [rowspec(), rowspec(), rowspec(), rowspec(),
                pl.BlockSpec((tm, D), lambda i: (i, 3)), pl.BlockSpec((tm, D), lambda i: (i, 4)),
                rowspec(), full(lnw), full(lnb), full(e), full(et), full(wo), full(nf), full(rw),
                full(rb)]
    out_specs = [rowspec(), rowspec(), pl.BlockSpec((tm, LANES), lambda i: (i, 0)),
                 pl.BlockSpec((tm, LANES), lambda i: (i, 0))]
    out_shape = [jax.ShapeDtypeStruct((M, D), F32), jax.ShapeDtypeStruct((M, D), BF16),
                 jax.ShapeDtypeStruct((M, LANES), jnp.int32), jax.ShapeDtypeStruct((M, LANES), F32)]
    return pl.pallas_call(
        functools.partial(_merge_kernel, n_experts=n_experts),
        grid=(M // tm,), in_specs=in_specs, out_specs=out_specs, out_shape=out_shape,
        compiler_params=_cparams("parallel"),
        name="merge_router",
    )(x, y, bonus, g, p5, p5, o_mla, lnw, lnb, e, et, wo, nf, rw, rb)


def _w1_prep_kernel(w_ref, p_ref, g_ref, l_ref):
    perm = p_ref[...]
    half = perm.shape[0] // 2
    for c in range(w_ref.shape[1] // perm.shape[0]):
        chunk = w_ref[:, c * 2 * half:(c + 1) * 2 * half].astype(BF16)
        both = jnp.dot(chunk, perm, preferred_element_type=F32)
        g_ref[:, c * half:(c + 1) * half] = both[:, :half].astype(BF16)
        l_ref[:, c * half:(c + 1) * half] = both[:, half:].astype(BF16)


def _w1_prep(w1, *, tr, tc):
    E, D, F2 = w1.shape
    pw = 2 * LANES
    tr = _pick_tile(D, tr, 8)
    tc = _pick_tile(F2, tc, pw)
    j = np.arange(LANES)
    perm = np.zeros((pw, pw), np.float32)
    perm[2 * j, j] = 1.0
    perm[2 * j + 1, LANES + j] = 1.0
    out = jax.ShapeDtypeStruct((E, D, F2 // 2), BF16)
    return pl.pallas_call(
        _w1_prep_kernel,
        grid=(E, D // tr, F2 // tc),
        in_specs=[pl.BlockSpec((None, tr, tc), lambda e, i, j: (e, i, j)),
                  pl.BlockSpec((pw, pw), lambda e, i, j: (0, 0))],
        out_specs=[pl.BlockSpec((None, tr, tc // 2), lambda e, i, j: (e, i, j))] * 2,
        out_shape=[out, out],
        compiler_params=_cparams("parallel", "parallel", "parallel"),
        name="moe_w1_prep",
    )(w1, jnp.asarray(perm, BF16))


def _expert_kernel(se_ref, sh_ref, sr_ref, sm_ref, x_ref, w1g_ref, w1l_ref, b1g_ref, b1l_ref,
                   w2_ref, b2_ref, o_ref, w2b_ref):
    s = pl.program_id(0)
    mode = sm_ref[s]

    @pl.when(mode == 0)
    def _():
        o_ref[...] = jnp.zeros(o_ref.shape, o_ref.dtype)

    @pl.when(mode == 2)
    def _():
        w2b_ref[...] = w2_ref[...].astype(BF16)

    @pl.when(mode >= 1)
    def _():
        x = x_ref[...]
        hg = jnp.dot(x, w1g_ref[...], preferred_element_type=F32) + b1g_ref[...]
        hl = jnp.dot(x, w1l_ref[...], preferred_element_type=F32) + b1l_ref[...]
        hg = jnp.minimum(hg, SWIGLU_LIMIT)
        hl = jnp.clip(hl, -SWIGLU_LIMIT, SWIGLU_LIMIT)
        act = hg * jax.nn.sigmoid(SWIGLU_ALPHA * hg) * (hl + 1.0)
        y = jnp.dot(act.astype(BF16), w2b_ref[...], preferred_element_type=F32)
        first = (sh_ref[s] == 0).astype(F32)
        o_ref[...] = (y + first * b2_ref[...]).astype(o_ref.dtype)


def _experts(sched, xs, w1g, w1l, b1g, b1l, w2, b2, *, bm, n_half):
    se, sh, sr, sm = sched
    NR, D = xs.shape
    E, _, F = w1g.shape
    fh = F // n_half
    S = se.shape[0]
    in_specs = [pl.BlockSpec((bm, D), lambda s, se, sh, sr, sm: (sr[s], 0)),
                pl.BlockSpec((None, D, fh), lambda s, se, sh, sr, sm: (se[s], 0, sh[s])),
                pl.BlockSpec((None, D, fh), lambda s, se, sh, sr, sm: (se[s], 0, sh[s])),
                pl.BlockSpec((None, 1, fh), lambda s, se, sh, sr, sm: (se[s], 0, sh[s])),
                pl.BlockSpec((None, 1, fh), lambda s, se, sh, sr, sm: (se[s], 0, sh[s])),
                pl.BlockSpec((None, fh, D), lambda s, se, sh, sr, sm: (se[s], sh[s], 0)),
                pl.BlockSpec((None, 1, D), lambda s, se, sh, sr, sm: (se[s], 0, 0))]
    grid_spec = pltpu.PrefetchScalarGridSpec(
        num_scalar_prefetch=4, grid=(S,), in_specs=in_specs,
        out_specs=pl.BlockSpec((bm, D), lambda s, se, sh, sr, sm: (sr[s], sh[s])),
        scratch_shapes=[pltpu.VMEM((fh, D), BF16)])
    return pl.pallas_call(
        _expert_kernel, grid_spec=grid_spec,
        out_shape=jax.ShapeDtypeStruct((NR, n_half * D), BF16),
        compiler_params=_cparams("arbitrary"),
        name="moe_experts",
    )(se, sh, sr, sm, xs, w1g, w1l, b1g, b1l, w2, b2)


def _combine_kernel(h_ref, yg_ref, tg_ref, nw_ref, o_ref, *, n_half):
    acc = h_ref[...]
    tg = tg_ref[...]
    D = acc.shape[1]
    for j in range(TOP_K):
        rows = yg_ref[j, :, :D].astype(F32)
        for hh in range(1, n_half):
            rows = rows + yg_ref[j, :, hh * D:(hh + 1) * D].astype(F32)
        acc = acc + rows * tg[:, j:j + 1]
    o_ref[...] = _rms(acc, nw_ref[...])


def _combine(h, yg, tg, nw, *, tm, n_half):
    M, D = h.shape
    tm = _pick_tile(M, tm, 8)
    return pl.pallas_call(
        functools.partial(_combine_kernel, n_half=n_half),
        grid=(M // tm,),
        in_specs=[pl.BlockSpec((tm, D), lambda i: (i, 0)),
                  pl.BlockSpec((TOP_K, tm, n_half * D), lambda i: (0, i, 0)),
                  pl.BlockSpec((tm, LANES), lambda i: (i, 0)),
                  pl.BlockSpec((1, D), lambda i: (0, 0))],
        out_specs=pl.BlockSpec((tm, D), lambda i: (i, 0)),
        out_shape=jax.ShapeDtypeStruct((M, D), F32),
        compiler_params=_cparams("parallel"),
        name="moe_combine",
    )(h, yg, tg, nw)


def _rope_tables(positions):
    inv = ROPE_THETA ** (-jnp.arange(0, ROPE_DIM, 2, dtype=F32) / ROPE_DIM)
    ang = positions.astype(F32)[:, None] * inv[None, :]
    cos, sin = jnp.cos(ang), jnp.sin(ang)
    cos2 = jnp.concatenate([cos, cos], axis=-1)
    sin2 = jnp.concatenate([sin, sin], axis=-1)
    n = positions.shape[0]
    ones = jnp.ones((n, NOPE_DIM), F32)
    z64 = jnp.zeros((n, QK_PAD - NOPE_DIM - ROPE_DIM), F32)
    z128 = jnp.zeros((n, NOPE_DIM), F32)
    qtab = jnp.concatenate([ones, cos2, z64, z128, sin2, z64], axis=-1)
    ktab = jnp.concatenate([cos2, sin2], axis=-1)
    return qtab, ktab


def _rot_cols(w):
    half = ROPE_DIM // 2
    return jnp.concatenate([-w[..., half:], w[..., :half]], axis=-1)


def _moe_schedule(top_idx, n_experts, bm, n_half, n_blocks):
    M = top_idx.shape[0]
    flat_e = top_idx.reshape(-1)
    onehot = (flat_e[:, None] == jnp.arange(n_experts, dtype=jnp.int32)[None, :]).astype(jnp.int32)
    rank = jnp.take_along_axis(jnp.cumsum(onehot, axis=0) - onehot, flat_e[:, None], axis=1)[:, 0]
    counts = jnp.sum(onehot, axis=0)
    nb = (counts + bm - 1) // bm
    blk_end = jnp.cumsum(nb)
    blk_start = blk_end - nb
    dest = blk_start[flat_e] * bm + rank
    flat_tok = jnp.repeat(jnp.arange(M, dtype=jnp.int32), TOP_K)
    row_tok = jnp.full((n_blocks * bm,), M, jnp.int32).at[dest].set(flat_tok)
    n_real = blk_end[-1]
    steps_end = jnp.cumsum(nb * n_half)
    s = jnp.arange(n_blocks * n_half, dtype=jnp.int32)
    real = s < n_real * n_half
    e_of = jnp.minimum(jnp.searchsorted(steps_end, s, side='right'), n_experts - 1).astype(jnp.int32)
    local = s - (steps_end - nb * n_half)[e_of]
    nb_e = jnp.maximum(nb[e_of], 1)
    h_real = local // nb_e
    r_real = blk_start[e_of] + local % nb_e
    last_e = e_of[jnp.maximum(n_real * n_half - 1, 0)]
    tail = s - n_real * n_half
    n_tail = jnp.maximum(n_blocks - n_real, 1)
    se = jnp.where(real, e_of, last_e).astype(jnp.int32)
    sh = jnp.where(real, h_real, n_half - 1 - tail // n_tail).astype(jnp.int32)
    sr = jnp.where(real, r_real, n_real + tail % n_tail).astype(jnp.int32)
    sm = jnp.where(real, jnp.where(local % nb_e == 0, 2, 1), 0).astype(jnp.int32)
    return dest, row_tok, (se, sh, sr, sm)


def kernel(x_prompt, x_sample, cache_ckv, cache_krope, state_wkv, state_shift, page_table, norm_mix_w, w_in, rwkv_mu, rwkv_w0, rwkv_w2, rwkv_a0, rwkv_a2, rwkv_g2, rwkv_k_k, rwkv_k_a, rwkv_r_k, rwkv_ln_w, rwkv_ln_b, mla_q_norm_w, mla_w_uq, mla_kv_norm_w, mla_w_uk, mla_w_uv, w_out, norm_ffn_w, router_w, router_b, expert_w1, expert_b1, expert_w2, expert_b2, norm_final_w):
    B, T, D = x_prompt.shape
    Bd, Td, _ = x_sample.shape
    H = D // RWKV_HEAD_DIM
    HD = RWKV_HEAD_DIM
    dlora, alora, glora = rwkv_w2.shape[0], rwkv_a2.shape[0], rwkv_g2.shape[0]
    qlora, Hm, _ = mla_w_uq.shape
    kvlora = mla_w_uk.shape[0]
    n_experts = router_w.shape[1]
    F = expert_w2.shape[1]
    page = cache_ckv.shape[1]
    past = page_table.shape[1] * page
    scale = float((NOPE_DIM + ROPE_DIM) ** -0.5)
    row2 = lambda a: a.reshape(1, -1).astype(F32)

    sizes = [D, D, D, dlora, alora, glora, qlora, kvlora, ROPE_DIM, D, D]
    offs = np.concatenate([[0], np.cumsum(sizes)]).tolist()
    col = lambda i: w_in[:, offs[i]:offs[i + 1]]
    lw_pad, la_pad = _round_up(dlora, LANES), _round_up(alora, LANES)
    padc = lambda a, n: jnp.pad(a, ((0, 0), (0, n - a.shape[1])))
    w5 = jnp.concatenate([col(0), col(1), col(2), col(9), col(10)], axis=1).astype(BF16)
    lora_w = lw_pad + la_pad + glora
    ws = jnp.concatenate([padc(col(3), lw_pad), padc(col(4), la_pad), col(5), col(6), col(7),
                          col(8), _rot_cols(col(8))], axis=1)
    col_cq, col_ckv, col_kr = lora_w, lora_w + qlora, lora_w + qlora + kvlora
    ws = padc(ws, _round_up(ws.shape[1], LANES)).astype(BF16)
    mu = rwkv_mu
    mur, muk, muv = row2(mu[:D]), row2(mu[D:2 * D]), row2(mu[2 * D:3 * D])
    m0 = 3 * D
    mul = jnp.concatenate([jnp.pad(mu[m0:m0 + dlora], (0, lw_pad - dlora)),
                           jnp.pad(mu[m0 + dlora:m0 + dlora + alora], (0, la_pad - alora)),
                           mu[m0 + dlora + alora:]]).reshape(1, -1)
    w2p = jnp.pad(rwkv_w2, ((0, lw_pad - dlora), (0, 0))).astype(BF16)
    a2p = jnp.pad(rwkv_a2, ((0, la_pad - alora), (0, 0))).astype(BF16)
    head_of = jnp.arange(D, dtype=jnp.int32) // HD
    e_sel = (head_of[:, None] == jnp.arange(LANES, dtype=jnp.int32)[None, :]).astype(BF16)
    et_sel = e_sel.T
    prep_consts = (mur, muk, muv, mul, row2(rwkv_w0), w2p, row2(rwkv_a0), a2p, rwkv_g2.astype(BF16),
                   row2(rwkv_k_k), row2(rwkv_k_a), row2(rwkv_r_k), e_sel, et_sel, lw_pad, la_pad)

    zq = jnp.zeros((qlora, Hm, QK_PAD - NOPE_DIM - ROPE_DIM), F32)
    q_nope_w, q_rope_w = mla_w_uq[..., :NOPE_DIM], mla_w_uq[..., NOPE_DIM:]
    wqa = jnp.concatenate([q_nope_w, q_rope_w, zq], axis=-1).reshape(qlora, Hm * QK_PAD).astype(BF16)
    wqb = jnp.concatenate([jnp.zeros_like(q_nope_w), _rot_cols(q_rope_w), zq],
                          axis=-1).reshape(qlora, Hm * QK_PAD).astype(BF16)
    zk = jnp.zeros((kvlora, Hm, QK_PAD - NOPE_DIM), F32)
    wk_top = jnp.concatenate([mla_w_uk, zk], axis=-1).reshape(kvlora, Hm * QK_PAD)
    sel = jnp.concatenate([jnp.zeros((ROPE_DIM, NOPE_DIM), F32), jnp.eye(ROPE_DIM, dtype=F32),
                           jnp.zeros((ROPE_DIM, QK_PAD - NOPE_DIM - ROPE_DIM), F32)], axis=-1)
    wk = jnp.concatenate([wk_top, jnp.tile(sel, (1, Hm))], axis=0).astype(BF16)
    wv = mla_w_uv.reshape(kvlora, Hm * V_DIM).astype(BF16)
    w_uk_t = jnp.transpose(mla_w_uk, (1, 2, 0)).astype(BF16)
    w_uv_h = jnp.transpose(mla_w_uv, (1, 0, 2)).astype(BF16)
    wo = w_out.astype(BF16)
    rw = jnp.pad(router_w, ((0, 0), (0, LANES - n_experts)))
    rb = jnp.pad(router_b, (0, LANES - n_experts)).reshape(1, LANES)
    w1g, w1l = _w1_prep(expert_w1, tr=512, tc=2048)
    b1g = expert_b1[:, None, 0::2]
    b1l = expert_b1[:, None, 1::2]
    w2e = expert_w2
    b2e = expert_b2[:, None, :]
    nmw = row2(norm_mix_w)

    def token_stage(x2d, shift_rows, n_groups, n_tiles, tt, shift, positions, tm_mla):
        p5 = _project(x2d, nmw, w5, normalize=True, tm=1024, tn=512)
        ps = _project(x2d, nmw, ws, normalize=True, tm=1024, tn=ws.shape[1])
        if shift_rows is None:
            s5 = jnp.zeros((n_groups, shift, w5.shape[1]), F32)
            ss = jnp.zeros((n_groups, shift, ws.shape[1]), F32)
        else:
            s5 = _project(shift_rows, nmw, w5, normalize=False, tm=1024, tn=512)
            ss = _project(shift_rows, nmw, ws, normalize=False, tm=1024, tn=ws.shape[1])
            s5 = s5.reshape(n_groups, shift, -1)
            ss = ss.reshape(n_groups, shift, -1)
        prep = _rwkv_prep(p5, ps, s5, ss, prep_consts, n_groups=n_groups, n_tiles=n_tiles, tt=tt,
                          shift=shift, D=D, lora_w=lora_w)
        qtab, ktab = _rope_tables(positions)
        mla = _mla_prep(ps, qtab, ktab, row2(mla_q_norm_w), row2(mla_kv_norm_w), wqa, wqb, wk, wv,
                        col_cq=col_cq, col_ckv=col_ckv, col_kr=col_kr, qlora=qlora, kvlora=kvlora,
                        n_heads=Hm, hg=min(4, Hm), tm=tm_mla, scale=scale)
        return p5, prep, mla

    xp = x_prompt.reshape(B * T, D)
    tt_p = min(128, T)
    p5_p, prep_p, mla_p = token_stage(xp, None, B, T // tt_p, tt_p, 1, jnp.arange(T), min(512, T))
    r_p, lw_p, k_p, v_p, al_p, be_p, g_p, bo_p = prep_p
    vsplit = 2 if (B * H * 2) % LANES == 0 and B * H < LANES else 1
    vr = HD // vsplit

    def to_kl(a):
        a = a.reshape(B, T, H, HD).transpose(1, 3, 0, 2).reshape(T, HD, B * H)
        return jnp.tile(a, (1, 1, vsplit))

    def to_vl(a):
        a = a.reshape(B, T, H, vsplit, vr).transpose(1, 4, 3, 0, 2)
        return a.reshape(T, vr, vsplit * B * H)

    nl_p = vsplit * B * H
    pad_l = _round_up(nl_p, LANES) - nl_p
    padl = lambda a: jnp.pad(a, ((0, 0), (0, 0), (0, pad_l))) if pad_l else a
    s0_p = jnp.zeros((HD, vr, nl_p + pad_l), F32)
    y_kl, s_kl = _rwkv_scan(padl(to_kl(r_p)), padl(to_kl(lw_p)), padl(to_kl(k_p)), padl(to_vl(v_p)),
                            padl(to_kl(al_p)), padl(to_kl(be_p)), s0_p, tb=64)
    y_p = y_kl[:, :, :nl_p].reshape(T, vr, vsplit, B, H).transpose(3, 0, 4, 2, 1).reshape(B * T, D)
    wkv_p = s_kl[:, :, :nl_p].reshape(HD, vr, vsplit, B, H).transpose(3, 4, 2, 1, 0).reshape(B, H, HD, HD)

    q_p, kf_p, vf_p, ckv_p, kr_p = mla_p
    o_mla_p = _flash_attention(q_p, kf_p, vf_p, B=B, T=T, n_heads=Hm, tq=512, tk=512)

    xs_tm = jnp.swapaxes(x_sample, 0, 1).reshape(Td * Bd, D)
    pos_s = jnp.repeat(past + jnp.arange(Td), Bd)
    p5_s, prep_s, mla_s = token_stage(xs_tm, state_shift, 1, Td, Bd, Bd, pos_s, Td * Bd)
    r_s, lw_s, k_s, v_s, al_s, be_s, g_s, bo_s = prep_s

    def to_kl_s(a):
        return a.reshape(Td, Bd, H, HD).transpose(0, 3, 1, 2).reshape(Td, HD, Bd * H)

    nl_s = Bd * H
    pad_s = _round_up(nl_s, LANES) - nl_s
    pads = lambda a: jnp.pad(a, ((0, 0), (0, 0), (0, pad_s))) if pad_s else a
    s0_s = pads(state_wkv.transpose(3, 2, 0, 1).reshape(HD, HD, nl_s))
    y_sl, s_sl = _rwkv_scan(pads(to_kl_s(r_s)), pads(to_kl_s(lw_s)), pads(to_kl_s(k_s)),
                            pads(to_kl_s(v_s)), pads(to_kl_s(al_s)), pads(to_kl_s(be_s)), s0_s, tb=Td)
    y_s = y_sl[:, :, :nl_s].reshape(Td, HD, Bd, H).transpose(0, 2, 3, 1).reshape(Td * Bd, D)
    wkv_s = s_sl[:, :, :nl_s].reshape(HD, HD, Bd, H).transpose(2, 3, 1, 0)

    q_s, _, _, ckv_s, kr_s = mla_s
    q4 = q_s.reshape(Td, Bd, Hm, QK_PAD)
    qn_h = q4[..., :NOPE_DIM].transpose(2, 1, 0, 3).reshape(Hm, Bd * Td, NOPE_DIM)
    q_lat = _head_matmul(qn_h, w_uk_t)
    q_lat = q_lat.reshape(Hm, Bd, Td, kvlora).transpose(1, 0, 2, 3).reshape(Bd, Hm * Td, kvlora)
    q_rp = q4[..., NOPE_DIM:NOPE_DIM + ROPE_DIM].transpose(1, 2, 0, 3).reshape(Bd, Hm * Td, ROPE_DIM)
    qs = jnp.concatenate([q_lat.astype(BF16), q_rp], axis=-1)
    ckv_s_bt = ckv_s.reshape(Td, Bd, kvlora).transpose(1, 0, 2)
    kr_s_bt = kr_s.reshape(Td, Bd, ROPE_DIM).transpose(1, 0, 2)
    npad = _round_up(Td, 8)
    padn = lambda a: jnp.pad(a, ((0, 0), (0, npad - Td), (0, 0)))
    o_lat = _paged_attention(page_table, qs, padn(ckv_s_bt), padn(kr_s_bt), cache_ckv, cache_krope,
                             pg=16, n_new=Td)
    o_lat_h = o_lat.reshape(Bd, Hm, Td, kvlora).transpose(1, 2, 0, 3).reshape(Hm, Td * Bd, kvlora)
    o_mla_s = _head_matmul(o_lat_h, w_uv_h)
    o_mla_s = o_mla_s.transpose(1, 0, 2).reshape(Td * Bd, Hm * V_DIM)

    mg = functools.partial(_merge, lnw=row2(rwkv_ln_w), lnb=row2(rwkv_ln_b), e=e_sel, et=et_sel, wo=wo,
                           nf=row2(norm_ffn_w), rw=rw, rb=rb, tm=128, n_experts=n_experts)
    h_p, hn_p, ti_p, tg_p = mg(xp, y_p, bo_p, g_p, p5_p, o_mla_p)
    h_s, hn_s, ti_s, tg_s = mg(xs_tm, y_s, bo_s, g_s, p5_s, o_mla_s)
    h_all = jnp.concatenate([h_p, h_s], axis=0)
    hn_all = jnp.concatenate([hn_p, hn_s], axis=0)
    ti_all = jnp.concatenate([ti_p, ti_s], axis=0)[:, :TOP_K]
    tg_all = jnp.concatenate([tg_p, tg_s], axis=0)
    M = h_all.shape[0]
    bm = 256
    n_half = 2 if F % (2 * LANES) == 0 else 1
    n_blocks = (M * TOP_K + n_experts * (bm - 1)) // bm
    dest, row_tok, sched = _moe_schedule(ti_all, n_experts, bm, n_half, n_blocks)
    hn_pad = jnp.concatenate([hn_all, jnp.zeros((1, D), BF16)], axis=0)
    xs_rows = hn_pad[row_tok]
    y_halves = _experts(sched, xs_rows, w1g, w1l, b1g, b1l, w2e, b2e, bm=bm, n_half=n_half)
    dest_slot_major = dest.reshape(M, TOP_K).T.reshape(-1)
    yg = y_halves[dest_slot_major].reshape(TOP_K, M, n_half * D)
    y_all = _combine(h_all, yg, tg_all, row2(norm_final_w), tm=128, n_half=n_half)

    y_prompt = y_all[:B * T].reshape(B, T, D)
    y_sample = y_all[B * T:].reshape(Td, Bd, D).transpose(1, 0, 2)
    last = jnp.concatenate([x_prompt[:, -1], x_sample[:, -1]], axis=0)
    shift_out = _rmsnorm_rows(last, nmw)
    return (y_prompt, y_sample,
            ckv_p.reshape(B, T // page, page, kvlora), kr_p.reshape(B, T // page, page, ROPE_DIM),
            wkv_p, shift_out[:B],
            ckv_s_bt, kr_s_bt, wkv_s, shift_out[B:])
```

```python
import functools

import numpy as np
import jax
import jax.numpy as jnp
from jax import lax
from jax.experimental import pallas as pl
from jax.experimental.pallas import tpu as pltpu

F32 = jnp.float32
BF16 = jnp.bfloat16

NORM_EPS = 1e-6
GN_EPS = 64e-5
RWKV_HEAD_DIM = 64
NOPE_DIM = 128
ROPE_DIM = 64
V_DIM = 128
QK_PAD = 256
ROPE_THETA = 10000.0
NEG_INF = -1e30
TOP_K = 4
SWIGLU_ALPHA = 1.702
SWIGLU_LIMIT = 7.0
LANES = 128
VMEM_LIMIT = 56 * 1024 * 1024


def _round_up(n, m):
    return (n + m - 1) // m * m


def _pick_tile(n, target, mult):
    if n <= target:
        return n
    t = target // mult * mult
    while n % t:
        t -= mult
    return t


def _cparams(*sem):
    return pltpu.CompilerParams(dimension_semantics=sem, vmem_limit_bytes=VMEM_LIMIT)


def _split3(x):
    hi = x.astype(BF16)
    r1 = x - hi.astype(F32)
    mid = r1.astype(BF16)
    lo = (r1 - mid.astype(F32)).astype(BF16)
    return hi, mid, lo


def _dot_sel(x, sel_bf16):
    hi, mid, lo = _split3(x)
    d = lambda a: jnp.dot(a, sel_bf16, preferred_element_type=F32)
    return d(hi) + d(mid) + d(lo)


def _tree_sum(xs):
    xs = list(xs)
    while len(xs) > 1:
        xs = [xs[i] + xs[i + 1] for i in range(0, len(xs) - 1, 2)] + ([xs[-1]] if len(xs) % 2 else [])
    return xs[0]


def _rms(x, w):
    return x * lax.rsqrt(jnp.mean(x * x, axis=-1, keepdims=True) + NORM_EPS) * w


def _proj_kernel(x_ref, nw_ref, w_ref, o_ref, xn_ref, *, normalize):
    @pl.when(pl.program_id(1) == 0)
    def _():
        x = x_ref[...]
        if normalize:
            x = _rms(x, nw_ref[...])
        xn_ref[...] = x.astype(BF16)

    o_ref[...] = jnp.dot(xn_ref[...], w_ref[...], preferred_element_type=F32)


def _project(x, norm_w, w, *, normalize, tm, tn):
    M, D = x.shape
    N = w.shape[1]
    tm = _pick_tile(M, tm, 8)
    tn = _pick_tile(N, tn, LANES)
    return pl.pallas_call(
        functools.partial(_proj_kernel, normalize=normalize),
        grid=(M // tm, N // tn),
        in_specs=[pl.BlockSpec((tm, D), lambda i, j: (i, 0)),
                  pl.BlockSpec((1, D), lambda i, j: (0, 0)),
                  pl.BlockSpec((D, tn), lambda i, j: (0, j))],
        out_specs=pl.BlockSpec((tm, tn), lambda i, j: (i, j)),
        out_shape=jax.ShapeDtypeStruct((M, N), F32),
        scratch_shapes=[pltpu.VMEM((tm, D), BF16)],
        compiler_params=_cparams("parallel", "arbitrary"),
        name="norm_proj",
    )(x, norm_w, w)


def _rmsnorm_rows_kernel(x_ref, w_ref, o_ref):
    o_ref[...] = _rms(x_ref[...], w_ref[...])


def _rmsnorm_rows(x, w):
    return pl.pallas_call(
        _rmsnorm_rows_kernel,
        out_shape=jax.ShapeDtypeStruct(x.shape, F32),
        name="rmsnorm_rows",
    )(x, w)


def _rwkv_prep_math(r, k, v, lo, w0_ref, w2_ref, a0_ref, a2_ref, g2_ref, kk_ref, ka_ref, rk_ref,
                    e_ref, et_ref, *, lw_pad, la_pad):
    hw = lo[:, :lw_pad]
    ha = lo[:, lw_pad:lw_pad + la_pad]
    hg = lo[:, lw_pad + la_pad:]
    mm = lambda a, b_ref: jnp.dot(a.astype(BF16), b_ref[...], preferred_element_type=F32)
    w = -jax.nn.softplus(-(w0_ref[...] + mm(jnp.tanh(hw), w2_ref))) - 0.5
    a = jax.nn.sigmoid(a0_ref[...] + mm(ha, a2_ref))
    g = mm(jax.nn.sigmoid(hg), g2_ref)
    e = e_ref[...]
    et = et_ref[...]
    kk = k * kk_ref[...]
    ssq = _dot_sel(kk * kk, e)
    inv = lax.rsqrt(jnp.maximum(ssq, 1e-24))
    kk = kk * _dot_sel(inv, et)
    k2 = k * (1.0 + (a - 1.0) * ka_ref[...])
    bsum = _dot_sel(_dot_sel(r * k2 * rk_ref[...], e), et)
    return r, -jnp.exp(w), k2, v, -kk, kk * a, g, bsum * v


def _rwkv_prep_packed_kernel(r_ref, k_ref, v_ref, l_ref,
                             mur_ref, muk_ref, muv_ref, mul_ref, w0_ref, w2_ref, a0_ref, a2_ref,
                             g2_ref, kk_ref, ka_ref, rk_ref, e_ref, et_ref,
                             zr_ref, zw_ref, zk_ref, zv_ref, za_ref, zb_ref, go_ref, bo_ref,
                             cr_ref, ck_ref, cv_ref, cl_ref, *, lw_pad, la_pad, n_heads, vsplit):
    t = pl.program_id(0)
    n_seq, tt, D = r_ref.shape
    nk = D // n_heads
    vr = nk // vsplit

    @pl.when(t == 0)
    def _():
        for c_ref in (cr_ref, ck_ref, cv_ref, cl_ref):
            c_ref[...] = jnp.zeros(c_ref.shape, F32)

    def lerp(p, c_ref, b, mu_ref):
        rolled = pltpu.roll(p, 1, axis=0)
        row = lax.broadcasted_iota(jnp.int32, p.shape, 0)
        prev = jnp.where(row == 0, c_ref[b], rolled)
        c_ref[b] = p[tt - 1:, :]
        return p + (prev - p) * mu_ref[...]

    outs = []
    for b in range(n_seq):
        r = lerp(r_ref[b], cr_ref, b, mur_ref)
        k = lerp(k_ref[b], ck_ref, b, muk_ref)
        v = lerp(v_ref[b], cv_ref, b, muv_ref)
        lo = lerp(l_ref[b], cl_ref, b, mul_ref)
        res = _rwkv_prep_math(r, k, v, lo, w0_ref, w2_ref, a0_ref, a2_ref, g2_ref, kk_ref, ka_ref,
                              rk_ref, e_ref, et_ref, lw_pad=lw_pad, la_pad=la_pad)
        go_ref[b] = res[6]
        bo_ref[b] = res[7]
        outs.append(res)

    used = n_seq * vsplit * n_heads
    zpad = [jnp.zeros((tt, LANES - used), F32)] if used < LANES else []
    piece = lambda x, j: x[:, j * n_heads:(j + 1) * n_heads]
    for z_ref, idx in ((zr_ref, 0), (zw_ref, 1), (zk_ref, 2), (za_ref, 4), (zb_ref, 5)):
        for kq in range(nk):
            parts = [piece(outs[b][idx], kq) for b in range(n_seq) for _ in range(vsplit)]
            z_ref[kq * tt:(kq + 1) * tt, :] = jnp.concatenate(parts + zpad, axis=-1)
    for vq in range(vr):
        parts = [piece(outs[b][3], vs * vr + vq) for b in range(n_seq) for vs in range(vsplit)]
        zv_ref[vq * tt:(vq + 1) * tt, :] = jnp.concatenate(parts + zpad, axis=-1)


def _rwkv_prep_packed(p5, ps, consts, *, B, T, tt, D, lora_w, n_heads, vsplit):
    (mur, muk, muv, mul, w0, w2p, a0, a2p, g2, k_k, k_a, r_k, e, et, lw_pad, la_pad) = consts
    nk = D // n_heads
    vr = nk // vsplit
    n_tiles = T // tt
    big = lambda c: pl.BlockSpec((B, tt, D), lambda t, c=c: (0, t, c))
    full = lambda a: pl.BlockSpec(a.shape, lambda t: (0,) * a.ndim)
    in_specs = [big(0), big(1), big(2), pl.BlockSpec((B, tt, lora_w), lambda t: (0, t, 0)),
                full(mur), full(muk), full(muv), full(mul), full(w0), full(w2p), full(a0),
                full(a2p), full(g2), full(k_k), full(k_a), full(r_k), full(e), full(et)]
    zspec = pl.BlockSpec((nk * tt, LANES), lambda t: (t, 0))
    vspec = pl.BlockSpec((vr * tt, LANES), lambda t: (t, 0))
    tok = pl.BlockSpec((B, tt, D), lambda t: (0, t, 0))
    zshape = jax.ShapeDtypeStruct((n_tiles * nk * tt, LANES), F32)
    vshape = jax.ShapeDtypeStruct((n_tiles * vr * tt, LANES), F32)
    tshape = jax.ShapeDtypeStruct((B, T, D), F32)
    return pl.pallas_call(
        functools.partial(_rwkv_prep_packed_kernel, lw_pad=lw_pad, la_pad=la_pad, n_heads=n_heads,
                          vsplit=vsplit),
        grid=(n_tiles,),
        in_specs=in_specs,
        out_specs=[zspec, zspec, zspec, vspec, zspec, zspec, tok, tok],
        out_shape=[zshape, zshape, zshape, vshape, zshape, zshape, tshape, tshape],
        scratch_shapes=[pltpu.VMEM((B, 1, D), F32)] * 3 + [pltpu.VMEM((B, 1, lora_w), F32)],
        compiler_params=_cparams("arbitrary"),
        name="rwkv_prep_packed",
    )(p5, p5, p5, ps, mur, muk, muv, mul, w0, w2p, a0, a2p, g2, k_k, k_a, r_k, e, et)


def _rwkv_prep_kernel(r_ref, k_ref, v_ref, l_ref, sr_ref, sk_ref, sv_ref, sl_ref,
                      mur_ref, muk_ref, muv_ref, mul_ref, w0_ref, w2_ref, a0_ref, a2_ref, g2_ref,
                      kk_ref, ka_ref, rk_ref, e_ref, et_ref,
                      ro_ref, lw_ref, ko_ref, vo_ref, al_ref, be_ref, go_ref, bo_ref,
                      cr_ref, ck_ref, cv_ref, cl_ref, *, shift, lw_pad, la_pad):
    t = pl.program_id(1)

    @pl.when(t == 0)
    def _():
        cr_ref[...] = sr_ref[...]
        ck_ref[...] = sk_ref[...]
        cv_ref[...] = sv_ref[...]
        cl_ref[...] = sl_ref[...]

    def lerp(p_ref, c_ref, mu_ref):
        p = p_ref[...]
        if shift == 1:
            rolled = pltpu.roll(p, 1, axis=0)
            row = lax.broadcasted_iota(jnp.int32, p.shape, 0)
            prev = jnp.where(row == 0, c_ref[...], rolled)
            c_new = p[p.shape[0] - 1:, :]
        else:
            prev = c_ref[...]
            c_new = p
        xx = p + (prev - p) * mu_ref[...]
        c_ref[...] = c_new
        return xx

    r = lerp(r_ref, cr_ref, mur_ref)
    k = lerp(k_ref, ck_ref, muk_ref)
    v = lerp(v_ref, cv_ref, muv_ref)
    lo = lerp(l_ref, cl_ref, mul_ref)
    r, lw, k2, v, al, be, g, bonus = _rwkv_prep_math(
        r, k, v, lo, w0_ref, w2_ref, a0_ref, a2_ref, g2_ref, kk_ref, ka_ref, rk_ref, e_ref, et_ref,
        lw_pad=lw_pad, la_pad=la_pad)
    ro_ref[...] = r
    lw_ref[...] = lw
    ko_ref[...] = k2
    vo_ref[...] = v
    al_ref[...] = al
    be_ref[...] = be
    go_ref[...] = g
    bo_ref[...] = bonus


def _rwkv_prep(p5, ps, shift5, shifts, consts, *, n_groups, n_tiles, tt, shift, D, lora_w):
    M = p5.shape[0]
    (mur, muk, muv, mul, w0, w2p, a0, a2p, g2, k_k, k_a, r_k, e, et, lw_pad, la_pad) = consts
    row = lambda g, t: (g * n_tiles + t)
    big = lambda c: pl.BlockSpec((tt, D), lambda g, t, c=c: (row(g, t), c))
    sh = lambda c: pl.BlockSpec((None, shift, D), lambda g, t, c=c: (g, 0, c))
    full = lambda a: pl.BlockSpec(a.shape, lambda g, t: (0,) * a.ndim)
    in_specs = [big(0), big(1), big(2),
                pl.BlockSpec((tt, lora_w), lambda g, t: (row(g, t), 0)),
                sh(0), sh(1), sh(2),
                pl.BlockSpec((None, shift, lora_w), lambda g, t: (g, 0, 0)),
                full(mur), full(muk), full(muv), full(mul), full(w0), full(w2p), full(a0),
                full(a2p), full(g2), full(k_k), full(k_a), full(r_k), full(e), full(et)]
    out_spec = pl.BlockSpec((tt, D), lambda g, t: (row(g, t), 0))
    outs = pl.pallas_call(
        functools.partial(_rwkv_prep_kernel, shift=shift, lw_pad=lw_pad, la_pad=la_pad),
        grid=(n_groups, n_tiles),
        in_specs=in_specs,
        out_specs=[out_spec] * 8,
        out_shape=[jax.ShapeDtypeStruct((M, D), F32)] * 8,
        scratch_shapes=[pltpu.VMEM((shift, D), F32)] * 3 + [pltpu.VMEM((shift, lora_w), F32)],
        compiler_params=_cparams("parallel", "arbitrary"),
        name="rwkv_prep",
    )(p5, p5, p5, ps, shift5, shift5, shift5, shifts,
      mur, muk, muv, mul, w0, w2p, a0, a2p, g2, k_k, k_a, r_k, e, et)
    return outs


def _rwkv_scan_kernel(r_ref, w_ref, k_ref, v_ref, a_ref, b_ref, s0_ref, y_ref, sf_ref, s_ref, w_scr,
                      *, tb, vr, nk, packed):
    tblk = pl.program_id(1)

    @pl.when(tblk == 0)
    def _():
        s_ref[...] = s0_ref[...]

    n_acc = 8
    if packed:
        row = lambda ref, t, k: jnp.broadcast_to(ref[pl.ds(k * tb + t, 1), :], (vr, LANES))
        tile = lambda ref, t, n: ref[pl.ds(t, n, stride=tb), :]
    else:
        row = lambda ref, t, k: jnp.broadcast_to(ref[t, pl.ds(k, 1), :], (vr, LANES))
        tile = lambda ref, t, n: ref[t]

    def accumulate(parts, k, term):
        parts[k % n_acc] = term if parts[k % n_acc] is None else parts[k % n_acc] + term

    sa_parts = [None] * n_acc
    for k in range(nk):
        accumulate(sa_parts, k, s_ref[k] * row(a_ref, 0, k))
    sa0 = _tree_sum(sa_parts)

    def step(t, sa):
        w_scr[...] = jnp.exp(tile(w_ref, t, nk))
        vt = tile(v_ref, t, vr)
        tn = jnp.minimum(t + 1, tb - 1)
        y_parts = [None] * n_acc
        sa_parts = [None] * n_acc
        for k in range(nk):
            wk = jnp.broadcast_to(w_scr[pl.ds(k, 1), :], (vr, LANES))
            s = s_ref[k] * wk + sa * row(b_ref, t, k) + vt * row(k_ref, t, k)
            s_ref[k] = s
            accumulate(y_parts, k, s * row(r_ref, t, k))
            accumulate(sa_parts, k, s * row(a_ref, tn, k))
        y_ref[t] = _tree_sum(y_parts)
        return _tree_sum(sa_parts)

    lax.fori_loop(0, tb, step, sa0)

    @pl.when(tblk == pl.num_programs(1) - 1)
    def _():
        sf_ref[...] = s_ref[...]


def _rwkv_scan(r, w, k, v, a, b, s0, *, tb, packed=False):
    K, VR, NL = s0.shape
    if packed:
        T = r.shape[0] // K
        assert NL == LANES and T % tb == 0
        kspec = pl.BlockSpec((K * tb, LANES), lambda n, t: (t, 0))
        vin = pl.BlockSpec((VR * tb, LANES), lambda n, t: (t, 0))
    else:
        T = r.shape[0]
        tb = min(tb, T)
        assert T % tb == 0 and NL % LANES == 0
        kspec = pl.BlockSpec((tb, K, LANES), lambda n, t: (t, 0, n))
        vin = pl.BlockSpec((tb, VR, LANES), lambda n, t: (t, 0, n))
    vspec = pl.BlockSpec((tb, VR, LANES), lambda n, t: (t, 0, n))
    sspec = pl.BlockSpec((K, VR, LANES), lambda n, t: (0, 0, n))
    return pl.pallas_call(
        functools.partial(_rwkv_scan_kernel, tb=tb, vr=VR, nk=K, packed=packed),
        grid=(NL // LANES, T // tb),
        in_specs=[kspec, kspec, kspec, vin, kspec, kspec, sspec],
        out_specs=[vspec, sspec],
        out_shape=[jax.ShapeDtypeStruct((T, VR, NL), F32),
                   jax.ShapeDtypeStruct((K, VR, NL), F32)],
        scratch_shapes=[pltpu.VMEM((K, VR, LANES), F32), pltpu.VMEM((K, LANES), F32)],
        compiler_params=_cparams("parallel", "arbitrary"),
        name="rwkv_scan",
    )(r, w, k, v, a, b, s0)


def _mla_prep_kernel(cq_ref, ckv_ref, kr_ref, qtab_ref, ktab_ref, qn_ref, kvn_ref,
                     wqa_ref, wqb_ref, wk_ref, wv_ref,
                     q_ref, k_ref, v_ref, ckv_o_ref, kr_o_ref, cqn_ref, ckk_ref, *, hg, scale):
    @pl.when(pl.program_id(1) == 0)
    def _():
        cqn_ref[...] = _rms(cq_ref[...], qn_ref[...]).astype(BF16)
        ckv = _rms(ckv_ref[...], kvn_ref[...])
        ckv_o_ref[...] = ckv
        kr = kr_ref[...]
        ktab = ktab_ref[...]
        krope = kr[:, :ROPE_DIM] * ktab[:, :ROPE_DIM] + kr[:, ROPE_DIM:] * ktab[:, ROPE_DIM:]
        kr_o_ref[...] = krope
        ckk_ref[...] = jnp.concatenate([ckv, krope], axis=-1).astype(BF16)

    cqn = cqn_ref[...]
    qtab = qtab_ref[...]
    cosf = jnp.concatenate([qtab[:, :QK_PAD]] * hg, axis=-1)
    sinf = jnp.concatenate([qtab[:, QK_PAD:]] * hg, axis=-1)
    qa = jnp.dot(cqn, wqa_ref[...], preferred_element_type=F32)
    qb = jnp.dot(cqn, wqb_ref[...], preferred_element_type=F32)
    q_ref[...] = ((qa * cosf + qb * sinf) * scale).astype(BF16)
    ckk = ckk_ref[...]
    k_ref[...] = jnp.dot(ckk, wk_ref[...], preferred_element_type=F32).astype(BF16)
    v_ref[...] = jnp.dot(ckk[:, :ckk.shape[1] - ROPE_DIM], wv_ref[...],
                         preferred_element_type=F32).astype(BF16)


def _mla_prep(ps, qtab, ktab, qn, kvn, wqa, wqb, wk, wv, *, col_cq, col_ckv, col_kr, qlora, kvlora,
              n_heads, hg, tm, scale):
    M = ps.shape[0]
    tm = _pick_tile(M, tm, 8)
    n_tab = qtab.shape[0] // tm
    grid = (M // tm, n_heads // hg)
    tabspec = lambda a: pl.BlockSpec((tm, a.shape[1]), lambda i, j: (i % n_tab, 0))
    full = lambda a: pl.BlockSpec(a.shape, lambda i, j: (0,) * a.ndim)
    in_specs = [pl.BlockSpec((tm, qlora), lambda i, j: (i, col_cq // qlora)),
                pl.BlockSpec((tm, kvlora), lambda i, j: (i, col_ckv // kvlora)),
                pl.BlockSpec((tm, 2 * ROPE_DIM), lambda i, j: (i, col_kr // (2 * ROPE_DIM))),
                tabspec(qtab), tabspec(ktab), full(qn), full(kvn),
                pl.BlockSpec((qlora, hg * QK_PAD), lambda i, j: (0, j)),
                pl.BlockSpec((qlora, hg * QK_PAD), lambda i, j: (0, j)),
                pl.BlockSpec((kvlora + ROPE_DIM, hg * QK_PAD), lambda i, j: (0, j)),
                pl.BlockSpec((kvlora, hg * V_DIM), lambda i, j: (0, j))]
    out_specs = [pl.BlockSpec((tm, hg * QK_PAD), lambda i, j: (i, j)),
                 pl.BlockSpec((tm, hg * QK_PAD), lambda i, j: (i, j)),
                 pl.BlockSpec((tm, hg * V_DIM), lambda i, j: (i, j)),
                 pl.BlockSpec((tm, kvlora), lambda i, j: (i, 0)),
                 pl.BlockSpec((tm, ROPE_DIM), lambda i, j: (i, 0))]
    out_shape = [jax.ShapeDtypeStruct((M, n_heads * QK_PAD), BF16),
                 jax.ShapeDtypeStruct((M, n_heads * QK_PAD), BF16),
                 jax.ShapeDtypeStruct((M, n_heads * V_DIM), BF16),
                 jax.ShapeDtypeStruct((M, kvlora), F32),
                 jax.ShapeDtypeStruct((M, ROPE_DIM), F32)]
    return pl.pallas_call(
        functools.partial(_mla_prep_kernel, hg=hg, scale=scale),
        grid=grid, in_specs=in_specs, out_specs=out_specs, out_shape=out_shape,
        scratch_shapes=[pltpu.VMEM((tm, qlora), BF16), pltpu.VMEM((tm, kvlora + ROPE_DIM), BF16)],
        compiler_params=_cparams("parallel", "arbitrary"),
        name="mla_prep",
    )(ps, ps, ps, qtab, ktab, qn, kvn, wqa, wqb, wk, wv)


def _flash_kernel(q_ref, k_ref, v_ref, o_ref, *, tq, tk):
    i = pl.program_id(2)
    q = q_ref[...]
    n_sub = tq // tk

    def block(j, carry, masked):
        m, l, acc = carry
        start = pl.multiple_of(j * tk, tk)
        kb = k_ref[pl.ds(start, tk), :]
        vb = v_ref[pl.ds(start, tk), :]
        s = lax.dot_general(q, kb, (((1,), (1,)), ((), ())), preferred_element_type=F32)
        if masked:
            qpos = i * tq + lax.broadcasted_iota(jnp.int32, s.shape, 0)
            kpos = j * tk + lax.broadcasted_iota(jnp.int32, s.shape, 1)
            s = jnp.where(kpos <= qpos, s, NEG_INF)
        m_new = jnp.maximum(m, jnp.max(s, axis=-1, keepdims=True))
        p = jnp.exp(s - m_new)
        corr = jnp.exp(m - m_new)
        l = corr * l + jnp.sum(p, axis=-1, keepdims=True)
        acc = corr * acc + jnp.dot(p.astype(BF16), vb, preferred_element_type=F32)
        return m_new, l, acc

    init = (jnp.full((tq, 1), NEG_INF, F32), jnp.zeros((tq, 1), F32),
            jnp.zeros((tq, v_ref.shape[1]), F32))
    carry = lax.fori_loop(0, i * n_sub, lambda j, c: block(j, c, False), init)
    for d in range(n_sub):
        carry = block(i * n_sub + d, carry, True)
    m, l, acc = carry
    o_ref[...] = (acc / l).astype(o_ref.dtype)


def _flash_attention(q, k, v, *, B, T, n_heads, tq, tk):
    tq = min(tq, T)
    tk = min(tk, tq)
    nq = T // tq
    return pl.pallas_call(
        functools.partial(_flash_kernel, tq=tq, tk=tk),
        grid=(B, n_heads, nq),
        in_specs=[pl.BlockSpec((tq, QK_PAD), lambda b, h, i: (b * nq + i, h)),
                  pl.BlockSpec((T, QK_PAD), lambda b, h, i: (b, h)),
                  pl.BlockSpec((T, V_DIM), lambda b, h, i: (b, h))],
        out_specs=pl.BlockSpec((tq, V_DIM), lambda b, h, i: (b * nq + i, h)),
        out_shape=jax.ShapeDtypeStruct((B * T, n_heads * V_DIM), F32),
        compiler_params=_cparams("parallel", "parallel", "arbitrary"),
        name="mla_flash",
    )(q, k, v)


def _bmm_kernel(x_ref, w_ref, o_ref):
    o_ref[...] = jnp.dot(x_ref[...].astype(BF16), w_ref[...], preferred_element_type=F32)


def _head_matmul(x, w):
    H, R, K = x.shape
    N = w.shape[2]
    return pl.pallas_call(
        _bmm_kernel,
        grid=(H,),
        in_specs=[pl.BlockSpec((None, R, K), lambda h: (h, 0, 0)),
                  pl.BlockSpec((None, K, N), lambda h: (h, 0, 0))],
        out_specs=pl.BlockSpec((None, R, N), lambda h: (h, 0, 0)),
        out_shape=jax.ShapeDtypeStruct((H, R, N), F32),
        compiler_params=_cparams("parallel"),
        name="head_matmul",
    )(x, w)


def _paged_kernel(pt_ref, q_ref, cn_ref, kn_ref, *refs, pg, n_new, kvlora):
    ckv_refs = refs[:pg]
    kr_refs = refs[pg:2 * pg]
    o_ref = refs[2 * pg]
    m_ref, l_ref, acc_ref = refs[2 * pg + 1:]
    g = pl.program_id(1)

    @pl.when(g == 0)
    def _():
        m_ref[...] = jnp.full(m_ref.shape, NEG_INF, F32)
        l_ref[...] = jnp.zeros(l_ref.shape, F32)
        acc_ref[...] = jnp.zeros(acc_ref.shape, F32)

    q = q_ref[...]
    ql = q[:, :kvlora]
    qr = q[:, kvlora:]
    dn = (((1,), (1,)), ((), ()))

    def scores(c_ref, k_ref, k_is_transposed):
        kb = k_ref[...].astype(BF16)
        if k_is_transposed:
            rope = jnp.dot(qr, kb, preferred_element_type=F32)
        else:
            rope = lax.dot_general(qr, kb, dn, preferred_element_type=F32)
        return lax.dot_general(ql, c_ref[...].astype(BF16), dn, preferred_element_type=F32) + rope

    def update(s, value_refs):
        m = m_ref[...]
        m_new = jnp.maximum(m, jnp.max(s, axis=-1, keepdims=True))
        p = jnp.exp(s - m_new).astype(BF16)
        corr = jnp.exp(m - m_new)
        l_ref[...] = corr * l_ref[...] + jnp.sum(p.astype(F32), axis=-1, keepdims=True)
        pv = None
        off = 0
        for c_ref in value_refs:
            n = c_ref.shape[0]
            part = jnp.dot(p[:, off:off + n], c_ref[...].astype(BF16), preferred_element_type=F32)
            pv = part if pv is None else pv + part
            off += n
        acc_ref[...] = corr * acc_ref[...] + pv
        m_ref[...] = m_new

    s_all = jnp.concatenate([scores(c, k, True) for c, k in zip(ckv_refs, kr_refs)], axis=-1)
    update(s_all, ckv_refs)

    @pl.when(g == pl.num_programs(1) - 1)
    def _():
        rows = q.shape[0]
        npad = cn_ref.shape[0]
        tq = lax.broadcasted_iota(jnp.int32, (rows, npad), 0) % n_new
        kj = lax.broadcasted_iota(jnp.int32, (rows, npad), 1)
        update(jnp.where(kj <= tq, scores(cn_ref, kn_ref, False), NEG_INF), [cn_ref])
        o_ref[...] = acc_ref[...] / l_ref[...]


def _paged_attention(page_table, qs, ckv_new, kr_new, cache_ckv, cache_krope, *, pg, n_new):
    Bd, R, QW = qs.shape
    kvlora = cache_ckv.shape[2]
    page = cache_ckv.shape[1]
    n_pages = page_table.shape[1]
    pg = min(pg, n_pages)
    assert n_pages % pg == 0
    npad = ckv_new.shape[1]
    cspec = lambda i: pl.BlockSpec((None, page, kvlora), lambda b, g, pt, i=i: (pt[b, g * pg + i], 0, 0))
    krope_t = jnp.swapaxes(cache_krope, 1, 2)
    kspec = lambda i: pl.BlockSpec((None, ROPE_DIM, page), lambda b, g, pt, i=i: (pt[b, g * pg + i], 0, 0))
    in_specs = ([pl.BlockSpec((None, R, QW), lambda b, g, pt: (b, 0, 0)),
                 pl.BlockSpec((None, npad, kvlora), lambda b, g, pt: (b, 0, 0)),
                 pl.BlockSpec((None, npad, ROPE_DIM), lambda b, g, pt: (b, 0, 0))]
                + [cspec(i) for i in range(pg)] + [kspec(i) for i in range(pg)])
    grid_spec = pltpu.PrefetchScalarGridSpec(
        num_scalar_prefetch=1, grid=(Bd, n_pages // pg), in_specs=in_specs,
        out_specs=pl.BlockSpec((None, R, kvlora), lambda b, g, pt: (b, 0, 0)),
        scratch_shapes=[pltpu.VMEM((R, 1), F32), pltpu.VMEM((R, 1), F32), pltpu.VMEM((R, kvlora), F32)])
    return pl.pallas_call(
        functools.partial(_paged_kernel, pg=pg, n_new=n_new, kvlora=kvlora),
        grid_spec=grid_spec,
        out_shape=jax.ShapeDtypeStruct((Bd, R, kvlora), F32),
        compiler_params=_cparams("parallel", "arbitrary"),
        name="mla_paged",
    )(page_table, qs, ckv_new, kr_new, *([cache_ckv] * pg), *([krope_t] * pg))


def _merge_rows(x, y, bo, g, ga, gb, om, lnw_ref, lnb_ref, e_ref, et_ref, wor_ref, wom_ref, nf_ref,
                rw_ref, rb_ref, *, n_experts):
    e = e_ref[...]
    et = et_ref[...]
    inv_n = 1.0 / RWKV_HEAD_DIM
    mean = _dot_sel(_dot_sel(y, e) * inv_n, et)
    yc = y - mean
    var = _dot_sel(_dot_sel(yc * yc, e) * inv_n, et)
    yn = yc * lax.rsqrt(var + GN_EPS) * lnw_ref[...] + lnb_ref[...]
    o_rwkv = (yn + bo) * g
    h = (x + jnp.dot((jax.nn.sigmoid(ga) * o_rwkv).astype(BF16), wor_ref[...], preferred_element_type=F32)
         + jnp.dot((jax.nn.sigmoid(gb) * om).astype(BF16), wom_ref[...], preferred_element_type=F32))
    hn = _rms(h, nf_ref[...])

    rw = rw_ref[...]
    w_hi = rw.astype(BF16)
    w_r1 = rw - w_hi.astype(F32)
    w_mid = w_r1.astype(BF16)
    w_lo = (w_r1 - w_mid.astype(F32)).astype(BF16)
    x_hi, x_mid, x_lo = _split3(hn)
    d = lambda a, b: jnp.dot(a, b, preferred_element_type=F32)
    logits = (d(x_hi, w_hi) + (d(x_hi, w_mid) + d(x_mid, w_hi))
              + (d(x_hi, w_lo) + d(x_mid, w_mid) + d(x_lo, w_hi))) + rb_ref[...]
    lane = lax.broadcasted_iota(jnp.int32, logits.shape, 1)
    work = jnp.where(lane < n_experts, logits, -jnp.inf)
    vals, idxs = [], []
    for _ in range(TOP_K):
        mx = jnp.max(work, axis=-1, keepdims=True)
        ix = jnp.min(jnp.where(work == mx, lane, LANES), axis=-1, keepdims=True)
        vals.append(mx)
        idxs.append(ix)
        work = jnp.where(lane == ix, -jnp.inf, work)
    ex = [jnp.exp(vv - vals[0]) for vv in vals]
    den = ex[0] + ex[1] + ex[2] + ex[3]
    ti = jnp.zeros(logits.shape, jnp.int32)
    tg = jnp.zeros(logits.shape, F32)
    for j in range(TOP_K):
        ti = jnp.where(lane == j, idxs[j], ti)
        tg = jnp.where(lane == j, ex[j] / den, tg)
    return h, hn.astype(BF16), ti, tg


def _merge_kernel(x_ref, y_ref, bo_ref, g_ref, ga_ref, gb_ref, om_ref, *refs, n_experts, n_heads,
                  vsplit):
    consts, (h_ref, hn_ref, ti_ref, tg_ref) = refs[:-4], refs[-4:]
    n_seq, tm, D = x_ref.shape
    for b in range(n_seq):
        if vsplit:
            vr = D // n_heads // vsplit
            pieces = [None] * (vr * vsplit)
            for vq in range(vr):
                rows = y_ref[pl.ds(vq, tm, stride=vr), :]
                for vs in range(vsplit):
                    lo = (b * vsplit + vs) * n_heads
                    pieces[vs * vr + vq] = rows[:, lo:lo + n_heads]
            y = jnp.concatenate(pieces, axis=-1)
        else:
            y = y_ref[b]
        h, hn, ti, tg = _merge_rows(x_ref[b], y, bo_ref[b], g_ref[b], ga_ref[b], gb_ref[b], om_ref[b],
                                    *consts, n_experts=n_experts)
        h_ref[b] = h
        hn_ref[b] = hn
        ti_ref[b] = ti
        tg_ref[b] = tg


def _merge(x, y, bonus, g, p5, o_mla, consts, *, tm, n_experts, n_heads, vsplit):
    G, R, D = x.shape
    tm = _pick_tile(R, tm, 8)
    blk = lambda c=0: pl.BlockSpec((G, tm, D), lambda i, c=c: (0, i, c))
    full = lambda a: pl.BlockSpec(a.shape, lambda i: (0,) * a.ndim)
    if vsplit:
        vr = D // n_heads // vsplit
        yspec = pl.BlockSpec((tm * vr, LANES), lambda i: (i, 0))
    else:
        yspec = blk()
    in_specs = [blk(), yspec, blk(), blk(), blk(3), blk(4), blk()] + [full(a) for a in consts]
    lane_blk = pl.BlockSpec((G, tm, LANES), lambda i: (0, i, 0))
    out_specs = [blk(), blk(), lane_blk, lane_blk]
    out_shape = [jax.ShapeDtypeStruct((G, R, D), F32), jax.ShapeDtypeStruct((G, R, D), BF16),
                 jax.ShapeDtypeStruct((G, R, LANES), jnp.int32),
                 jax.ShapeDtypeStruct((G, R, LANES), F32)]
    return pl.pallas_call(
        functools.partial(_merge_kernel, n_experts=n_experts, n_heads=n_heads, vsplit=vsplit),
        grid=(R // tm,), in_specs=in_specs, out_specs=out_specs, out_shape=out_shape,
        compiler_params=_cparams("parallel"),
        name="merge_router",
    )(x, y, bonus, g, p5, p5, o_mla, *consts)


def _w1_prep_kernel(w_ref, p_ref, g_ref, l_ref):
    perm = p_ref[...]
    half = perm.shape[0] // 2
    for c in range(w_ref.shape[1] // perm.shape[0]):
        chunk = w_ref[:, c * 2 * half:(c + 1) * 2 * half].astype(BF16)
        both = jnp.dot(chunk, perm, preferred_element_type=F32)
        g_ref[:, c * half:(c + 1) * half] = both[:, :half].astype(BF16)
        l_ref[:, c * half:(c + 1) * half] = both[:, half:].astype(BF16)


def _w1_prep(w1, *, tr, tc):
    E, D, F2 = w1.shape
    pw = 2 * LANES
    tr = _pick_tile(D, tr, 8)
    tc = _pick_tile(F2, tc, pw)
    j = np.arange(LANES)
    perm = np.zeros((pw, pw), np.float32)
    perm[2 * j, j] = 1.0
    perm[2 * j + 1, LANES + j] = 1.0
    out = jax.ShapeDtypeStruct((E, D, F2 // 2), BF16)
    return pl.pallas_call(
        _w1_prep_kernel,
        grid=(E, D // tr, F2 // tc),
        in_specs=[pl.BlockSpec((None, tr, tc), lambda e, i, j: (e, i, j)),
                  pl.BlockSpec((pw, pw), lambda e, i, j: (0, 0))],
        out_specs=[pl.BlockSpec((None, tr, tc // 2), lambda e, i, j: (e, i, j))] * 2,
        out_shape=[out, out],
        compiler_params=_cparams("parallel", "parallel", "parallel"),
        name="moe_w1_prep",
    )(w1, jnp.asarray(perm, BF16))


def _expert_kernel(se_ref, sh_ref, sr_ref, sm_ref, x_ref, w1g_ref, w1l_ref, b1g_ref, b1l_ref,
                   w2_ref, b2_ref, o_ref, w2b_ref):
    s = pl.program_id(0)
    mode = sm_ref[s]

    @pl.when(mode == 0)
    def _():
        o_ref[...] = jnp.zeros(o_ref.shape, o_ref.dtype)

    @pl.when(mode == 2)
    def _():
        w2b_ref[...] = w2_ref[...].astype(BF16)

    @pl.when(mode >= 1)
    def _():
        x = x_ref[...]
        hg = jnp.dot(x, w1g_ref[...], preferred_element_type=F32) + b1g_ref[...]
        hl = jnp.dot(x, w1l_ref[...], preferred_element_type=F32) + b1l_ref[...]
        hg = jnp.minimum(hg, SWIGLU_LIMIT)
        hl = jnp.clip(hl, -SWIGLU_LIMIT, SWIGLU_LIMIT)
        act = hg * jax.nn.sigmoid(SWIGLU_ALPHA * hg) * (hl + 1.0)
        y = jnp.dot(act.astype(BF16), w2b_ref[...], preferred_element_type=F32)
        first = (sh_ref[s] == 0).astype(F32)
        o_ref[...] = (y + first * b2_ref[...]).astype(o_ref.dtype)


def _experts(sched, xs, w1g, w1l, b1g, b1l, w2, b2, *, bm, n_half):
    se, sh, sr, sm = sched
    NR, D = xs.shape
    E, _, F = w1g.shape
    fh = F // n_half
    S = se.shape[0]
    in_specs = [pl.BlockSpec((bm, D), lambda s, se, sh, sr, sm: (sr[s], 0)),
                pl.BlockSpec((None, D, fh), lambda s, se, sh, sr, sm: (se[s], 0, sh[s])),
                pl.BlockSpec((None, D, fh), lambda s, se, sh, sr, sm: (se[s], 0, sh[s])),
                pl.BlockSpec((None, 1, fh), lambda s, se, sh, sr, sm: (se[s], 0, sh[s])),
                pl.BlockSpec((None, 1, fh), lambda s, se, sh, sr, sm: (se[s], 0, sh[s])),
                pl.BlockSpec((None, fh, D), lambda s, se, sh, sr, sm: (se[s], sh[s], 0)),
                pl.BlockSpec((None, 1, D), lambda s, se, sh, sr, sm: (se[s], 0, 0))]
    grid_spec = pltpu.PrefetchScalarGridSpec(
        num_scalar_prefetch=4, grid=(S,), in_specs=in_specs,
        out_specs=pl.BlockSpec((bm, D), lambda s, se, sh, sr, sm: (sr[s], sh[s])),
        scratch_shapes=[pltpu.VMEM((fh, D), BF16)])
    return pl.pallas_call(
        _expert_kernel, grid_spec=grid_spec,
        out_shape=jax.ShapeDtypeStruct((NR, n_half * D), BF16),
        compiler_params=_cparams("arbitrary"),
        name="moe_experts",
    )(se, sh, sr, sm, xs, w1g, w1l, b1g, b1l, w2, b2)


def _combine_kernel(h_ref, yg_ref, tg_ref, nw_ref, o_ref, *, n_half):
    acc = h_ref[...]
    tg = tg_ref[...]
    D = acc.shape[1]
    for j in range(TOP_K):
        rows = yg_ref[j, :, :D].astype(F32)
        for hh in range(1, n_half):
            rows = rows + yg_ref[j, :, hh * D:(hh + 1) * D].astype(F32)
        acc = acc + rows * tg[:, j:j + 1]
    o_ref[...] = _rms(acc, nw_ref[...])


def _combine(h, yg, tg, nw, *, tm, n_half):
    M, D = h.shape
    tm = _pick_tile(M, tm, 8)
    return pl.pallas_call(
        functools.partial(_combine_kernel, n_half=n_half),
        grid=(M // tm,),
        in_specs=[pl.BlockSpec((tm, D), lambda i: (i, 0)),
                  pl.BlockSpec((TOP_K, tm, n_half * D), lambda i: (0, i, 0)),
                  pl.BlockSpec((tm, LANES), lambda i: (i, 0)),
                  pl.BlockSpec((1, D), lambda i: (0, 0))],
        out_specs=pl.BlockSpec((tm, D), lambda i: (i, 0)),
        out_shape=jax.ShapeDtypeStruct((M, D), F32),
        compiler_params=_cparams("parallel"),
        name="moe_combine",
    )(h, yg, tg, nw)


def _rope_tables(positions):
    inv = ROPE_THETA ** (-jnp.arange(0, ROPE_DIM, 2, dtype=F32) / ROPE_DIM)
    ang = positions.astype(F32)[:, None] * inv[None, :]
    cos, sin = jnp.cos(ang), jnp.sin(ang)
    cos2 = jnp.concatenate([cos, cos], axis=-1)
    sin2 = jnp.concatenate([sin, sin], axis=-1)
    n = positions.shape[0]
    ones = jnp.ones((n, NOPE_DIM), F32)
    z64 = jnp.zeros((n, QK_PAD - NOPE_DIM - ROPE_DIM), F32)
    z128 = jnp.zeros((n, NOPE_DIM), F32)
    qtab = jnp.concatenate([ones, cos2, z64, z128, sin2, z64], axis=-1)
    ktab = jnp.concatenate([cos2, sin2], axis=-1)
    return qtab, ktab


def _rot_cols(w):
    half = ROPE_DIM // 2
    return jnp.concatenate([-w[..., half:], w[..., :half]], axis=-1)


def _moe_schedule(top_idx, n_experts, bm, n_half, n_blocks):
    M = top_idx.shape[0]
    flat_e = top_idx.reshape(-1)
    onehot = (flat_e[:, None] == jnp.arange(n_experts, dtype=jnp.int32)[None, :]).astype(jnp.int32)
    rank = jnp.take_along_axis(jnp.cumsum(onehot, axis=0) - onehot, flat_e[:, None], axis=1)[:, 0]
    counts = jnp.sum(onehot, axis=0)
    nb = (counts + bm - 1) // bm
    blk_end = jnp.cumsum(nb)
    blk_start = blk_end - nb
    dest = blk_start[flat_e] * bm + rank
    flat_tok = jnp.repeat(jnp.arange(M, dtype=jnp.int32), TOP_K)
    row_tok = jnp.full((n_blocks * bm,), M, jnp.int32).at[dest].set(flat_tok)
    n_real = blk_end[-1]
    steps_end = jnp.cumsum(nb * n_half)
    s = jnp.arange(n_blocks * n_half, dtype=jnp.int32)
    real = s < n_real * n_half
    e_of = jnp.minimum(jnp.sum(steps_end[None, :] <= s[:, None], axis=1), n_experts - 1).astype(jnp.int32)
    local = s - (steps_end - nb * n_half)[e_of]
    nb_e = jnp.maximum(nb[e_of], 1)
    h_real = local // nb_e
    r_real = blk_start[e_of] + local % nb_e
    last_e = e_of[jnp.maximum(n_real * n_half - 1, 0)]
    tail = s - n_real * n_half
    n_tail = jnp.maximum(n_blocks - n_real, 1)
    se = jnp.where(real, e_of, last_e).astype(jnp.int32)
    sh = jnp.where(real, h_real, n_half - 1 - tail // n_tail).astype(jnp.int32)
    sr = jnp.where(real, r_real, n_real + tail % n_tail).astype(jnp.int32)
    sm = jnp.where(real, jnp.where(local % nb_e == 0, 2, 1), 0).astype(jnp.int32)
    return dest, row_tok, (se, sh, sr, sm)


def kernel(x_prompt, x_sample, cache_ckv, cache_krope, state_wkv, state_shift, page_table, norm_mix_w, w_in, rwkv_mu, rwkv_w0, rwkv_w2, rwkv_a0, rwkv_a2, rwkv_g2, rwkv_k_k, rwkv_k_a, rwkv_r_k, rwkv_ln_w, rwkv_ln_b, mla_q_norm_w, mla_w_uq, mla_kv_norm_w, mla_w_uk, mla_w_uv, w_out, norm_ffn_w, router_w, router_b, expert_w1, expert_b1, expert_w2, expert_b2, norm_final_w):
    B, T, D = x_prompt.shape
    Bd, Td, _ = x_sample.shape
    H = D // RWKV_HEAD_DIM
    HD = RWKV_HEAD_DIM
    dlora, alora, glora = rwkv_w2.shape[0], rwkv_a2.shape[0], rwkv_g2.shape[0]
    qlora, Hm, _ = mla_w_uq.shape
    kvlora = mla_w_uk.shape[0]
    n_experts = router_w.shape[1]
    F = expert_w2.shape[1]
    page = cache_ckv.shape[1]
    past = page_table.shape[1] * page
    scale = float((NOPE_DIM + ROPE_DIM) ** -0.5)
    row2 = lambda a: a.reshape(1, -1).astype(F32)

    sizes = [D, D, D, dlora, alora, glora, qlora, kvlora, ROPE_DIM, D, D]
    offs = np.concatenate([[0], np.cumsum(sizes)]).tolist()
    col = lambda i: w_in[:, offs[i]:offs[i + 1]]
    lw_pad, la_pad = _round_up(dlora, LANES), _round_up(alora, LANES)
    padc = lambda a, n: jnp.pad(a, ((0, 0), (0, n - a.shape[1])))
    perm = np.arange(D).reshape(H, HD).T.reshape(-1)
    pc = lambda a: a[..., perm]
    w5 = jnp.concatenate([pc(col(0)), pc(col(1)), pc(col(2)), pc(col(9)), col(10)], axis=1).astype(BF16)
    lora_w = lw_pad + la_pad + glora
    ws = jnp.concatenate([padc(col(3), lw_pad), padc(col(4), la_pad), col(5), col(6), col(7),
                          col(8), _rot_cols(col(8))], axis=1)
    col_cq, col_ckv, col_kr = lora_w, lora_w + qlora, lora_w + qlora + kvlora
    ws = padc(ws, _round_up(ws.shape[1], LANES)).astype(BF16)
    mu = rwkv_mu
    mur, muk, muv = row2(pc(mu[:D])), row2(pc(mu[D:2 * D])), row2(pc(mu[2 * D:3 * D]))
    m0 = 3 * D
    mul = jnp.concatenate([jnp.pad(mu[m0:m0 + dlora], (0, lw_pad - dlora)),
                           jnp.pad(mu[m0 + dlora:m0 + dlora + alora], (0, la_pad - alora)),
                           mu[m0 + dlora + alora:]]).reshape(1, -1)
    w2p = jnp.pad(pc(rwkv_w2), ((0, lw_pad - dlora), (0, 0))).astype(BF16)
    a2p = jnp.pad(pc(rwkv_a2), ((0, la_pad - alora), (0, 0))).astype(BF16)
    head_of = jnp.arange(D, dtype=jnp.int32) % H
    e_sel = (head_of[:, None] == jnp.arange(LANES, dtype=jnp.int32)[None, :]).astype(BF16)
    et_sel = e_sel.T
    prep_consts = (mur, muk, muv, mul, row2(pc(rwkv_w0)), w2p, row2(pc(rwkv_a0)), a2p,
                   pc(rwkv_g2).astype(BF16), row2(pc(rwkv_k_k)), row2(pc(rwkv_k_a)),
                   row2(pc(rwkv_r_k.reshape(-1))), e_sel, et_sel, lw_pad, la_pad)

    zq = jnp.zeros((qlora, Hm, QK_PAD - NOPE_DIM - ROPE_DIM), F32)
    q_nope_w, q_rope_w = mla_w_uq[..., :NOPE_DIM], mla_w_uq[..., NOPE_DIM:]
    wqa = jnp.concatenate([q_nope_w, q_rope_w, zq], axis=-1).reshape(qlora, Hm * QK_PAD).astype(BF16)
    wqb = jnp.concatenate([jnp.zeros_like(q_nope_w), _rot_cols(q_rope_w), zq],
                          axis=-1).reshape(qlora, Hm * QK_PAD).astype(BF16)
    zk = jnp.zeros((kvlora, Hm, QK_PAD - NOPE_DIM), F32)
    wk_top = jnp.concatenate([mla_w_uk, zk], axis=-1).reshape(kvlora, Hm * QK_PAD)
    sel = jnp.concatenate([jnp.zeros((ROPE_DIM, NOPE_DIM), F32), jnp.eye(ROPE_DIM, dtype=F32),
                           jnp.zeros((ROPE_DIM, QK_PAD - NOPE_DIM - ROPE_DIM), F32)], axis=-1)
    wk = jnp.concatenate([wk_top, jnp.tile(sel, (1, Hm))], axis=0).astype(BF16)
    wv = mla_w_uv.reshape(kvlora, Hm * V_DIM).astype(BF16)
    w_uk_t = jnp.transpose(mla_w_uk, (1, 2, 0)).astype(BF16)
    w_uv_h = jnp.transpose(mla_w_uv, (1, 0, 2)).astype(BF16)
    wo_mla = w_out.astype(BF16)
    wo_rwkv = w_out[perm, :].astype(BF16)
    rw = jnp.pad(router_w, ((0, 0), (0, LANES - n_experts)))
    rb = jnp.pad(router_b, (0, LANES - n_experts)).reshape(1, LANES)
    w1g, w1l = _w1_prep(expert_w1, tr=512, tc=2048)
    b1g = expert_b1[:, None, 0::2]
    b1l = expert_b1[:, None, 1::2]
    w2e = expert_w2
    b2e = expert_b2[:, None, :]
    nmw = row2(norm_mix_w)

    def project_stage(x2d, positions, tm_mla):
        p5 = _project(x2d, nmw, w5, normalize=True, tm=1024, tn=512)
        ps = _project(x2d, nmw, ws, normalize=True, tm=1024, tn=ws.shape[1])
        qtab, ktab = _rope_tables(positions)
        mla = _mla_prep(ps, qtab, ktab, row2(mla_q_norm_w), row2(mla_kv_norm_w), wqa, wqb, wk, wv,
                        col_cq=col_cq, col_ckv=col_ckv, col_kr=col_kr, qlora=qlora, kvlora=kvlora,
                        n_heads=Hm, hg=min(4, Hm), tm=tm_mla, scale=scale)
        return p5, ps, mla

    merge_consts = (row2(pc(rwkv_ln_w)), row2(pc(rwkv_ln_b)), e_sel, et_sel, wo_rwkv, wo_mla,
                    row2(norm_ffn_w), rw, rb)

    xp = x_prompt.reshape(B * T, D)
    p5_p, ps_p, mla_p = project_stage(xp, jnp.arange(T), min(512, T))
    vsplit = max(1, LANES // (B * H))
    assert HD % vsplit == 0 and B * H * vsplit <= LANES
    vr = HD // vsplit
    tb_p = min(64, T)
    zr, zw, zk, zv, za, zb, g_p, bo_p = _rwkv_prep_packed(
        p5_p.reshape(B, T, -1), ps_p.reshape(B, T, -1), prep_consts, B=B, T=T, tt=tb_p, D=D,
        lora_w=lora_w, n_heads=H, vsplit=vsplit)
    s0_p = jnp.zeros((HD, vr, LANES), F32)
    y_pk, s_kl = _rwkv_scan(zr, zw, zk, zv, za, zb, s0_p, tb=tb_p, packed=True)
    nl_p = B * vsplit * H
    wkv_p = s_kl[:, :, :nl_p].reshape(HD, vr, B, vsplit, H).transpose(2, 4, 3, 1, 0).reshape(B, H, HD, HD)

    q_p, kf_p, vf_p, ckv_p, kr_p = mla_p
    o_mla_p = _flash_attention(q_p, kf_p, vf_p, B=B, T=T, n_heads=Hm, tq=512, tk=512)
    h_p, hn_p, ti_p, tg_p = _merge(
        x_prompt, y_pk.reshape(T * vr, LANES), bo_p, g_p, p5_p.reshape(B, T, -1),
        o_mla_p.reshape(B, T, -1), merge_consts, tm=64, n_experts=n_experts, n_heads=H, vsplit=vsplit)

    xs_tm = jnp.swapaxes(x_sample, 0, 1).reshape(Td * Bd, D)
    pos_s = jnp.repeat(past + jnp.arange(Td), Bd)
    p5_s, ps_s, mla_s = project_stage(xs_tm, pos_s, Td * Bd)
    s5 = _project(state_shift, nmw, w5, normalize=False, tm=1024, tn=512).reshape(1, Bd, -1)
    ss = _project(state_shift, nmw, ws, normalize=False, tm=1024, tn=ws.shape[1]).reshape(1, Bd, -1)
    r_s, lw_s, k_s, v_s, al_s, be_s, g_s, bo_s = _rwkv_prep(
        p5_s, ps_s, s5, ss, prep_consts, n_groups=1, n_tiles=Td, tt=Bd, shift=Bd, D=D, lora_w=lora_w)

    def to_kl_s(a):
        return a.reshape(Td, Bd, HD, H).transpose(0, 2, 1, 3).reshape(Td, HD, Bd * H)

    nl_s = Bd * H
    pad_s = _round_up(nl_s, LANES) - nl_s
    pads = lambda a: jnp.pad(a, ((0, 0), (0, 0), (0, pad_s))) if pad_s else a
    s0_s = pads(state_wkv.transpose(3, 2, 0, 1).reshape(HD, HD, nl_s))
    y_sl, s_sl = _rwkv_scan(pads(to_kl_s(r_s)), pads(to_kl_s(lw_s)), pads(to_kl_s(k_s)),
                            pads(to_kl_s(v_s)), pads(to_kl_s(al_s)), pads(to_kl_s(be_s)), s0_s, tb=Td)
    y_s = y_sl[:, :, :nl_s].reshape(Td, HD, Bd, H).transpose(0, 2, 1, 3).reshape(Td * Bd, D)
    wkv_s = s_sl[:, :, :nl_s].reshape(HD, HD, Bd, H).transpose(2, 3, 1, 0)

    q_s, _, _, ckv_s, kr_s = mla_s
    q4 = q_s.reshape(Td, Bd, Hm, QK_PAD)
    qn_h = q4[..., :NOPE_DIM].transpose(2, 1, 0, 3).reshape(Hm, Bd * Td, NOPE_DIM)
    q_lat = _head_matmul(qn_h, w_uk_t)
    q_lat = q_lat.reshape(Hm, Bd, Td, kvlora).transpose(1, 0, 2, 3).reshape(Bd, Hm * Td, kvlora)
    q_rp = q4[..., NOPE_DIM:NOPE_DIM + ROPE_DIM].transpose(1, 2, 0, 3).reshape(Bd, Hm * Td, ROPE_DIM)
    qs = jnp.concatenate([q_lat.astype(BF16), q_rp], axis=-1)
    ckv_s_bt = ckv_s.reshape(Td, Bd, kvlora).transpose(1, 0, 2)
    kr_s_bt = kr_s.reshape(Td, Bd, ROPE_DIM).transpose(1, 0, 2)
    npad = _round_up(Td, 8)
    padn = lambda a: jnp.pad(a, ((0, 0), (0, npad - Td), (0, 0)))
    o_lat = _paged_attention(page_table, qs, padn(ckv_s_bt), padn(kr_s_bt), cache_ckv, cache_krope,
                             pg=16, n_new=Td)
    o_lat_h = o_lat.reshape(Bd, Hm, Td, kvlora).transpose(1, 2, 0, 3).reshape(Hm, Td * Bd, kvlora)
    o_mla_s = _head_matmul(o_lat_h, w_uv_h)
    o_mla_s = o_mla_s.transpose(1, 0, 2).reshape(Td * Bd, Hm * V_DIM)

    g1 = lambda a: a[None]
    h_s, hn_s, ti_s, tg_s = _merge(g1(xs_tm), g1(y_s), g1(bo_s), g1(g_s), g1(p5_s), g1(o_mla_s),
                                   merge_consts, tm=128, n_experts=n_experts, n_heads=H, vsplit=0)
    flat = lambda a: a.reshape(-1, a.shape[-1])
    h_all = jnp.concatenate([flat(h_p), flat(h_s)], axis=0)
    hn_all = jnp.concatenate([flat(hn_p), flat(hn_s)], axis=0)
    ti_all = jnp.concatenate([flat(ti_p), flat(ti_s)], axis=0)[:, :TOP_K]
    tg_all = jnp.concatenate([flat(tg_p), flat(tg_s)], axis=0)
    M = h_all.shape[0]
    bm = 256
    n_half = 2 if F % (2 * LANES) == 0 else 1
    n_blocks = (M * TOP_K + n_experts * (bm - 1)) // bm
    dest, row_tok, sched = _moe_schedule(ti_all, n_experts, bm, n_half, n_blocks)
    hn_pad = jnp.concatenate([hn_all, jnp.zeros((8, D), BF16)], axis=0)
    xs_rows = hn_pad[row_tok]
    y_halves = _experts(sched, xs_rows, w1g, w1l, b1g, b1l, w2e, b2e, bm=bm, n_half=n_half)
    dest_slot_major = dest.reshape(M, TOP_K).T.reshape(-1)
    yg = y_halves[dest_slot_major].reshape(TOP_K, M, n_half * D)
    y_all = _combine(h_all, yg, tg_all, row2(norm_final_w), tm=128, n_half=n_half)

    y_prompt = y_all[:B * T].reshape(B, T, D)
    y_sample = y_all[B * T:].reshape(Td, Bd, D).transpose(1, 0, 2)
    last = jnp.concatenate([x_prompt[:, -1], x_sample[:, -1]], axis=0)
    shift_out = _rmsnorm_rows(last, nmw)
    return (y_prompt, y_sample,
            ckv_p.reshape(B, T // page, page, kvlora), kr_p.reshape(B, T // page, page, ROPE_DIM),
            wkv_p, shift_out[:B],
            ckv_s_bt, kr_s_bt, wkv_s, shift_out[B:])
```

```python
import functools

import numpy as np
import jax
import jax.numpy as jnp
from jax import lax
from jax.experimental import pallas as pl
from jax.experimental.pallas import tpu as pltpu

F32 = jnp.float32
BF16 = jnp.bfloat16

NORM_EPS = 1e-6
GN_EPS = 64e-5
RWKV_HEAD_DIM = 64
NOPE_DIM = 128
ROPE_DIM = 64
V_DIM = 128
QK_PAD = 256
ROPE_THETA = 10000.0
NEG_INF = -1e30
TOP_K = 4
SWIGLU_ALPHA = 1.702
SWIGLU_LIMIT = 7.0
LANES = 128
VMEM_LIMIT = 56 * 1024 * 1024


def _round_up(n, m):
    return (n + m - 1) // m * m


def _pick_tile(n, target, mult):
    if n <= target:
        return n
    t = target // mult * mult
    while n % t:
        t -= mult
    return t


def _cparams(*sem):
    return pltpu.CompilerParams(dimension_semantics=sem, vmem_limit_bytes=VMEM_LIMIT)


def _split3(x):
    hi = x.astype(BF16)
    r1 = x - hi.astype(F32)
    mid = r1.astype(BF16)
    lo = (r1 - mid.astype(F32)).astype(BF16)
    return hi, mid, lo


def _tree_sum(xs):
    xs = list(xs)
    while len(xs) > 1:
        xs = [xs[i] + xs[i + 1] for i in range(0, len(xs) - 1, 2)] + ([xs[-1]] if len(xs) % 2 else [])
    return xs[0]


def _rms(x, w):
    return x * lax.rsqrt(jnp.mean(x * x, axis=-1, keepdims=True) + NORM_EPS) * w


def _proj_kernel(x_ref, nw_ref, w_ref, o_ref, xn_ref, *, normalize):
    @pl.when(pl.program_id(1) == 0)
    def _():
        x = x_ref[...]
        if normalize:
            x = _rms(x, nw_ref[...])
        xn_ref[...] = x.astype(BF16)

    o_ref[...] = jnp.dot(xn_ref[...], w_ref[...], preferred_element_type=F32)


def _project(x, norm_w, w, *, normalize, tm, tn):
    M, D = x.shape
    N = w.shape[1]
    tm = _pick_tile(M, tm, 8)
    tn = _pick_tile(N, tn, LANES)
    return pl.pallas_call(
        functools.partial(_proj_kernel, normalize=normalize),
        grid=(M // tm, N // tn),
        in_specs=[pl.BlockSpec((tm, D), lambda i, j: (i, 0)),
                  pl.BlockSpec((1, D), lambda i, j: (0, 0)),
                  pl.BlockSpec((D, tn), lambda i, j: (0, j))],
        out_specs=pl.BlockSpec((tm, tn), lambda i, j: (i, j)),
        out_shape=jax.ShapeDtypeStruct((M, N), F32),
        scratch_shapes=[pltpu.VMEM((tm, D), BF16)],
        compiler_params=_cparams("parallel", "arbitrary"),
        name="norm_proj",
    )(x, norm_w, w)


def _rmsnorm_rows_kernel(x_ref, w_ref, o_ref):
    o_ref[...] = _rms(x_ref[...], w_ref[...])


def _rmsnorm_rows(x, w):
    return pl.pallas_call(
        _rmsnorm_rows_kernel,
        out_shape=jax.ShapeDtypeStruct(x.shape, F32),
        name="rmsnorm_rows",
    )(x, w)


def _head_sum(x, n_heads):
    d = x.shape[1]
    assert d % LANES == 0 and LANES % n_heads == 0
    s = _tree_sum([x[:, j * LANES:(j + 1) * LANES] for j in range(d // LANES)])
    shift = LANES // 2
    while shift >= n_heads:
        s = s + pltpu.roll(s, shift, axis=1)
        shift //= 2
    return jnp.concatenate([s] * (d // LANES), axis=-1)


def _rwkv_prep_math(r, k, v, lo, w0_ref, w2_ref, a0_ref, a2_ref, g2_ref, kk_ref, ka_ref, rk_ref,
                    *, lw_pad, la_pad, n_heads):
    hw = lo[:, :lw_pad]
    ha = lo[:, lw_pad:lw_pad + la_pad]
    hg = lo[:, lw_pad + la_pad:]
    mm = lambda a, b_ref: jnp.dot(a.astype(BF16), b_ref[...], preferred_element_type=F32)
    w = -jax.nn.softplus(-(w0_ref[...] + mm(jnp.tanh(hw), w2_ref))) - 0.5
    a = jax.nn.sigmoid(a0_ref[...] + mm(ha, a2_ref))
    g = mm(jax.nn.sigmoid(hg), g2_ref)
    kk = k * kk_ref[...]
    kk = kk * lax.rsqrt(jnp.maximum(_head_sum(kk * kk, n_heads), 1e-24))
    k2 = k * (1.0 + (a - 1.0) * ka_ref[...])
    bsum = _head_sum(r * k2 * rk_ref[...], n_heads)
    return r, jnp.exp(-jnp.exp(w)), k2, v, -kk, kk * a, g, bsum * v


def _rwkv_prep_packed_kernel(r_ref, k_ref, v_ref, l_ref,
                             mur_ref, muk_ref, muv_ref, mul_ref, w0_ref, w2_ref, a0_ref, a2_ref,
                             g2_ref, kk_ref, ka_ref, rk_ref,
                             zr_ref, zw_ref, zk_ref, zv_ref, za_ref, zb_ref, go_ref, bo_ref,
                             cr_ref, ck_ref, cv_ref, cl_ref, *, lw_pad, la_pad, n_heads, vsplit):
    t = pl.program_id(0)
    n_seq, tt, D = r_ref.shape
    nk = D // n_heads
    vr = nk // vsplit

    @pl.when(t == 0)
    def _():
        for c_ref in (cr_ref, ck_ref, cv_ref, cl_ref):
            c_ref[...] = jnp.zeros(c_ref.shape, F32)

    def lerp(p, c_ref, b, mu_ref):
        rolled = pltpu.roll(p, 1, axis=0)
        row = lax.broadcasted_iota(jnp.int32, p.shape, 0)
        prev = jnp.where(row == 0, c_ref[b], rolled)
        c_ref[b] = p[tt - 1:, :]
        return p + (prev - p) * mu_ref[...]

    outs = []
    for b in range(n_seq):
        r = lerp(r_ref[b], cr_ref, b, mur_ref)
        k = lerp(k_ref[b], ck_ref, b, muk_ref)
        v = lerp(v_ref[b], cv_ref, b, muv_ref)
        lo = lerp(l_ref[b], cl_ref, b, mul_ref)
        res = _rwkv_prep_math(r, k, v, lo, w0_ref, w2_ref, a0_ref, a2_ref, g2_ref, kk_ref, ka_ref,
                              rk_ref, lw_pad=lw_pad, la_pad=la_pad, n_heads=n_heads)
        go_ref[b] = res[6]
        bo_ref[b] = res[7]
        outs.append(res)

    used = n_seq * vsplit * n_heads
    zpad = [jnp.zeros((tt, LANES - used), F32)] if used < LANES else []
    piece = lambda x, j: x[:, j * n_heads:(j + 1) * n_heads]
    for z_ref, idx in ((zr_ref, 0), (zw_ref, 1), (zk_ref, 2), (za_ref, 4), (zb_ref, 5)):
        for kq in range(nk):
            parts = [piece(outs[b][idx], kq) for b in range(n_seq) for _ in range(vsplit)]
            z_ref[kq * tt:(kq + 1) * tt, :] = jnp.concatenate(parts + zpad, axis=-1)
    for vq in range(vr):
        parts = [piece(outs[b][3], vs * vr + vq) for b in range(n_seq) for vs in range(vsplit)]
        zv_ref[vq * tt:(vq + 1) * tt, :] = jnp.concatenate(parts + zpad, axis=-1)


def _rwkv_prep_packed(p5, ps, consts, *, B, T, tt, D, lora_w, n_heads, vsplit):
    (mur, muk, muv, mul, w0, w2p, a0, a2p, g2, k_k, k_a, r_k, lw_pad, la_pad) = consts
    nk = D // n_heads
    vr = nk // vsplit
    n_tiles = T // tt
    big = lambda c: pl.BlockSpec((B, tt, D), lambda t, c=c: (0, t, c))
    full = lambda a: pl.BlockSpec(a.shape, lambda t: (0,) * a.ndim)
    in_specs = [big(0), big(1), big(2), pl.BlockSpec((B, tt, lora_w), lambda t: (0, t, 0)),
                full(mur), full(muk), full(muv), full(mul), full(w0), full(w2p), full(a0),
                full(a2p), full(g2), full(k_k), full(k_a), full(r_k)]
    zspec = pl.BlockSpec((nk * tt, LANES), lambda t: (t, 0))
    vspec = pl.BlockSpec((vr * tt, LANES), lambda t: (t, 0))
    tok = pl.BlockSpec((B, tt, D), lambda t: (0, t, 0))
    zshape = jax.ShapeDtypeStruct((n_tiles * nk * tt, LANES), F32)
    vshape = jax.ShapeDtypeStruct((n_tiles * vr * tt, LANES), F32)
    tshape = jax.ShapeDtypeStruct((B, T, D), F32)
    return pl.pallas_call(
        functools.partial(_rwkv_prep_packed_kernel, lw_pad=lw_pad, la_pad=la_pad, n_heads=n_heads,
                          vsplit=vsplit),
        grid=(n_tiles,),
        in_specs=in_specs,
        out_specs=[zspec, zspec, zspec, vspec, zspec, zspec, tok, tok],
        out_shape=[zshape, zshape, zshape, vshape, zshape, zshape, tshape, tshape],
        scratch_shapes=[pltpu.VMEM((B, 1, D), F32)] * 3 + [pltpu.VMEM((B, 1, lora_w), F32)],
        compiler_params=_cparams("arbitrary"),
        name="rwkv_prep_packed",
    )(p5, p5, p5, ps, mur, muk, muv, mul, w0, w2p, a0, a2p, g2, k_k, k_a, r_k)


def _rwkv_prep_kernel(r_ref, k_ref, v_ref, l_ref, sr_ref, sk_ref, sv_ref, sl_ref,
                      mur_ref, muk_ref, muv_ref, mul_ref, w0_ref, w2_ref, a0_ref, a2_ref, g2_ref,
                      kk_ref, ka_ref, rk_ref,
                      ro_ref, lw_ref, ko_ref, vo_ref, al_ref, be_ref, go_ref, bo_ref,
                      cr_ref, ck_ref, cv_ref, cl_ref, *, shift, lw_pad, la_pad, n_heads):
    t = pl.program_id(1)

    @pl.when(t == 0)
    def _():
        cr_ref[...] = sr_ref[...]
        ck_ref[...] = sk_ref[...]
        cv_ref[...] = sv_ref[...]
        cl_ref[...] = sl_ref[...]

    def lerp(p_ref, c_ref, mu_ref):
        p = p_ref[...]
        if shift == 1:
            rolled = pltpu.roll(p, 1, axis=0)
            row = lax.broadcasted_iota(jnp.int32, p.shape, 0)
            prev = jnp.where(row == 0, c_ref[...], rolled)
            c_new = p[p.shape[0] - 1:, :]
        else:
            prev = c_ref[...]
            c_new = p
        xx = p + (prev - p) * mu_ref[...]
        c_ref[...] = c_new
        return xx

    r = lerp(r_ref, cr_ref, mur_ref)
    k = lerp(k_ref, ck_ref, muk_ref)
    v = lerp(v_ref, cv_ref, muv_ref)
    lo = lerp(l_ref, cl_ref, mul_ref)
    r, lw, k2, v, al, be, g, bonus = _rwkv_prep_math(
        r, k, v, lo, w0_ref, w2_ref, a0_ref, a2_ref, g2_ref, kk_ref, ka_ref, rk_ref,
        lw_pad=lw_pad, la_pad=la_pad, n_heads=n_heads)
    ro_ref[...] = r
    lw_ref[...] = lw
    ko_ref[...] = k2
    vo_ref[...] = v
    al_ref[...] = al
    be_ref[...] = be
    go_ref[...] = g
    bo_ref[...] = bonus


def _rwkv_prep(p5, ps, shift5, shifts, consts, *, n_groups, n_tiles, tt, shift, D, lora_w, n_heads):
    M = p5.shape[0]
    (mur, muk, muv, mul, w0, w2p, a0, a2p, g2, k_k, k_a, r_k, lw_pad, la_pad) = consts
    row = lambda g, t: (g * n_tiles + t)
    big = lambda c: pl.BlockSpec((tt, D), lambda g, t, c=c: (row(g, t), c))
    sh = lambda c: pl.BlockSpec((None, shift, D), lambda g, t, c=c: (g, 0, c))
    full = lambda a: pl.BlockSpec(a.shape, lambda g, t: (0,) * a.ndim)
    in_specs = [big(0), big(1), big(2),
                pl.BlockSpec((tt, lora_w), lambda g, t: (row(g, t), 0)),
                sh(0), sh(1), sh(2),
                pl.BlockSpec((None, shift, lora_w), lambda g, t: (g, 0, 0)),
                full(mur), full(muk), full(muv), full(mul), full(w0), full(w2p), full(a0),
                full(a2p), full(g2), full(k_k), full(k_a), full(r_k)]
    out_spec = pl.BlockSpec((tt, D), lambda g, t: (row(g, t), 0))
    outs = pl.pallas_call(
        functools.partial(_rwkv_prep_kernel, shift=shift, lw_pad=lw_pad, la_pad=la_pad,
                          n_heads=n_heads),
        grid=(n_groups, n_tiles),
        in_specs=in_specs,
        out_specs=[out_spec] * 8,
        out_shape=[jax.ShapeDtypeStruct((M, D), F32)] * 8,
        scratch_shapes=[pltpu.VMEM((shift, D), F32)] * 3 + [pltpu.VMEM((shift, lora_w), F32)],
        compiler_params=_cparams("parallel", "arbitrary"),
        name="rwkv_prep",
    )(p5, p5, p5, ps, shift5, shift5, shift5, shifts,
      mur, muk, muv, mul, w0, w2p, a0, a2p, g2, k_k, k_a, r_k)
    return outs


def _rwkv_scan_kernel(r_ref, w_ref, k_ref, v_ref, a_ref, b_ref, s0_ref, y_ref, sf_ref, s_ref,
                      *, tb, vr, nk, packed):
    tblk = pl.program_id(1)

    @pl.when(tblk == 0)
    def _():
        s_ref[...] = s0_ref[...]

    n_acc = 8
    if packed:
        row = lambda ref, t, k: jnp.broadcast_to(ref[pl.ds(k * tb + t, 1), :], (vr, LANES))
        tile = lambda ref, t, n: ref[pl.ds(t, n, stride=tb), :]
    else:
        row = lambda ref, t, k: jnp.broadcast_to(ref[t, pl.ds(k, 1), :], (vr, LANES))
        tile = lambda ref, t, n: ref[t]

    def accumulate(parts, k, term):
        parts[k % n_acc] = term if parts[k % n_acc] is None else parts[k % n_acc] + term

    sa_parts = [None] * n_acc
    for k in range(nk):
        accumulate(sa_parts, k, s_ref[k] * row(a_ref, 0, k))
    sa0 = _tree_sum(sa_parts)

    def step(t, sa):
        vt = tile(v_ref, t, vr)
        tn = jnp.minimum(t + 1, tb - 1)
        y_parts = [None] * n_acc
        sa_parts = [None] * n_acc
        for k in range(nk):
            s = s_ref[k] * row(w_ref, t, k) + sa * row(b_ref, t, k) + vt * row(k_ref, t, k)
            s_ref[k] = s
            accumulate(y_parts, k, s * row(r_ref, t, k))
            accumulate(sa_parts, k, s * row(a_ref, tn, k))
        y_ref[t] = _tree_sum(y_parts)
        return _tree_sum(sa_parts)

    lax.fori_loop(0, tb, step, sa0)

    @pl.when(tblk == pl.num_programs(1) - 1)
    def _():
        sf_ref[...] = s_ref[...]


def _rwkv_scan(r, w, k, v, a, b, s0, *, tb, packed=False):
    K, VR, NL = s0.shape
    if packed:
        T = r.shape[0] // K
        assert NL == LANES and T % tb == 0
        kspec = pl.BlockSpec((K * tb, LANES), lambda n, t: (t, 0))
        vin = pl.BlockSpec((VR * tb, LANES), lambda n, t: (t, 0))
    else:
        T = r.shape[0]
        tb = min(tb, T)
        assert T % tb == 0 and NL % LANES == 0
        kspec = pl.BlockSpec((tb, K, LANES), lambda n, t: (t, 0, n))
        vin = pl.BlockSpec((tb, VR, LANES), lambda n, t: (t, 0, n))
    vspec = pl.BlockSpec((tb, VR, LANES), lambda n, t: (t, 0, n))
    sspec = pl.BlockSpec((K, VR, LANES), lambda n, t: (0, 0, n))
    return pl.pallas_call(
        functools.partial(_rwkv_scan_kernel, tb=tb, vr=VR, nk=K, packed=packed),
        grid=(NL // LANES, T // tb),
        in_specs=[kspec, kspec, kspec, vin, kspec, kspec, sspec],
        out_specs=[vspec, sspec],
        out_shape=[jax.ShapeDtypeStruct((T, VR, NL), F32),
                   jax.ShapeDtypeStruct((K, VR, NL), F32)],
        scratch_shapes=[pltpu.VMEM((K, VR, LANES), F32)],
        compiler_params=_cparams("parallel", "arbitrary"),
        name="rwkv_scan",
    )(r, w, k, v, a, b, s0)


def _mla_prep_kernel(cq_ref, ckv_ref, kr_ref, qtab_ref, ktab_ref, qn_ref, kvn_ref,
                     wqa_ref, wqb_ref, wk_ref, wv_ref,
                     q_ref, k_ref, v_ref, ckv_o_ref, kr_o_ref, cqn_ref, ckk_ref, *, hg, scale):
    @pl.when(pl.program_id(1) == 0)
    def _():
        cqn_ref[...] = _rms(cq_ref[...], qn_ref[...]).astype(BF16)
        ckv = _rms(ckv_ref[...], kvn_ref[...])
        ckv_o_ref[...] = ckv
        kr = kr_ref[...]
        ktab = ktab_ref[...]
        krope = kr[:, :ROPE_DIM] * ktab[:, :ROPE_DIM] + kr[:, ROPE_DIM:] * ktab[:, ROPE_DIM:]
        kr_o_ref[...] = krope
        ckk_ref[...] = jnp.concatenate([ckv, krope], axis=-1).astype(BF16)

    cqn = cqn_ref[...]
    qtab = qtab_ref[...]
    cosf = jnp.concatenate([qtab[:, :QK_PAD]] * hg, axis=-1)
    sinf = jnp.concatenate([qtab[:, QK_PAD:]] * hg, axis=-1)
    qa = jnp.dot(cqn, wqa_ref[...], preferred_element_type=F32)
    qb = jnp.dot(cqn, wqb_ref[...], preferred_element_type=F32)
    q_ref[...] = ((qa * cosf + qb * sinf) * scale).astype(BF16)
    ckk = ckk_ref[...]
    k_ref[...] = jnp.dot(ckk, wk_ref[...], preferred_element_type=F32).astype(BF16)
    v_ref[...] = jnp.dot(ckk[:, :ckk.shape[1] - ROPE_DIM], wv_ref[...],
                         preferred_element_type=F32).astype(BF16)


def _mla_prep(ps, qtab, ktab, qn, kvn, wqa, wqb, wk, wv, *, col_cq, col_ckv, col_kr, qlora, kvlora,
              n_heads, hg, tm, scale):
    M = ps.shape[0]
    tm = _pick_tile(M, tm, 8)
    n_tab = qtab.shape[0] // tm
    grid = (M // tm, n_heads // hg)
    tabspec = lambda a: pl.BlockSpec((tm, a.shape[1]), lambda i, j: (i % n_tab, 0))
    full = lambda a: pl.BlockSpec(a.shape, lambda i, j: (0,) * a.ndim)
    in_specs = [pl.BlockSpec((tm, qlora), lambda i, j: (i, col_cq // qlora)),
                pl.BlockSpec((tm, kvlora), lambda i, j: (i, col_ckv // kvlora)),
                pl.BlockSpec((tm, 2 * ROPE_DIM), lambda i, j: (i, col_kr // (2 * ROPE_DIM))),
                tabspec(qtab), tabspec(ktab), full(qn), full(kvn),
                pl.BlockSpec((qlora, hg * QK_PAD), lambda i, j: (0, j)),
                pl.BlockSpec((qlora, hg * QK_PAD), lambda i, j: (0, j)),
                pl.BlockSpec((kvlora + ROPE_DIM, hg * QK_PAD), lambda i, j: (0, j)),
                pl.BlockSpec((kvlora, hg * V_DIM), lambda i, j: (0, j))]
    out_specs = [pl.BlockSpec((tm, hg * QK_PAD), lambda i, j: (i, j)),
                 pl.BlockSpec((tm, hg * QK_PAD), lambda i, j: (i, j)),
                 pl.BlockSpec((tm, hg * V_DIM), lambda i, j: (i, j)),
                 pl.BlockSpec((tm, kvlora), lambda i, j: (i, 0)),
                 pl.BlockSpec((tm, ROPE_DIM), lambda i, j: (i, 0))]
    out_shape = [jax.ShapeDtypeStruct((M, n_heads * QK_PAD), BF16),
                 jax.ShapeDtypeStruct((M, n_heads * QK_PAD), BF16),
                 jax.ShapeDtypeStruct((M, n_heads * V_DIM), BF16),
                 jax.ShapeDtypeStruct((M, kvlora), F32),
                 jax.ShapeDtypeStruct((M, ROPE_DIM), F32)]
    return pl.pallas_call(
        functools.partial(_mla_prep_kernel, hg=hg, scale=scale),
        grid=grid, in_specs=in_specs, out_specs=out_specs, out_shape=out_shape,
        scratch_shapes=[pltpu.VMEM((tm, qlora), BF16), pltpu.VMEM((tm, kvlora + ROPE_DIM), BF16)],
        compiler_params=_cparams("parallel", "arbitrary"),
        name="mla_prep",
    )(ps, ps, ps, qtab, ktab, qn, kvn, wqa, wqb, wk, wv)


def _flash_kernel(q_ref, k_ref, v_ref, o_ref, *, tq, tk):
    i = pl.program_id(2)
    q = q_ref[...]
    n_sub = tq // tk

    def block(j, carry, masked):
        m, l, acc = carry
        start = pl.multiple_of(j * tk, tk)
        kb = k_ref[pl.ds(start, tk), :]
        vb = v_ref[pl.ds(start, tk), :]
        s = lax.dot_general(q, kb, (((1,), (1,)), ((), ())), preferred_element_type=F32)
        if masked:
            qpos = i * tq + lax.broadcasted_iota(jnp.int32, s.shape, 0)
            kpos = j * tk + lax.broadcasted_iota(jnp.int32, s.shape, 1)
            s = jnp.where(kpos <= qpos, s, NEG_INF)
        m_new = jnp.maximum(m, jnp.max(s, axis=-1, keepdims=True))
        p = jnp.exp(s - m_new)
        corr = jnp.exp(m - m_new)
        l = corr * l + jnp.sum(p, axis=-1, keepdims=True)
        acc = corr * acc + jnp.dot(p.astype(BF16), vb, preferred_element_type=F32)
        return m_new, l, acc

    init = (jnp.full((tq, 1), NEG_INF, F32), jnp.zeros((tq, 1), F32),
            jnp.zeros((tq, v_ref.shape[1]), F32))
    carry = lax.fori_loop(0, i * n_sub, lambda j, c: block(j, c, False), init)
    for d in range(n_sub):
        carry = block(i * n_sub + d, carry, True)
    m, l, acc = carry
    o_ref[...] = (acc / l).astype(o_ref.dtype)


def _flash_attention(q, k, v, *, B, T, n_heads, tq, tk):
    tq = min(tq, T)
    tk = min(tk, tq)
    nq = T // tq
    return pl.pallas_call(
        functools.partial(_flash_kernel, tq=tq, tk=tk),
        grid=(B, n_heads, nq),
        in_specs=[pl.BlockSpec((tq, QK_PAD), lambda b, h, i: (b * nq + i, h)),
                  pl.BlockSpec((T, QK_PAD), lambda b, h, i: (b, h)),
                  pl.BlockSpec((T, V_DIM), lambda b, h, i: (b, h))],
        out_specs=pl.BlockSpec((tq, V_DIM), lambda b, h, i: (b * nq + i, h)),
        out_shape=jax.ShapeDtypeStruct((B * T, n_heads * V_DIM), F32),
        compiler_params=_cparams("parallel", "parallel", "arbitrary"),
        name="mla_flash",
    )(q, k, v)


def _bmm_kernel(x_ref, w_ref, o_ref):
    o_ref[...] = jnp.dot(x_ref[...].astype(BF16), w_ref[...], preferred_element_type=F32)


def _head_matmul(x, w):
    H, R, K = x.shape
    N = w.shape[2]
    return pl.pallas_call(
        _bmm_kernel,
        grid=(H,),
        in_specs=[pl.BlockSpec((None, R, K), lambda h: (h, 0, 0)),
                  pl.BlockSpec((None, K, N), lambda h: (h, 0, 0))],
        out_specs=pl.BlockSpec((None, R, N), lambda h: (h, 0, 0)),
        out_shape=jax.ShapeDtypeStruct((H, R, N), F32),
        compiler_params=_cparams("parallel"),
        name="head_matmul",
    )(x, w)


def _paged_kernel(pt_ref, q_ref, cn_ref, kn_ref, *refs, pg, n_new, kvlora):
    ckv_refs = refs[:pg]
    kr_refs = refs[pg:2 * pg]
    o_ref = refs[2 * pg]
    m_ref, l_ref, acc_ref = refs[2 * pg + 1:]
    g = pl.program_id(1)

    @pl.when(g == 0)
    def _():
        m_ref[...] = jnp.full(m_ref.shape, NEG_INF, F32)
        l_ref[...] = jnp.zeros(l_ref.shape, F32)
        acc_ref[...] = jnp.zeros(acc_ref.shape, F32)

    q = q_ref[...]
    ql = q[:, :kvlora]
    qr = q[:, kvlora:]
    dn = (((1,), (1,)), ((), ()))

    def scores(c_ref, k_ref, k_is_transposed):
        kb = k_ref[...].astype(BF16)
        if k_is_transposed:
            rope = jnp.dot(qr, kb, preferred_element_type=F32)
        else:
            rope = lax.dot_general(qr, kb, dn, preferred_element_type=F32)
        return lax.dot_general(ql, c_ref[...].astype(BF16), dn, preferred_element_type=F32) + rope

    def update(s, value_refs):
        m = m_ref[...]
        m_new = jnp.maximum(m, jnp.max(s, axis=-1, keepdims=True))
        p = jnp.exp(s - m_new).astype(BF16)
        corr = jnp.exp(m - m_new)
        l_ref[...] = corr * l_ref[...] + jnp.sum(p.astype(F32), axis=-1, keepdims=True)
        pv = None
        off = 0
        for c_ref in value_refs:
            n = c_ref.shape[0]
            part = jnp.dot(p[:, off:off + n], c_ref[...].astype(BF16), preferred_element_type=F32)
            pv = part if pv is None else pv + part
            off += n
        acc_ref[...] = corr * acc_ref[...] + pv
        m_ref[...] = m_new

    s_all = jnp.concatenate([scores(c, k, True) for c, k in zip(ckv_refs, kr_refs)], axis=-1)
    update(s_all, ckv_refs)

    @pl.when(g == pl.num_programs(1) - 1)
    def _():
        rows = q.shape[0]
        npad = cn_ref.shape[0]
        tq = lax.broadcasted_iota(jnp.int32, (rows, npad), 0) % n_new
        kj = lax.broadcasted_iota(jnp.int32, (rows, npad), 1)
        update(jnp.where(kj <= tq, scores(cn_ref, kn_ref, False), NEG_INF), [cn_ref])
        o_ref[...] = acc_ref[...] / l_ref[...]


def _paged_attention(page_table, qs, ckv_new, kr_new, cache_ckv, cache_krope, *, pg, n_new):
    Bd, R, QW = qs.shape
    kvlora = cache_ckv.shape[2]
    page = cache_ckv.shape[1]
    n_pages = page_table.shape[1]
    pg = min(pg, n_pages)
    assert n_pages % pg == 0
    npad = ckv_new.shape[1]
    cspec = lambda i: pl.BlockSpec((None, page, kvlora), lambda b, g, pt, i=i: (pt[b, g * pg + i], 0, 0))
    krope_t = jnp.swapaxes(cache_krope, 1, 2)
    kspec = lambda i: pl.BlockSpec((None, ROPE_DIM, page), lambda b, g, pt, i=i: (pt[b, g * pg + i], 0, 0))
    in_specs = ([pl.BlockSpec((None, R, QW), lambda b, g, pt: (b, 0, 0)),
                 pl.BlockSpec((None, npad, kvlora), lambda b, g, pt: (b, 0, 0)),
                 pl.BlockSpec((None, npad, ROPE_DIM), lambda b, g, pt: (b, 0, 0))]
                + [cspec(i) for i in range(pg)] + [kspec(i) for i in range(pg)])
    grid_spec = pltpu.PrefetchScalarGridSpec(
        num_scalar_prefetch=1, grid=(Bd, n_pages // pg), in_specs=in_specs,
        out_specs=pl.BlockSpec((None, R, kvlora), lambda b, g, pt: (b, 0, 0)),
        scratch_shapes=[pltpu.VMEM((R, 1), F32), pltpu.VMEM((R, 1), F32), pltpu.VMEM((R, kvlora), F32)])
    return pl.pallas_call(
        functools.partial(_paged_kernel, pg=pg, n_new=n_new, kvlora=kvlora),
        grid_spec=grid_spec,
        out_shape=jax.ShapeDtypeStruct((Bd, R, kvlora), F32),
        compiler_params=_cparams("parallel", "arbitrary"),
        name="mla_paged",
    )(page_table, qs, ckv_new, kr_new, *([cache_ckv] * pg), *([krope_t] * pg))


def _merge_rows(x, y, bo, g, ga, gb, om, lnw_ref, lnb_ref, wor_ref, wom_ref, nf_ref,
                rw_ref, rb_ref, *, n_experts, n_heads):
    inv_n = 1.0 / RWKV_HEAD_DIM
    mean = _head_sum(y, n_heads) * inv_n
    yc = y - mean
    var = _head_sum(yc * yc, n_heads) * inv_n
    yn = yc * lax.rsqrt(var + GN_EPS) * lnw_ref[...] + lnb_ref[...]
    o_rwkv = (yn + bo) * g
    h = (x + jnp.dot((jax.nn.sigmoid(ga) * o_rwkv).astype(BF16), wor_ref[...], preferred_element_type=F32)
         + jnp.dot((jax.nn.sigmoid(gb) * om).astype(BF16), wom_ref[...], preferred_element_type=F32))
    hn = _rms(h, nf_ref[...])

    rw = rw_ref[...]
    w_hi = rw.astype(BF16)
    w_r1 = rw - w_hi.astype(F32)
    w_mid = w_r1.astype(BF16)
    w_lo = (w_r1 - w_mid.astype(F32)).astype(BF16)
    x_hi, x_mid, x_lo = _split3(hn)
    d = lambda a, b: jnp.dot(a, b, preferred_element_type=F32)
    logits = (d(x_hi, w_hi) + (d(x_hi, w_mid) + d(x_mid, w_hi))
              + (d(x_hi, w_lo) + d(x_mid, w_mid) + d(x_lo, w_hi))) + rb_ref[...]
    lane = lax.broadcasted_iota(jnp.int32, logits.shape, 1)
    work = jnp.where(lane < n_experts, logits, -jnp.inf)
    vals, idxs = [], []
    for _ in range(TOP_K):
        mx = jnp.max(work, axis=-1, keepdims=True)
        ix = jnp.min(jnp.where(work == mx, lane, LANES), axis=-1, keepdims=True)
        vals.append(mx)
        idxs.append(ix)
        work = jnp.where(lane == ix, -jnp.inf, work)
    ex = [jnp.exp(vv - vals[0]) for vv in vals]
    den = ex[0] + ex[1] + ex[2] + ex[3]
    ti = jnp.zeros(logits.shape, jnp.int32)
    tg = jnp.zeros(logits.shape, F32)
    for j in range(TOP_K):
        ti = jnp.where(lane == j, idxs[j], ti)
        tg = jnp.where(lane == j, ex[j] / den, tg)
    return h, hn.astype(BF16), ti, tg


def _merge_kernel(x_ref, y_ref, bo_ref, g_ref, ga_ref, gb_ref, om_ref, *refs, n_experts, n_heads,
                  vsplit):
    consts, (h_ref, hn_ref, ti_ref, tg_ref) = refs[:-4], refs[-4:]
    n_seq, tm, D = x_ref.shape
    for b in range(n_seq):
        if vsplit:
            vr = D // n_heads // vsplit
            pieces = [None] * (vr * vsplit)
            for vq in range(vr):
                rows = y_ref[pl.ds(vq, tm, stride=vr), :]
                for vs in range(vsplit):
                    lo = (b * vsplit + vs) * n_heads
                    pieces[vs * vr + vq] = rows[:, lo:lo + n_heads]
            y = jnp.concatenate(pieces, axis=-1)
        else:
            y = y_ref[b]
        h, hn, ti, tg = _merge_rows(x_ref[b], y, bo_ref[b], g_ref[b], ga_ref[b], gb_ref[b], om_ref[b],
                                    *consts, n_experts=n_experts, n_heads=n_heads)
        h_ref[b] = h
        hn_ref[b] = hn
        ti_ref[b] = ti
        tg_ref[b] = tg


def _merge(x, y, bonus, g, p5, o_mla, consts, *, tm, n_experts, n_heads, vsplit):
    G, R, D = x.shape
    tm = _pick_tile(R, tm, 8)
    blk = lambda c=0: pl.BlockSpec((G, tm, D), lambda i, c=c: (0, i, c))
    full = lambda a: pl.BlockSpec(a.shape, lambda i: (0,) * a.ndim)
    if vsplit:
        vr = D // n_heads // vsplit
        yspec = pl.BlockSpec((tm * vr, LANES), lambda i: (i, 0))
    else:
        yspec = blk()
    in_specs = [blk(), yspec, blk(), blk(), blk(3), blk(4), blk()] + [full(a) for a in consts]
    lane_blk = pl.BlockSpec((G, tm, LANES), lambda i: (0, i, 0))
    out_specs = [blk(), blk(), lane_blk, lane_blk]
    out_shape = [jax.ShapeDtypeStruct((G, R, D), F32), jax.ShapeDtypeStruct((G, R, D), BF16),
                 jax.ShapeDtypeStruct((G, R, LANES), jnp.int32),
                 jax.ShapeDtypeStruct((G, R, LANES), F32)]
    return pl.pallas_call(
        functools.partial(_merge_kernel, n_experts=n_experts, n_heads=n_heads, vsplit=vsplit),
        grid=(R // tm,), in_specs=in_specs, out_specs=out_specs, out_shape=out_shape,
        compiler_params=_cparams("parallel"),
        name="merge_router",
    )(x, y, bonus, g, p5, p5, o_mla, *consts)


def _w1_prep_kernel(w_ref, p_ref, g_ref, l_ref):
    perm = p_ref[...]
    half = perm.shape[0] // 2
    for c in range(w_ref.shape[1] // perm.shape[0]):
        chunk = w_ref[:, c * 2 * half:(c + 1) * 2 * half].astype(BF16)
        both = jnp.dot(chunk, perm, preferred_element_type=F32)
        g_ref[:, c * half:(c + 1) * half] = both[:, :half].astype(BF16)
        l_ref[:, c * half:(c + 1) * half] = both[:, half:].astype(BF16)


def _w1_prep(w1, *, tr, tc):
    E, D, F2 = w1.shape
    pw = 2 * LANES
    tr = _pick_tile(D, tr, 8)
    tc = _pick_tile(F2, tc, pw)
    j = np.arange(LANES)
    perm = np.zeros((pw, pw), np.float32)
    perm[2 * j, j] = 1.0
    perm[2 * j + 1, LANES + j] = 1.0
    out = jax.ShapeDtypeStruct((E, D, F2 // 2), BF16)
    return pl.pallas_call(
        _w1_prep_kernel,
        grid=(E, D // tr, F2 // tc),
        in_specs=[pl.BlockSpec((None, tr, tc), lambda e, i, j: (e, i, j)),
                  pl.BlockSpec((pw, pw), lambda e, i, j: (0, 0))],
        out_specs=[pl.BlockSpec((None, tr, tc // 2), lambda e, i, j: (e, i, j))] * 2,
        out_shape=[out, out],
        compiler_params=_cparams("parallel", "parallel", "parallel"),
        name="moe_w1_prep",
    )(w1, jnp.asarray(perm, BF16))


def _expert_kernel(se_ref, sh_ref, sr_ref, sm_ref, x_ref, w1g_ref, w1l_ref, b1g_ref, b1l_ref,
                   w2_ref, b2_ref, o_ref, w2b_ref):
    s = pl.program_id(0)
    mode = sm_ref[s]

    @pl.when(mode == 0)
    def _():
        o_ref[...] = jnp.zeros(o_ref.shape, o_ref.dtype)

    @pl.when(mode == 2)
    def _():
        w2b_ref[...] = w2_ref[...].astype(BF16)

    @pl.when(mode >= 1)
    def _():
        x = x_ref[...]
        hg = jnp.dot(x, w1g_ref[...], preferred_element_type=F32) + b1g_ref[...]
        hl = jnp.dot(x, w1l_ref[...], preferred_element_type=F32) + b1l_ref[...]
        hg = jnp.minimum(hg, SWIGLU_LIMIT)
        hl = jnp.clip(hl, -SWIGLU_LIMIT, SWIGLU_LIMIT)
        act = hg * jax.nn.sigmoid(SWIGLU_ALPHA * hg) * (hl + 1.0)
        y = jnp.dot(act.astype(BF16), w2b_ref[...], preferred_element_type=F32)
        first = (sh_ref[s] == 0).astype(F32)
        o_ref[...] = (y + first * b2_ref[...]).astype(o_ref.dtype)


def _experts(sched, xs, w1g, w1l, b1g, b1l, w2, b2, *, bm, n_half):
    se, sh, sr, sm = sched
    NR, D = xs.shape
    E, _, F = w1g.shape
    fh = F // n_half
    S = se.shape[0]
    in_specs = [pl.BlockSpec((bm, D), lambda s, se, sh, sr, sm: (sr[s], 0)),
                pl.BlockSpec((None, D, fh), lambda s, se, sh, sr, sm: (se[s], 0, sh[s])),
                pl.BlockSpec((None, D, fh), lambda s, se, sh, sr, sm: (se[s], 0, sh[s])),
                pl.BlockSpec((None, 1, fh), lambda s, se, sh, sr, sm: (se[s], 0, sh[s])),
                pl.BlockSpec((None, 1, fh), lambda s, se, sh, sr, sm: (se[s], 0, sh[s])),
                pl.BlockSpec((None, fh, D), lambda s, se, sh, sr, sm: (se[s], sh[s], 0)),
                pl.BlockSpec((None, 1, D), lambda s, se, sh, sr, sm: (se[s], 0, 0))]
    grid_spec = pltpu.PrefetchScalarGridSpec(
        num_scalar_prefetch=4, grid=(S,), in_specs=in_specs,
        out_specs=pl.BlockSpec((bm, D), lambda s, se, sh, sr, sm: (sr[s], sh[s])),
        scratch_shapes=[pltpu.VMEM((fh, D), BF16)])
    return pl.pallas_call(
        _expert_kernel, grid_spec=grid_spec,
        out_shape=jax.ShapeDtypeStruct((NR, n_half * D), BF16),
        compiler_params=_cparams("arbitrary"),
        name="moe_experts",
    )(se, sh, sr, sm, xs, w1g, w1l, b1g, b1l, w2, b2)


def _combine_kernel(h_ref, yg_ref, tg_ref, nw_ref, o_ref, *, n_half):
    acc = h_ref[...]
    tg = tg_ref[...]
    D = acc.shape[1]
    for j in range(TOP_K):
        rows = yg_ref[j, :, :D].astype(F32)
        for hh in range(1, n_half):
            rows = rows + yg_ref[j, :, hh * D:(hh + 1) * D].astype(F32)
        acc = acc + rows * tg[:, j:j + 1]
    o_ref[...] = _rms(acc, nw_ref[...])


def _combine(h, yg, tg, nw, *, tm, n_half):
    M, D = h.shape
    tm = _pick_tile(M, tm, 8)
    return pl.pallas_call(
        functools.partial(_combine_kernel, n_half=n_half),
        grid=(M // tm,),
        in_specs=[pl.BlockSpec((tm, D), lambda i: (i, 0)),
                  pl.BlockSpec((TOP_K, tm, n_half * D), lambda i: (0, i, 0)),
                  pl.BlockSpec((tm, LANES), lambda i: (i, 0)),
                  pl.BlockSpec((1, D), lambda i: (0, 0))],
        out_specs=pl.BlockSpec((tm, D), lambda i: (i, 0)),
        out_shape=jax.ShapeDtypeStruct((M, D), F32),
        compiler_params=_cparams("parallel"),
        name="moe_combine",
    )(h, yg, tg, nw)


def _rope_tables(positions):
    inv = ROPE_THETA ** (-jnp.arange(0, ROPE_DIM, 2, dtype=F32) / ROPE_DIM)
    ang = positions.astype(F32)[:, None] * inv[None, :]
    cos, sin = jnp.cos(ang), jnp.sin(ang)
    cos2 = jnp.concatenate([cos, cos], axis=-1)
    sin2 = jnp.concatenate([sin, sin], axis=-1)
    n = positions.shape[0]
    ones = jnp.ones((n, NOPE_DIM), F32)
    z64 = jnp.zeros((n, QK_PAD - NOPE_DIM - ROPE_DIM), F32)
    z128 = jnp.zeros((n, NOPE_DIM), F32)
    qtab = jnp.concatenate([ones, cos2, z64, z128, sin2, z64], axis=-1)
    ktab = jnp.concatenate([cos2, sin2], axis=-1)
    return qtab, ktab


def _rot_cols(w):
    half = ROPE_DIM // 2
    return jnp.concatenate([-w[..., half:], w[..., :half]], axis=-1)


def _moe_schedule(top_idx, n_experts, bm, n_half, n_blocks):
    M = top_idx.shape[0]
    flat_e = top_idx.reshape(-1)
    onehot = (flat_e[:, None] == jnp.arange(n_experts, dtype=jnp.int32)[None, :]).astype(jnp.int32)
    rank = jnp.take_along_axis(jnp.cumsum(onehot, axis=0) - onehot, flat_e[:, None], axis=1)[:, 0]
    counts = jnp.sum(onehot, axis=0)
    nb = (counts + bm - 1) // bm
    blk_end = jnp.cumsum(nb)
    blk_start = blk_end - nb
    dest = blk_start[flat_e] * bm + rank
    flat_tok = jnp.repeat(jnp.arange(M, dtype=jnp.int32), TOP_K)
    row_tok = jnp.zeros((n_blocks * bm,), jnp.int32).at[dest].set(flat_tok)
    n_real = blk_end[-1]
    steps_end = jnp.cumsum(nb * n_half)
    s = jnp.arange(n_blocks * n_half, dtype=jnp.int32)
    real = s < n_real * n_half
    e_of = jnp.minimum(jnp.sum(steps_end[None, :] <= s[:, None], axis=1), n_experts - 1).astype(jnp.int32)
    local = s - (steps_end - nb * n_half)[e_of]
    nb_e = jnp.maximum(nb[e_of], 1)
    h_real = local // nb_e
    r_real = blk_start[e_of] + local % nb_e
    last_e = e_of[jnp.maximum(n_real * n_half - 1, 0)]
    tail = s - n_real * n_half
    n_tail = jnp.maximum(n_blocks - n_real, 1)
    se = jnp.where(real, e_of, last_e).astype(jnp.int32)
    sh = jnp.where(real, h_real, n_half - 1 - tail // n_tail).astype(jnp.int32)
    sr = jnp.where(real, r_real, n_real + tail % n_tail).astype(jnp.int32)
    sm = jnp.where(real, jnp.where(local % nb_e == 0, 2, 1), 0).astype(jnp.int32)
    return dest, row_tok, (se, sh, sr, sm)


def kernel(x_prompt, x_sample, cache_ckv, cache_krope, state_wkv, state_shift, page_table, norm_mix_w, w_in, rwkv_mu, rwkv_w0, rwkv_w2, rwkv_a0, rwkv_a2, rwkv_g2, rwkv_k_k, rwkv_k_a, rwkv_r_k, rwkv_ln_w, rwkv_ln_b, mla_q_norm_w, mla_w_uq, mla_kv_norm_w, mla_w_uk, mla_w_uv, w_out, norm_ffn_w, router_w, router_b, expert_w1, expert_b1, expert_w2, expert_b2, norm_final_w):
    B, T, D = x_prompt.shape
    Bd, Td, _ = x_sample.shape
    H = D // RWKV_HEAD_DIM
    HD = RWKV_HEAD_DIM
    dlora, alora, glora = rwkv_w2.shape[0], rwkv_a2.shape[0], rwkv_g2.shape[0]
    qlora, Hm, _ = mla_w_uq.shape
    kvlora = mla_w_uk.shape[0]
    n_experts = router_w.shape[1]
    F = expert_w2.shape[1]
    page = cache_ckv.shape[1]
    past = page_table.shape[1] * page
    scale = float((NOPE_DIM + ROPE_DIM) ** -0.5)
    row2 = lambda a: a.reshape(1, -1).astype(F32)

    sizes = [D, D, D, dlora, alora, glora, qlora, kvlora, ROPE_DIM, D, D]
    offs = np.concatenate([[0], np.cumsum(sizes)]).tolist()
    col = lambda i: w_in[:, offs[i]:offs[i + 1]]
    lw_pad, la_pad = _round_up(dlora, LANES), _round_up(alora, LANES)
    padc = lambda a, n: jnp.pad(a, ((0, 0), (0, n - a.shape[1])))
    perm = np.arange(D).reshape(H, HD).T.reshape(-1)
    pc = lambda a: a[..., perm]
    w5 = jnp.concatenate([pc(col(0)), pc(col(1)), pc(col(2)), pc(col(9)), col(10)], axis=1).astype(BF16)
    lora_w = lw_pad + la_pad + glora
    ws = jnp.concatenate([padc(col(3), lw_pad), padc(col(4), la_pad), col(5), col(6), col(7),
                          col(8), _rot_cols(col(8))], axis=1)
    col_cq, col_ckv, col_kr = lora_w, lora_w + qlora, lora_w + qlora + kvlora
    ws = padc(ws, _round_up(ws.shape[1], LANES)).astype(BF16)
    mu = rwkv_mu
    mur, muk, muv = row2(pc(mu[:D])), row2(pc(mu[D:2 * D])), row2(pc(mu[2 * D:3 * D]))
    m0 = 3 * D
    mul = jnp.concatenate([jnp.pad(mu[m0:m0 + dlora], (0, lw_pad - dlora)),
                           jnp.pad(mu[m0 + dlora:m0 + dlora + alora], (0, la_pad - alora)),
                           mu[m0 + dlora + alora:]]).reshape(1, -1)
    w2p = jnp.pad(pc(rwkv_w2), ((0, lw_pad - dlora), (0, 0))).astype(BF16)
    a2p = jnp.pad(pc(rwkv_a2), ((0, la_pad - alora), (0, 0))).astype(BF16)
    prep_consts = (mur, muk, muv, mul, row2(pc(rwkv_w0)), w2p, row2(pc(rwkv_a0)), a2p,
                   pc(rwkv_g2).astype(BF16), row2(pc(rwkv_k_k)), row2(pc(rwkv_k_a)),
                   row2(pc(rwkv_r_k.reshape(-1))), lw_pad, la_pad)

    zq = jnp.zeros((qlora, Hm, QK_PAD - NOPE_DIM - ROPE_DIM), F32)
    q_nope_w, q_rope_w = mla_w_uq[..., :NOPE_DIM], mla_w_uq[..., NOPE_DIM:]
    wqa = jnp.concatenate([q_nope_w, q_rope_w, zq], axis=-1).reshape(qlora, Hm * QK_PAD).astype(BF16)
    wqb = jnp.concatenate([jnp.zeros_like(q_nope_w), _rot_cols(q_rope_w), zq],
                          axis=-1).reshape(qlora, Hm * QK_PAD).astype(BF16)
    zk = jnp.zeros((kvlora, Hm, QK_PAD - NOPE_DIM), F32)
    wk_top = jnp.concatenate([mla_w_uk, zk], axis=-1).reshape(kvlora, Hm * QK_PAD)
    sel = jnp.concatenate([jnp.zeros((ROPE_DIM, NOPE_DIM), F32), jnp.eye(ROPE_DIM, dtype=F32),
                           jnp.zeros((ROPE_DIM, QK_PAD - NOPE_DIM - ROPE_DIM), F32)], axis=-1)
    wk = jnp.concatenate([wk_top, jnp.tile(sel, (1, Hm))], axis=0).astype(BF16)
    wv = mla_w_uv.reshape(kvlora, Hm * V_DIM).astype(BF16)
    w_uk_t = jnp.transpose(mla_w_uk, (1, 2, 0)).astype(BF16)
    w_uv_h = jnp.transpose(mla_w_uv, (1, 0, 2)).astype(BF16)
    wo_mla = w_out.astype(BF16)
    wo_rwkv = w_out[perm, :].astype(BF16)
    rw = jnp.pad(router_w, ((0, 0), (0, LANES - n_experts)))
    rb = jnp.pad(router_b, (0, LANES - n_experts)).reshape(1, LANES)
    w1g, w1l = _w1_prep(expert_w1, tr=512, tc=2048)
    b1g = expert_b1[:, None, 0::2]
    b1l = expert_b1[:, None, 1::2]
    w2e = expert_w2
    b2e = expert_b2[:, None, :]
    nmw = row2(norm_mix_w)

    def project_stage(x2d, positions, tm_mla):
        p5 = _project(x2d, nmw, w5, normalize=True, tm=1024, tn=512)
        ps = _project(x2d, nmw, ws, normalize=True, tm=1024, tn=ws.shape[1])
        qtab, ktab = _rope_tables(positions)
        mla = _mla_prep(ps, qtab, ktab, row2(mla_q_norm_w), row2(mla_kv_norm_w), wqa, wqb, wk, wv,
                        col_cq=col_cq, col_ckv=col_ckv, col_kr=col_kr, qlora=qlora, kvlora=kvlora,
                        n_heads=Hm, hg=min(4, Hm), tm=tm_mla, scale=scale)
        return p5, ps, mla

    merge_consts = (row2(pc(rwkv_ln_w)), row2(pc(rwkv_ln_b)), wo_rwkv, wo_mla,
                    row2(norm_ffn_w), rw, rb)

    xp = x_prompt.reshape(B * T, D)
    p5_p, ps_p, mla_p = project_stage(xp, jnp.arange(T), min(512, T))
    vsplit = max(1, LANES // (B * H))
    assert HD % vsplit == 0 and B * H * vsplit <= LANES
    vr = HD // vsplit
    tb_p = min(64, T)
    zr, zw, zk, zv, za, zb, g_p, bo_p = _rwkv_prep_packed(
        p5_p.reshape(B, T, -1), ps_p.reshape(B, T, -1), prep_consts, B=B, T=T, tt=tb_p, D=D,
        lora_w=lora_w, n_heads=H, vsplit=vsplit)
    s0_p = jnp.zeros((HD, vr, LANES), F32)
    y_pk, s_kl = _rwkv_scan(zr, zw, zk, zv, za, zb, s0_p, tb=tb_p, packed=True)
    nl_p = B * vsplit * H
    wkv_p = s_kl[:, :, :nl_p].reshape(HD, vr, B, vsplit, H).transpose(2, 4, 3, 1, 0).reshape(B, H, HD, HD)

    q_p, kf_p, vf_p, ckv_p, kr_p = mla_p
    o_mla_p = _flash_attention(q_p, kf_p, vf_p, B=B, T=T, n_heads=Hm, tq=1024, tk=512)
    h_p, hn_p, ti_p, tg_p = _merge(
        x_prompt, y_pk.reshape(T * vr, LANES), bo_p, g_p, p5_p.reshape(B, T, -1),
        o_mla_p.reshape(B, T, -1), merge_consts, tm=64, n_experts=n_experts, n_heads=H, vsplit=vsplit)

    xs_tm = jnp.swapaxes(x_sample, 0, 1).reshape(Td * Bd, D)
    pos_s = jnp.repeat(past + jnp.arange(Td), Bd)
    p5_s, ps_s, mla_s = project_stage(xs_tm, pos_s, Td * Bd)
    s5 = _project(state_shift, nmw, w5, normalize=False, tm=1024, tn=512).reshape(1, Bd, -1)
    ss = _project(state_shift, nmw, ws, normalize=False, tm=1024, tn=ws.shape[1]).reshape(1, Bd, -1)
    r_s, lw_s, k_s, v_s, al_s, be_s, g_s, bo_s = _rwkv_prep(
        p5_s, ps_s, s5, ss, prep_consts, n_groups=1, n_tiles=Td, tt=Bd, shift=Bd, D=D, lora_w=lora_w,
        n_heads=H)

    def to_kl_s(a):
        return a.reshape(Td, Bd, HD, H).transpose(0, 2, 1, 3).reshape(Td, HD, Bd * H)

    nl_s = Bd * H
    pad_s = _round_up(nl_s, LANES) - nl_s
    pads = lambda a: jnp.pad(a, ((0, 0), (0, 0), (0, pad_s))) if pad_s else a
    s0_s = pads(state_wkv.transpose(3, 2, 0, 1).reshape(HD, HD, nl_s))
    y_sl, s_sl = _rwkv_scan(pads(to_kl_s(r_s)), pads(to_kl_s(lw_s)), pads(to_kl_s(k_s)),
                            pads(to_kl_s(v_s)), pads(to_kl_s(al_s)), pads(to_kl_s(be_s)), s0_s, tb=Td)
    y_s = y_sl[:, :, :nl_s].reshape(Td, HD, Bd, H).transpose(0, 2, 1, 3).reshape(Td * Bd, D)
    wkv_s = s_sl[:, :, :nl_s].reshape(HD, HD, Bd, H).transpose(2, 3, 1, 0)

    q_s, _, _, ckv_s, kr_s = mla_s
    q4 = q_s.reshape(Td, Bd, Hm, QK_PAD)
    qn_h = q4[..., :NOPE_DIM].transpose(2, 1, 0, 3).reshape(Hm, Bd * Td, NOPE_DIM)
    q_lat = _head_matmul(qn_h, w_uk_t)
    q_lat = q_lat.reshape(Hm, Bd, Td, kvlora).transpose(1, 0, 2, 3).reshape(Bd, Hm * Td, kvlora)
    q_rp = q4[..., NOPE_DIM:NOPE_DIM + ROPE_DIM].transpose(1, 2, 0, 3).reshape(Bd, Hm * Td, ROPE_DIM)
    qs = jnp.concatenate([q_lat.astype(BF16), q_rp], axis=-1)
    ckv_s_bt = ckv_s.reshape(Td, Bd, kvlora).transpose(1, 0, 2)
    kr_s_bt = kr_s.reshape(Td, Bd, ROPE_DIM).transpose(1, 0, 2)
    npad = _round_up(Td, 8)
    padn = lambda a: jnp.pad(a, ((0, 0), (0, npad - Td), (0, 0)))
    o_lat = _paged_attention(page_table, qs, padn(ckv_s_bt), padn(kr_s_bt), cache_ckv, cache_krope,
                             pg=16, n_new=Td)
    o_lat_h = o_lat.reshape(Bd, Hm, Td, kvlora).transpose(1, 2, 0, 3).reshape(Hm, Td * Bd, kvlora)
    o_mla_s = _head_matmul(o_lat_h, w_uv_h)
    o_mla_s = o_mla_s.transpose(1, 0, 2).reshape(Td * Bd, Hm * V_DIM)

    g1 = lambda a: a[None]
    h_s, hn_s, ti_s, tg_s = _merge(g1(xs_tm), g1(y_s), g1(bo_s), g1(g_s), g1(p5_s), g1(o_mla_s),
                                   merge_consts, tm=128, n_experts=n_experts, n_heads=H, vsplit=0)
    flat = lambda a: a.reshape(-1, a.shape[-1])
    h_all = jnp.concatenate([flat(h_p), flat(h_s)], axis=0)
    hn_all = jnp.concatenate([flat(hn_p), flat(hn_s)], axis=0)
    ti_all = jnp.concatenate([flat(ti_p), flat(ti_s)], axis=0)[:, :TOP_K]
    tg_all = jnp.concatenate([flat(tg_p), flat(tg_s)], axis=0)
    M = h_all.shape[0]
    bm = 256
    n_half = 2 if F % (2 * LANES) == 0 else 1
    n_blocks = (M * TOP_K + n_experts * (bm - 1)) // bm
    dest, row_tok, sched = _moe_schedule(ti_all, n_experts, bm, n_half, n_blocks)
    xs_rows = hn_all[row_tok]
    y_halves = _experts(sched, xs_rows, w1g, w1l, b1g, b1l, w2e, b2e, bm=bm, n_half=n_half)
    dest_slot_major = dest.reshape(M, TOP_K).T.reshape(-1)
    yg = y_halves[dest_slot_major].reshape(TOP_K, M, n_half * D)
    y_all = _combine(h_all, yg, tg_all, row2(norm_final_w), tm=128, n_half=n_half)

    y_prompt = y_all[:B * T].reshape(B, T, D)
    y_sample = y_all[B * T:].reshape(Td, Bd, D).transpose(1, 0, 2)
    last = jnp.concatenate([x_prompt[:, -1], x_sample[:, -1]], axis=0)
    shift_out = _rmsnorm_rows(last, nmw)
    return (y_prompt, y_sample,
            ckv_p.reshape(B, T // page, page, kvlora), kr_p.reshape(B, T // page, page, ROPE_DIM),
            wkv_p, shift_out[:B],
            ckv_s_bt, kr_s_bt, wkv_s, shift_out[B:])
```

```python
import functools

import numpy as np
import jax
import jax.numpy as jnp
from jax import lax
from jax.experimental import pallas as pl
from jax.experimental.pallas import tpu as pltpu

F32 = jnp.float32
BF16 = jnp.bfloat16

NORM_EPS = 1e-6
GN_EPS = 64e-5
RWKV_HEAD_DIM = 64
NOPE_DIM = 128
ROPE_DIM = 64
V_DIM = 128
QK_PAD = 256
ROPE_THETA = 10000.0
NEG_INF = -1e30
TOP_K = 4
SWIGLU_ALPHA = 1.702
SWIGLU_LIMIT = 7.0
LANES = 128
VMEM_LIMIT = 56 * 1024 * 1024


def _round_up(n, m):
    return (n + m - 1) // m * m


def _pick_tile(n, target, mult):
    if n <= target:
        return n
    t = target // mult * mult
    while n % t:
        t -= mult
    return t


def _cparams(*sem):
    return pltpu.CompilerParams(dimension_semantics=sem, vmem_limit_bytes=VMEM_LIMIT)


def _split3(x):
    hi = x.astype(BF16)
    r1 = x - hi.astype(F32)
    mid = r1.astype(BF16)
    lo = (r1 - mid.astype(F32)).astype(BF16)
    return hi, mid, lo


def _tree_sum(xs):
    xs = list(xs)
    while len(xs) > 1:
        xs = [xs[i] + xs[i + 1] for i in range(0, len(xs) - 1, 2)] + ([xs[-1]] if len(xs) % 2 else [])
    return xs[0]


def _rms(x, w):
    return x * lax.rsqrt(jnp.mean(x * x, axis=-1, keepdims=True) + NORM_EPS) * w


def _proj_kernel(x_ref, nw_ref, w_ref, o_ref, xn_ref, *, normalize):
    @pl.when(pl.program_id(1) == 0)
    def _():
        x = x_ref[...]
        if normalize:
            x = _rms(x, nw_ref[...])
        xn_ref[...] = x.astype(BF16)

    o_ref[...] = jnp.dot(xn_ref[...], w_ref[...], preferred_element_type=F32)


def _project(x, norm_w, w, *, normalize, tm, tn):
    M, D = x.shape
    N = w.shape[1]
    tm = _pick_tile(M, tm, 8)
    tn = _pick_tile(N, tn, LANES)
    return pl.pallas_call(
        functools.partial(_proj_kernel, normalize=normalize),
        grid=(M // tm, N // tn),
        in_specs=[pl.BlockSpec((tm, D), lambda i, j: (i, 0)),
                  pl.BlockSpec((1, D), lambda i, j: (0, 0)),
                  pl.BlockSpec((D, tn), lambda i, j: (0, j))],
        out_specs=pl.BlockSpec((tm, tn), lambda i, j: (i, j)),
        out_shape=jax.ShapeDtypeStruct((M, N), F32),
        scratch_shapes=[pltpu.VMEM((tm, D), BF16)],
        compiler_params=_cparams("parallel", "arbitrary"),
        name="norm_proj",
    )(x, norm_w, w)


def _rmsnorm_rows_kernel(x_ref, w_ref, o_ref):
    o_ref[...] = _rms(x_ref[...], w_ref[...])


def _rmsnorm_rows(x, w):
    return pl.pallas_call(
        _rmsnorm_rows_kernel,
        out_shape=jax.ShapeDtypeStruct(x.shape, F32),
        name="rmsnorm_rows",
    )(x, w)


def _head_sum(x, n_heads):
    d = x.shape[1]
    assert d % LANES == 0 and LANES % n_heads == 0
    s = _tree_sum([x[:, j * LANES:(j + 1) * LANES] for j in range(d // LANES)])
    shift = LANES // 2
    while shift >= n_heads:
        s = s + pltpu.roll(s, shift, axis=1)
        shift //= 2
    return jnp.concatenate([s] * (d // LANES), axis=-1)


def _rwkv_prep_math(r, k, v, lo, w0_ref, w2_ref, a0_ref, a2_ref, g2_ref, kk_ref, ka_ref, rk_ref,
                    *, lw_pad, la_pad, n_heads):
    hw = lo[:, :lw_pad]
    ha = lo[:, lw_pad:lw_pad + la_pad]
    hg = lo[:, lw_pad + la_pad:]
    mm = lambda a, b_ref: jnp.dot(a.astype(BF16), b_ref[...], preferred_element_type=F32)
    w = -jax.nn.softplus(-(w0_ref[...] + mm(jnp.tanh(hw), w2_ref))) - 0.5
    a = jax.nn.sigmoid(a0_ref[...] + mm(ha, a2_ref))
    g = mm(jax.nn.sigmoid(hg), g2_ref)
    kk = k * kk_ref[...]
    kk = kk * lax.rsqrt(jnp.maximum(_head_sum(kk * kk, n_heads), 1e-24))
    k2 = k * (1.0 + (a - 1.0) * ka_ref[...])
    bsum = _head_sum(r * k2 * rk_ref[...], n_heads)
    return r, jnp.exp(-jnp.exp(w)), k2, v, -kk, kk * a, g, bsum * v


def _rwkv_prep_packed_kernel(r_ref, k_ref, v_ref, l_ref,
                             mur_ref, muk_ref, muv_ref, mul_ref, w0_ref, w2_ref, a0_ref, a2_ref,
                             g2_ref, kk_ref, ka_ref, rk_ref,
                             zr_ref, zw_ref, zk_ref, zv_ref, za_ref, zb_ref, go_ref, bo_ref,
                             cr_ref, ck_ref, cv_ref, cl_ref, *, lw_pad, la_pad, n_heads, vsplit):
    t = pl.program_id(0)
    n_seq, tt, D = r_ref.shape
    nk = D // n_heads
    vr = nk // vsplit

    @pl.when(t == 0)
    def _():
        for c_ref in (cr_ref, ck_ref, cv_ref, cl_ref):
            c_ref[...] = jnp.zeros(c_ref.shape, F32)

    def lerp(p, c_ref, b, mu_ref):
        rolled = pltpu.roll(p, 1, axis=0)
        row = lax.broadcasted_iota(jnp.int32, p.shape, 0)
        prev = jnp.where(row == 0, c_ref[b], rolled)
        c_ref[b] = p[tt - 1:, :]
        return p + (prev - p) * mu_ref[...]

    outs = []
    for b in range(n_seq):
        r = lerp(r_ref[b], cr_ref, b, mur_ref)
        k = lerp(k_ref[b], ck_ref, b, muk_ref)
        v = lerp(v_ref[b], cv_ref, b, muv_ref)
        lo = lerp(l_ref[b], cl_ref, b, mul_ref)
        res = _rwkv_prep_math(r, k, v, lo, w0_ref, w2_ref, a0_ref, a2_ref, g2_ref, kk_ref, ka_ref,
                              rk_ref, lw_pad=lw_pad, la_pad=la_pad, n_heads=n_heads)
        go_ref[b] = res[6]
        bo_ref[b] = res[7]
        outs.append(res)

    used = n_seq * vsplit * n_heads
    zpad = [jnp.zeros((tt, LANES - used), F32)] if used < LANES else []
    piece = lambda x, j: x[:, j * n_heads:(j + 1) * n_heads]
    for z_ref, idx in ((zr_ref, 0), (zw_ref, 1), (zk_ref, 2), (za_ref, 4), (zb_ref, 5)):
        for kq in range(nk):
            parts = [piece(outs[b][idx], kq) for b in range(n_seq) for _ in range(vsplit)]
            z_ref[kq * tt:(kq + 1) * tt, :] = jnp.concatenate(parts + zpad, axis=-1)
    for vq in range(vr):
        parts = [piece(outs[b][3], vs * vr + vq) for b in range(n_seq) for vs in range(vsplit)]
        zv_ref[vq * tt:(vq + 1) * tt, :] = jnp.concatenate(parts + zpad, axis=-1)


def _rwkv_prep_packed(p5, ps, consts, *, B, T, tt, D, lora_w, n_heads, vsplit):
    (mur, muk, muv, mul, w0, w2p, a0, a2p, g2, k_k, k_a, r_k, lw_pad, la_pad) = consts
    nk = D // n_heads
    vr = nk // vsplit
    n_tiles = T // tt
    big = lambda c: pl.BlockSpec((B, tt, D), lambda t, c=c: (0, t, c))
    full = lambda a: pl.BlockSpec(a.shape, lambda t: (0,) * a.ndim)
    in_specs = [big(0), big(1), big(2), pl.BlockSpec((B, tt, lora_w), lambda t: (0, t, 0)),
                full(mur), full(muk), full(muv), full(mul), full(w0), full(w2p), full(a0),
                full(a2p), full(g2), full(k_k), full(k_a), full(r_k)]
    zspec = pl.BlockSpec((nk * tt, LANES), lambda t: (t, 0))
    vspec = pl.BlockSpec((vr * tt, LANES), lambda t: (t, 0))
    tok = pl.BlockSpec((B, tt, D), lambda t: (0, t, 0))
    zshape = jax.ShapeDtypeStruct((n_tiles * nk * tt, LANES), F32)
    vshape = jax.ShapeDtypeStruct((n_tiles * vr * tt, LANES), F32)
    tshape = jax.ShapeDtypeStruct((B, T, D), F32)
    return pl.pallas_call(
        functools.partial(_rwkv_prep_packed_kernel, lw_pad=lw_pad, la_pad=la_pad, n_heads=n_heads,
                          vsplit=vsplit),
        grid=(n_tiles,),
        in_specs=in_specs,
        out_specs=[zspec, zspec, zspec, vspec, zspec, zspec, tok, tok],
        out_shape=[zshape, zshape, zshape, vshape, zshape, zshape, tshape, tshape],
        scratch_shapes=[pltpu.VMEM((B, 1, D), F32)] * 3 + [pltpu.VMEM((B, 1, lora_w), F32)],
        compiler_params=_cparams("arbitrary"),
        name="rwkv_prep_packed",
    )(p5, p5, p5, ps, mur, muk, muv, mul, w0, w2p, a0, a2p, g2, k_k, k_a, r_k)


def _rwkv_prep_kernel(r_ref, k_ref, v_ref, l_ref, sr_ref, sk_ref, sv_ref, sl_ref,
                      mur_ref, muk_ref, muv_ref, mul_ref, w0_ref, w2_ref, a0_ref, a2_ref, g2_ref,
                      kk_ref, ka_ref, rk_ref,
                      ro_ref, lw_ref, ko_ref, vo_ref, al_ref, be_ref, go_ref, bo_ref,
                      cr_ref, ck_ref, cv_ref, cl_ref, *, shift, lw_pad, la_pad, n_heads):
    t = pl.program_id(1)

    @pl.when(t == 0)
    def _():
        cr_ref[...] = sr_ref[...]
        ck_ref[...] = sk_ref[...]
        cv_ref[...] = sv_ref[...]
        cl_ref[...] = sl_ref[...]

    def lerp(p_ref, c_ref, mu_ref):
        p = p_ref[...]
        if shift == 1:
            rolled = pltpu.roll(p, 1, axis=0)
            row = lax.broadcasted_iota(jnp.int32, p.shape, 0)
            prev = jnp.where(row == 0, c_ref[...], rolled)
            c_new = p[p.shape[0] - 1:, :]
        else:
            prev = c_ref[...]
            c_new = p
        xx = p + (prev - p) * mu_ref[...]
        c_ref[...] = c_new
        return xx

    r = lerp(r_ref, cr_ref, mur_ref)
    k = lerp(k_ref, ck_ref, muk_ref)
    v = lerp(v_ref, cv_ref, muv_ref)
    lo = lerp(l_ref, cl_ref, mul_ref)
    r, lw, k2, v, al, be, g, bonus = _rwkv_prep_math(
        r, k, v, lo, w0_ref, w2_ref, a0_ref, a2_ref, g2_ref, kk_ref, ka_ref, rk_ref,
        lw_pad=lw_pad, la_pad=la_pad, n_heads=n_heads)
    ro_ref[...] = r
    lw_ref[...] = lw
    ko_ref[...] = k2
    vo_ref[...] = v
    al_ref[...] = al
    be_ref[...] = be
    go_ref[...] = g
    bo_ref[...] = bonus


def _rwkv_prep(p5, ps, shift5, shifts, consts, *, n_groups, n_tiles, tt, shift, D, lora_w, n_heads):
    M = p5.shape[0]
    (mur, muk, muv, mul, w0, w2p, a0, a2p, g2, k_k, k_a, r_k, lw_pad, la_pad) = consts
    row = lambda g, t: (g * n_tiles + t)
    big = lambda c: pl.BlockSpec((tt, D), lambda g, t, c=c: (row(g, t), c))
    sh = lambda c: pl.BlockSpec((None, shift, D), lambda g, t, c=c: (g, 0, c))
    full = lambda a: pl.BlockSpec(a.shape, lambda g, t: (0,) * a.ndim)
    in_specs = [big(0), big(1), big(2),
                pl.BlockSpec((tt, lora_w), lambda g, t: (row(g, t), 0)),
                sh(0), sh(1), sh(2),
                pl.BlockSpec((None, shift, lora_w), lambda g, t: (g, 0, 0)),
                full(mur), full(muk), full(muv), full(mul), full(w0), full(w2p), full(a0),
                full(a2p), full(g2), full(k_k), full(k_a), full(r_k)]
    out_spec = pl.BlockSpec((tt, D), lambda g, t: (row(g, t), 0))
    outs = pl.pallas_call(
        functools.partial(_rwkv_prep_kernel, shift=shift, lw_pad=lw_pad, la_pad=la_pad,
                          n_heads=n_heads),
        grid=(n_groups, n_tiles),
        in_specs=in_specs,
        out_specs=[out_spec] * 8,
        out_shape=[jax.ShapeDtypeStruct((M, D), F32)] * 8,
        scratch_shapes=[pltpu.VMEM((shift, D), F32)] * 3 + [pltpu.VMEM((shift, lora_w), F32)],
        compiler_params=_cparams("parallel", "arbitrary"),
        name="rwkv_prep",
    )(p5, p5, p5, ps, shift5, shift5, shift5, shifts,
      mur, muk, muv, mul, w0, w2p, a0, a2p, g2, k_k, k_a, r_k)
    return outs


def _rwkv_scan_kernel(r_ref, w_ref, k_ref, v_ref, a_ref, b_ref, s0_ref, y_ref, sf_ref, s_ref,
                      *, tb, vr, nk, packed):
    tblk = pl.program_id(1)

    @pl.when(tblk == 0)
    def _():
        s_ref[...] = s0_ref[...]

    n_acc = 8
    if packed:
        row = lambda ref, t, k: jnp.broadcast_to(ref[pl.ds(k * tb + t, 1), :], (vr, LANES))
        tile = lambda ref, t, n: ref[pl.ds(t, n, stride=tb), :]
    else:
        row = lambda ref, t, k: jnp.broadcast_to(ref[t, pl.ds(k, 1), :], (vr, LANES))
        tile = lambda ref, t, n: ref[t]

    def accumulate(parts, k, term):
        parts[k % n_acc] = term if parts[k % n_acc] is None else parts[k % n_acc] + term

    sa_parts = [None] * n_acc
    for k in range(nk):
        accumulate(sa_parts, k, s_ref[k] * row(a_ref, 0, k))
    sa0 = _tree_sum(sa_parts)

    def step(t, sa):
        vt = tile(v_ref, t, vr)
        tn = jnp.minimum(t + 1, tb - 1)
        y_parts = [None] * n_acc
        sa_parts = [None] * n_acc
        for k in range(nk):
            s = s_ref[k] * row(w_ref, t, k) + sa * row(b_ref, t, k) + vt * row(k_ref, t, k)
            s_ref[k] = s
            accumulate(y_parts, k, s * row(r_ref, t, k))
            accumulate(sa_parts, k, s * row(a_ref, tn, k))
        if packed:
            y_ref[pl.ds(t, vr, stride=tb), :] = _tree_sum(y_parts)
        else:
            y_ref[t] = _tree_sum(y_parts)
        return _tree_sum(sa_parts)

    lax.fori_loop(0, tb, step, sa0)

    @pl.when(tblk == pl.num_programs(1) - 1)
    def _():
        sf_ref[...] = s_ref[...]


def _rwkv_scan(r, w, k, v, a, b, s0, *, tb, packed=False):
    K, VR, NL = s0.shape
    if packed:
        T = r.shape[0] // K
        assert NL == LANES and T % tb == 0
        kspec = pl.BlockSpec((K * tb, LANES), lambda n, t: (t, 0))
        vspec = pl.BlockSpec((VR * tb, LANES), lambda n, t: (t, 0))
    else:
        T = r.shape[0]
        tb = min(tb, T)
        assert T % tb == 0 and NL % LANES == 0
        kspec = pl.BlockSpec((tb, K, LANES), lambda n, t: (t, 0, n))
        vspec = pl.BlockSpec((tb, VR, LANES), lambda n, t: (t, 0, n))
    vin = vspec
    sspec = pl.BlockSpec((K, VR, LANES), lambda n, t: (0, 0, n))
    return pl.pallas_call(
        functools.partial(_rwkv_scan_kernel, tb=tb, vr=VR, nk=K, packed=packed),
        grid=(NL // LANES, T // tb),
        in_specs=[kspec, kspec, kspec, vin, kspec, kspec, sspec],
        out_specs=[vspec, sspec],
        out_shape=[jax.ShapeDtypeStruct(v.shape, F32),
                   jax.ShapeDtypeStruct((K, VR, NL), F32)],
        scratch_shapes=[pltpu.VMEM((K, VR, LANES), F32)],
        compiler_params=_cparams("parallel", "arbitrary"),
        name="rwkv_scan",
    )(r, w, k, v, a, b, s0)


def _mla_prep_kernel(cq_ref, ckv_ref, kr_ref, qtab_ref, ktab_ref, qn_ref, kvn_ref,
                     wqa_ref, wqb_ref, wk_ref, wv_ref,
                     q_ref, k_ref, v_ref, ckv_o_ref, kr_o_ref, cqn_ref, ckk_ref, *, hg, scale):
    @pl.when(pl.program_id(1) == 0)
    def _():
        cqn_ref[...] = _rms(cq_ref[...], qn_ref[...]).astype(BF16)
        ckv = _rms(ckv_ref[...], kvn_ref[...])
        ckv_o_ref[...] = ckv
        kr = kr_ref[...]
        ktab = ktab_ref[...]
        krope = kr[:, :ROPE_DIM] * ktab[:, :ROPE_DIM] + kr[:, ROPE_DIM:] * ktab[:, ROPE_DIM:]
        kr_o_ref[...] = krope
        ckk_ref[...] = jnp.concatenate([ckv, krope], axis=-1).astype(BF16)

    cqn = cqn_ref[...]
    qtab = qtab_ref[...]
    cosf = jnp.concatenate([qtab[:, :QK_PAD]] * hg, axis=-1)
    sinf = jnp.concatenate([qtab[:, QK_PAD:]] * hg, axis=-1)
    qa = jnp.dot(cqn, wqa_ref[...], preferred_element_type=F32)
    qb = jnp.dot(cqn, wqb_ref[...], preferred_element_type=F32)
    q_ref[...] = ((qa * cosf + qb * sinf) * scale).astype(BF16)
    ckk = ckk_ref[...]
    k_ref[...] = jnp.dot(ckk, wk_ref[...], preferred_element_type=F32).astype(BF16)
    v_ref[...] = jnp.dot(ckk[:, :ckk.shape[1] - ROPE_DIM], wv_ref[...],
                         preferred_element_type=F32).astype(BF16)


def _mla_prep(ps, qtab, ktab, qn, kvn, wqa, wqb, wk, wv, *, col_cq, col_ckv, col_kr, qlora, kvlora,
              n_heads, hg, tm, scale):
    M = ps.shape[0]
    tm = _pick_tile(M, tm, 8)
    n_tab = qtab.shape[0] // tm
    grid = (M // tm, n_heads // hg)
    tabspec = lambda a: pl.BlockSpec((tm, a.shape[1]), lambda i, j: (i % n_tab, 0))
    full = lambda a: pl.BlockSpec(a.shape, lambda i, j: (0,) * a.ndim)
    in_specs = [pl.BlockSpec((tm, qlora), lambda i, j: (i, col_cq // qlora)),
                pl.BlockSpec((tm, kvlora), lambda i, j: (i, col_ckv // kvlora)),
                pl.BlockSpec((tm, 2 * ROPE_DIM), lambda i, j: (i, col_kr // (2 * ROPE_DIM))),
                tabspec(qtab), tabspec(ktab), full(qn), full(kvn),
                pl.BlockSpec((qlora, hg * QK_PAD), lambda i, j: (0, j)),
                pl.BlockSpec((qlora, hg * QK_PAD), lambda i, j: (0, j)),
                pl.BlockSpec((kvlora + ROPE_DIM, hg * QK_PAD), lambda i, j: (0, j)),
                pl.BlockSpec((kvlora, hg * V_DIM), lambda i, j: (0, j))]
    out_specs = [pl.BlockSpec((tm, hg * QK_PAD), lambda i, j: (i, j)),
                 pl.BlockSpec((tm, hg * QK_PAD), lambda i, j: (i, j)),
                 pl.BlockSpec((tm, hg * V_DIM), lambda i, j: (i, j)),
                 pl.BlockSpec((tm, kvlora), lambda i, j: (i, 0)),
                 pl.BlockSpec((tm, ROPE_DIM), lambda i, j: (i, 0))]
    out_shape = [jax.ShapeDtypeStruct((M, n_heads * QK_PAD), BF16),
                 jax.ShapeDtypeStruct((M, n_heads * QK_PAD), BF16),
                 jax.ShapeDtypeStruct((M, n_heads * V_DIM), BF16),
                 jax.ShapeDtypeStruct((M, kvlora), F32),
                 jax.ShapeDtypeStruct((M, ROPE_DIM), F32)]
    return pl.pallas_call(
        functools.partial(_mla_prep_kernel, hg=hg, scale=scale),
        grid=grid, in_specs=in_specs, out_specs=out_specs, out_shape=out_shape,
        scratch_shapes=[pltpu.VMEM((tm, qlora), BF16), pltpu.VMEM((tm, kvlora + ROPE_DIM), BF16)],
        compiler_params=_cparams("parallel", "arbitrary"),
        name="mla_prep",
    )(ps, ps, ps, qtab, ktab, qn, kvn, wqa, wqb, wk, wv)


def _flash_kernel(q_ref, k_ref, v_ref, o_ref, *, tq, tk):
    i = pl.program_id(2)
    q = q_ref[...]
    n_sub = tq // tk

    def block(j, carry, masked):
        m, l, acc = carry
        start = pl.multiple_of(j * tk, tk)
        kb = k_ref[pl.ds(start, tk), :]
        vb = v_ref[pl.ds(start, tk), :]
        s = lax.dot_general(q, kb, (((1,), (1,)), ((), ())), preferred_element_type=F32)
        if masked:
            qpos = i * tq + lax.broadcasted_iota(jnp.int32, s.shape, 0)
            kpos = j * tk + lax.broadcasted_iota(jnp.int32, s.shape, 1)
            s = jnp.where(kpos <= qpos, s, NEG_INF)
        m_new = jnp.maximum(m, jnp.max(s, axis=-1, keepdims=True))
        p = jnp.exp(s - m_new)
        corr = jnp.exp(m - m_new)
        l = corr * l + jnp.sum(p, axis=-1, keepdims=True)
        acc = corr * acc + jnp.dot(p.astype(BF16), vb, preferred_element_type=F32)
        return m_new, l, acc

    init = (jnp.full((tq, 1), NEG_INF, F32), jnp.zeros((tq, 1), F32),
            jnp.zeros((tq, v_ref.shape[1]), F32))
    carry = lax.fori_loop(0, i * n_sub, lambda j, c: block(j, c, False), init)
    for d in range(n_sub):
        carry = block(i * n_sub + d, carry, True)
    m, l, acc = carry
    o_ref[...] = (acc / l).astype(o_ref.dtype)


def _flash_attention(q, k, v, *, B, T, n_heads, tq, tk):
    tq = min(tq, T)
    tk = min(tk, tq)
    nq = T // tq
    return pl.pallas_call(
        functools.partial(_flash_kernel, tq=tq, tk=tk),
        grid=(B, n_heads, nq),
        in_specs=[pl.BlockSpec((tq, QK_PAD), lambda b, h, i: (b * nq + i, h)),
                  pl.BlockSpec((T, QK_PAD), lambda b, h, i: (b, h)),
                  pl.BlockSpec((T, V_DIM), lambda b, h, i: (b, h))],
        out_specs=pl.BlockSpec((tq, V_DIM), lambda b, h, i: (b * nq + i, h)),
        out_shape=jax.ShapeDtypeStruct((B * T, n_heads * V_DIM), F32),
        compiler_params=_cparams("parallel", "parallel", "arbitrary"),
        name="mla_flash",
    )(q, k, v)


def _bmm_kernel(x_ref, w_ref, o_ref):
    o_ref[...] = jnp.dot(x_ref[...].astype(BF16), w_ref[...], preferred_element_type=F32)


def _head_matmul(x, w):
    H, R, K = x.shape
    N = w.shape[2]
    return pl.pallas_call(
        _bmm_kernel,
        grid=(H,),
        in_specs=[pl.BlockSpec((None, R, K), lambda h: (h, 0, 0)),
                  pl.BlockSpec((None, K, N), lambda h: (h, 0, 0))],
        out_specs=pl.BlockSpec((None, R, N), lambda h: (h, 0, 0)),
        out_shape=jax.ShapeDtypeStruct((H, R, N), F32),
        compiler_params=_cparams("parallel"),
        name="head_matmul",
    )(x, w)


def _paged_kernel(pt_ref, q_ref, cn_ref, kn_ref, ckv_hbm, kr_hbm, o_ref,
                  cbuf, kbuf, csem, ksem, m_ref, l_ref, acc_ref, *, pg, n_new, kvlora):
    b = pl.program_id(0)
    g = pl.program_id(1)
    n_g = pl.num_programs(1)
    step = b * n_g + g
    slot = step % 2

    def page_copies(bb, gg, sl):
        cps = []
        for i in range(pg):
            page = pt_ref[bb, gg * pg + i]
            cps.append(pltpu.make_async_copy(ckv_hbm.at[page], cbuf.at[sl, i], csem.at[sl]))
            cps.append(pltpu.make_async_copy(kr_hbm.at[page], kbuf.at[sl, i], ksem.at[sl]))
        return cps

    @pl.when(step == 0)
    def _():
        for cp in page_copies(0, 0, 0):
            cp.start()

    nxt = step + 1

    @pl.when(nxt < pl.num_programs(0) * n_g)
    def _():
        for cp in page_copies(nxt // n_g, nxt % n_g, 1 - slot):
            cp.start()

    for cp in page_copies(b, g, slot):
        cp.wait()
    ckv_refs = [cbuf.at[slot, i] for i in range(pg)]
    kr_refs = [kbuf.at[slot, i] for i in range(pg)]

    @pl.when(g == 0)
    def _():
        m_ref[...] = jnp.full(m_ref.shape, NEG_INF, F32)
        l_ref[...] = jnp.zeros(l_ref.shape, F32)
        acc_ref[...] = jnp.zeros(acc_ref.shape, F32)

    q = q_ref[...]
    ql = q[:, :kvlora]
    qr = q[:, kvlora:]
    dn = (((1,), (1,)), ((), ()))

    def scores(c_ref, k_ref, k_is_transposed):
        kb = k_ref[...].astype(BF16)
        if k_is_transposed:
            rope = jnp.dot(qr, kb, preferred_element_type=F32)
        else:
            rope = lax.dot_general(qr, kb, dn, preferred_element_type=F32)
        return lax.dot_general(ql, c_ref[...].astype(BF16), dn, preferred_element_type=F32) + rope

    def update(s, value_refs):
        m = m_ref[...]
        m_new = jnp.maximum(m, jnp.max(s, axis=-1, keepdims=True))
        p = jnp.exp(s - m_new).astype(BF16)
        corr = jnp.exp(m - m_new)
        l_ref[...] = corr * l_ref[...] + jnp.sum(p.astype(F32), axis=-1, keepdims=True)
        pv = None
        off = 0
        for c_ref in value_refs:
            n = c_ref.shape[0]
            part = jnp.dot(p[:, off:off + n], c_ref[...].astype(BF16), preferred_element_type=F32)
            pv = part if pv is None else pv + part
            off += n
        acc_ref[...] = corr * acc_ref[...] + pv
        m_ref[...] = m_new

    s_all = jnp.concatenate([scores(c, k, True) for c, k in zip(ckv_refs, kr_refs)], axis=-1)
    update(s_all, ckv_refs)

    @pl.when(g == pl.num_programs(1) - 1)
    def _():
        rows = q.shape[0]
        npad = cn_ref.shape[0]
        tq = lax.broadcasted_iota(jnp.int32, (rows, npad), 0) % n_new
        kj = lax.broadcasted_iota(jnp.int32, (rows, npad), 1)
        update(jnp.where(kj <= tq, scores(cn_ref, kn_ref, False), NEG_INF), [cn_ref])
        o_ref[...] = acc_ref[...] / l_ref[...]


def _paged_attention(page_table, qs, ckv_new, kr_new, cache_ckv, cache_krope, *, pg, n_new):
    Bd, R, QW = qs.shape
    kvlora = cache_ckv.shape[2]
    page = cache_ckv.shape[1]
    n_pages = page_table.shape[1]
    pg = min(pg, n_pages)
    assert n_pages % pg == 0
    npad = ckv_new.shape[1]
    krope_t = jnp.swapaxes(cache_krope, 1, 2)
    in_specs = [pl.BlockSpec((None, R, QW), lambda b, g, pt: (b, 0, 0)),
                pl.BlockSpec((None, npad, kvlora), lambda b, g, pt: (b, 0, 0)),
                pl.BlockSpec((None, npad, ROPE_DIM), lambda b, g, pt: (b, 0, 0)),
                pl.BlockSpec(memory_space=pl.ANY),
                pl.BlockSpec(memory_space=pl.ANY)]
    grid_spec = pltpu.PrefetchScalarGridSpec(
        num_scalar_prefetch=1, grid=(Bd, n_pages // pg), in_specs=in_specs,
        out_specs=pl.BlockSpec((None, R, kvlora), lambda b, g, pt: (b, 0, 0)),
        scratch_shapes=[pltpu.VMEM((2, pg, page, kvlora), F32), pltpu.VMEM((2, pg, ROPE_DIM, page), F32),
                        pltpu.SemaphoreType.DMA((2,)), pltpu.SemaphoreType.DMA((2,)),
                        pltpu.VMEM((R, 1), F32), pltpu.VMEM((R, 1), F32), pltpu.VMEM((R, kvlora), F32)])
    return pl.pallas_call(
        functools.partial(_paged_kernel, pg=pg, n_new=n_new, kvlora=kvlora),
        grid_spec=grid_spec,
        out_shape=jax.ShapeDtypeStruct((Bd, R, kvlora), F32),
        compiler_params=_cparams("arbitrary", "arbitrary"),
        name="mla_paged",
    )(page_table, qs, ckv_new, kr_new, cache_ckv, krope_t)


def _merge_rows(x, y, bo, g, ga, gb, om, lnw_ref, lnb_ref, wor_ref, wom_ref, nf_ref,
                rw_ref, rb_ref, *, n_experts, n_heads):
    inv_n = 1.0 / RWKV_HEAD_DIM
    mean = _head_sum(y, n_heads) * inv_n
    yc = y - mean
    var = _head_sum(yc * yc, n_heads) * inv_n
    yn = yc * lax.rsqrt(var + GN_EPS) * lnw_ref[...] + lnb_ref[...]
    o_rwkv = (yn + bo) * g
    h = (x + jnp.dot((jax.nn.sigmoid(ga) * o_rwkv).astype(BF16), wor_ref[...], preferred_element_type=F32)
         + jnp.dot((jax.nn.sigmoid(gb) * om).astype(BF16), wom_ref[...], preferred_element_type=F32))
    hn = _rms(h, nf_ref[...])

    rw = rw_ref[...]
    w_hi = rw.astype(BF16)
    w_r1 = rw - w_hi.astype(F32)
    w_mid = w_r1.astype(BF16)
    x_hi, x_mid, _ = _split3(hn)
    d = lambda a, b: jnp.dot(a, b, preferred_element_type=F32)
    logits = d(x_hi, w_hi) + (d(x_hi, w_mid) + d(x_mid, w_hi)) + rb_ref[...]
    lane = lax.broadcasted_iota(jnp.int32, logits.shape, 1)
    work = jnp.where(lane < n_experts, logits, -jnp.inf)
    vals, idxs = [], []
    for _ in range(TOP_K):
        mx = jnp.max(work, axis=-1, keepdims=True)
        ix = jnp.min(jnp.where(work == mx, lane, LANES), axis=-1, keepdims=True)
        vals.append(mx)
        idxs.append(ix)
        work = jnp.where(lane == ix, -jnp.inf, work)
    ex = [jnp.exp(vv - vals[0]) for vv in vals]
    den = ex[0] + ex[1] + ex[2] + ex[3]
    ti = jnp.zeros(logits.shape, jnp.int32)
    tg = jnp.zeros(logits.shape, F32)
    for j in range(TOP_K):
        ti = jnp.where(lane == j, idxs[j], ti)
        tg = jnp.where(lane == j, ex[j] / den, tg)
    return h, hn.astype(BF16), ti, tg


def _merge_kernel(x_ref, y_ref, bo_ref, g_ref, ga_ref, gb_ref, om_ref, *refs, n_experts, n_heads,
                  vsplit):
    consts, (h_ref, hn_ref, ti_ref, tg_ref) = refs[:-4], refs[-4:]
    n_seq, tm, D = x_ref.shape
    for b in range(n_seq):
        if vsplit:
            vr = D // n_heads // vsplit
            pieces = [None] * (vr * vsplit)
            for vq in range(vr):
                rows = y_ref[vq * tm:(vq + 1) * tm, :]
                for vs in range(vsplit):
                    lo = (b * vsplit + vs) * n_heads
                    pieces[vs * vr + vq] = rows[:, lo:lo + n_heads]
            y = jnp.concatenate(pieces, axis=-1)
        else:
            y = y_ref[b]
        h, hn, ti, tg = _merge_rows(x_ref[b], y, bo_ref[b], g_ref[b], ga_ref[b], gb_ref[b], om_ref[b],
                                    *consts, n_experts=n_experts, n_heads=n_heads)
        h_ref[b] = h
        hn_ref[b] = hn
        ti_ref[b] = ti
        tg_ref[b] = tg


def _merge(x, y, bonus, g, p5, o_mla, consts, *, tm, n_experts, n_heads, vsplit):
    G, R, D = x.shape
    tm = _pick_tile(R, tm, 8)
    blk = lambda c=0: pl.BlockSpec((G, tm, D), lambda i, c=c: (0, i, c))
    full = lambda a: pl.BlockSpec(a.shape, lambda i: (0,) * a.ndim)
    if vsplit:
        vr = D // n_heads // vsplit
        yspec = pl.BlockSpec((tm * vr, LANES), lambda i: (i, 0))
    else:
        yspec = blk()
    in_specs = [blk(), yspec, blk(), blk(), blk(3), blk(4), blk()] + [full(a) for a in consts]
    lane_blk = pl.BlockSpec((G, tm, LANES), lambda i: (0, i, 0))
    out_specs = [blk(), blk(), lane_blk, lane_blk]
    out_shape = [jax.ShapeDtypeStruct((G, R, D), F32), jax.ShapeDtypeStruct((G, R, D), BF16),
                 jax.ShapeDtypeStruct((G, R, LANES), jnp.int32),
                 jax.ShapeDtypeStruct((G, R, LANES), F32)]
    return pl.pallas_call(
        functools.partial(_merge_kernel, n_experts=n_experts, n_heads=n_heads, vsplit=vsplit),
        grid=(R // tm,), in_specs=in_specs, out_specs=out_specs, out_shape=out_shape,
        compiler_params=_cparams("parallel"),
        name="merge_router",
    )(x, y, bonus, g, p5, p5, o_mla, *consts)


def _w1_prep_kernel(w_ref, p_ref, g_ref, l_ref):
    perm = p_ref[...]
    half = perm.shape[0] // 2
    for c in range(w_ref.shape[1] // perm.shape[0]):
        chunk = w_ref[:, c * 2 * half:(c + 1) * 2 * half].astype(BF16)
        both = jnp.dot(chunk, perm, preferred_element_type=F32)
        g_ref[:, c * half:(c + 1) * half] = both[:, :half].astype(BF16)
        l_ref[:, c * half:(c + 1) * half] = both[:, half:].astype(BF16)


def _w1_prep(w1, *, tr, tc):
    E, D, F2 = w1.shape
    pw = 2 * LANES
    tr = _pick_tile(D, tr, 8)
    tc = _pick_tile(F2, tc, pw)
    j = np.arange(LANES)
    perm = np.zeros((pw, pw), np.float32)
    perm[2 * j, j] = 1.0
    perm[2 * j + 1, LANES + j] = 1.0
    out = jax.ShapeDtypeStruct((E, D, F2 // 2), BF16)
    return pl.pallas_call(
        _w1_prep_kernel,
        grid=(E, D // tr, F2 // tc),
        in_specs=[pl.BlockSpec((None, tr, tc), lambda e, i, j: (e, i, j)),
                  pl.BlockSpec((pw, pw), lambda e, i, j: (0, 0))],
        out_specs=[pl.BlockSpec((None, tr, tc // 2), lambda e, i, j: (e, i, j))] * 2,
        out_shape=[out, out],
        compiler_params=_cparams("parallel", "parallel", "parallel"),
        name="moe_w1_prep",
    )(w1, jnp.asarray(perm, BF16))


def _expert_kernel(se_ref, sh_ref, sr_ref, sm_ref, x_ref, w1g_ref, w1l_ref, b1g_ref, b1l_ref,
                   w2_ref, b2_ref, o_ref, w2b_ref):
    s = pl.program_id(0)
    mode = sm_ref[s]

    @pl.when(mode == 0)
    def _():
        o_ref[...] = jnp.zeros(o_ref.shape, o_ref.dtype)

    @pl.when(mode == 2)
    def _():
        w2b_ref[...] = w2_ref[...].astype(BF16)

    @pl.when(mode >= 1)
    def _():
        x = x_ref[...]
        hg = jnp.dot(x, w1g_ref[...], preferred_element_type=F32) + b1g_ref[...]
        hl = jnp.dot(x, w1l_ref[...], preferred_element_type=F32) + b1l_ref[...]
        hg = jnp.minimum(hg, SWIGLU_LIMIT)
        hl = jnp.clip(hl, -SWIGLU_LIMIT, SWIGLU_LIMIT)
        act = hg * jax.nn.sigmoid(SWIGLU_ALPHA * hg) * (hl + 1.0)
        y = jnp.dot(act.astype(BF16), w2b_ref[...], preferred_element_type=F32)
        first = (sh_ref[s] == 0).astype(F32)
        o_ref[...] = (y + first * b2_ref[...]).astype(o_ref.dtype)


def _experts(sched, xs, w1g, w1l, b1g, b1l, w2, b2, *, bm, n_half):
    se, sh, sr, sm = sched
    NR, D = xs.shape
    E, _, F = w1g.shape
    fh = F // n_half
    S = se.shape[0]
    in_specs = [pl.BlockSpec((bm, D), lambda s, se, sh, sr, sm: (sr[s], 0)),
                pl.BlockSpec((None, D, fh), lambda s, se, sh, sr, sm: (se[s], 0, sh[s])),
                pl.BlockSpec((None, D, fh), lambda s, se, sh, sr, sm: (se[s], 0, sh[s])),
                pl.BlockSpec((None, 1, fh), lambda s, se, sh, sr, sm: (se[s], 0, sh[s])),
                pl.BlockSpec((None, 1, fh), lambda s, se, sh, sr, sm: (se[s], 0, sh[s])),
                pl.BlockSpec((None, fh, D), lambda s, se, sh, sr, sm: (se[s], sh[s], 0)),
                pl.BlockSpec((None, 1, D), lambda s, se, sh, sr, sm: (se[s], 0, 0))]
    grid_spec = pltpu.PrefetchScalarGridSpec(
        num_scalar_prefetch=4, grid=(S,), in_specs=in_specs,
        out_specs=pl.BlockSpec((bm, D), lambda s, se, sh, sr, sm: (sr[s], sh[s])),
        scratch_shapes=[pltpu.VMEM((fh, D), BF16)])
    return pl.pallas_call(
        _expert_kernel, grid_spec=grid_spec,
        out_shape=jax.ShapeDtypeStruct((NR, n_half * D), BF16),
        compiler_params=_cparams("arbitrary"),
        name="moe_experts",
    )(se, sh, sr, sm, xs, w1g, w1l, b1g, b1l, w2, b2)


def _combine_kernel(h_ref, yg_ref, tg_ref, nw_ref, o_ref, *, n_half):
    acc = h_ref[...]
    tg = tg_ref[...]
    D = acc.shape[1]
    for j in range(TOP_K):
        rows = yg_ref[j, :, :D].astype(F32)
        for hh in range(1, n_half):
            rows = rows + yg_ref[j, :, hh * D:(hh + 1) * D].astype(F32)
        acc = acc + rows * tg[:, j:j + 1]
    o_ref[...] = _rms(acc, nw_ref[...])


def _combine(h, yg, tg, nw, *, tm, n_half):
    M, D = h.shape
    tm = _pick_tile(M, tm, 8)
    return pl.pallas_call(
        functools.partial(_combine_kernel, n_half=n_half),
        grid=(M // tm,),
        in_specs=[pl.BlockSpec((tm, D), lambda i: (i, 0)),
                  pl.BlockSpec((TOP_K, tm, n_half * D), lambda i: (0, i, 0)),
                  pl.BlockSpec((tm, LANES), lambda i: (i, 0)),
                  pl.BlockSpec((1, D), lambda i: (0, 0))],
        out_specs=pl.BlockSpec((tm, D), lambda i: (i, 0)),
        out_shape=jax.ShapeDtypeStruct((M, D), F32),
        compiler_params=_cparams("parallel"),
        name="moe_combine",
    )(h, yg, tg, nw)


def _rope_tables(positions):
    inv = ROPE_THETA ** (-jnp.arange(0, ROPE_DIM, 2, dtype=F32) / ROPE_DIM)
    ang = positions.astype(F32)[:, None] * inv[None, :]
    cos, sin = jnp.cos(ang), jnp.sin(ang)
    cos2 = jnp.concatenate([cos, cos], axis=-1)
    sin2 = jnp.concatenate([sin, sin], axis=-1)
    n = positions.shape[0]
    ones = jnp.ones((n, NOPE_DIM), F32)
    z64 = jnp.zeros((n, QK_PAD - NOPE_DIM - ROPE_DIM), F32)
    z128 = jnp.zeros((n, NOPE_DIM), F32)
    qtab = jnp.concatenate([ones, cos2, z64, z128, sin2, z64], axis=-1)
    ktab = jnp.concatenate([cos2, sin2], axis=-1)
    return qtab, ktab


def _rot_cols(w):
    half = ROPE_DIM // 2
    return jnp.concatenate([-w[..., half:], w[..., :half]], axis=-1)


def _moe_schedule(top_idx, n_experts, bm, n_half, n_blocks):
    M = top_idx.shape[0]
    flat_e = top_idx.reshape(-1)
    onehot = (flat_e[:, None] == jnp.arange(n_experts, dtype=jnp.int32)[None, :]).astype(jnp.int32)
    rank = jnp.take_along_axis(jnp.cumsum(onehot, axis=0) - onehot, flat_e[:, None], axis=1)[:, 0]
    counts = jnp.sum(onehot, axis=0)
    nb = (counts + bm - 1) // bm
    blk_end = jnp.cumsum(nb)
    blk_start = blk_end - nb
    dest = blk_start[flat_e] * bm + rank
    flat_tok = jnp.repeat(jnp.arange(M, dtype=jnp.int32), TOP_K)
    row_tok = jnp.zeros((n_blocks * bm,), jnp.int32).at[dest].set(flat_tok)
    n_real = blk_end[-1]
    steps_end = jnp.cumsum(nb * n_half)
    s = jnp.arange(n_blocks * n_half, dtype=jnp.int32)
    real = s < n_real * n_half
    e_of = jnp.minimum(jnp.sum(steps_end[None, :] <= s[:, None], axis=1), n_experts - 1).astype(jnp.int32)
    local = s - (steps_end - nb * n_half)[e_of]
    nb_e = jnp.maximum(nb[e_of], 1)
    h_real = local // nb_e
    r_real = blk_start[e_of] + local % nb_e
    last_e = e_of[jnp.maximum(n_real * n_half - 1, 0)]
    tail = s - n_real * n_half
    n_tail = jnp.maximum(n_blocks - n_real, 1)
    se = jnp.where(real, e_of, last_e).astype(jnp.int32)
    sh = jnp.where(real, h_real, n_half - 1 - tail // n_tail).astype(jnp.int32)
    sr = jnp.where(real, r_real, n_real + tail % n_tail).astype(jnp.int32)
    sm = jnp.where(real, jnp.where(local % nb_e == 0, 2, 1), 0).astype(jnp.int32)
    return dest, row_tok, (se, sh, sr, sm)


def kernel(x_prompt, x_sample, cache_ckv, cache_krope, state_wkv, state_shift, page_table, norm_mix_w, w_in, rwkv_mu, rwkv_w0, rwkv_w2, rwkv_a0, rwkv_a2, rwkv_g2, rwkv_k_k, rwkv_k_a, rwkv_r_k, rwkv_ln_w, rwkv_ln_b, mla_q_norm_w, mla_w_uq, mla_kv_norm_w, mla_w_uk, mla_w_uv, w_out, norm_ffn_w, router_w, router_b, expert_w1, expert_b1, expert_w2, expert_b2, norm_final_w):
    B, T, D = x_prompt.shape
    Bd, Td, _ = x_sample.shape
    H = D // RWKV_HEAD_DIM
    HD = RWKV_HEAD_DIM
    dlora, alora, glora = rwkv_w2.shape[0], rwkv_a2.shape[0], rwkv_g2.shape[0]
    qlora, Hm, _ = mla_w_uq.shape
    kvlora = mla_w_uk.shape[0]
    n_experts = router_w.shape[1]
    F = expert_w2.shape[1]
    page = cache_ckv.shape[1]
    past = page_table.shape[1] * page
    scale = float((NOPE_DIM + ROPE_DIM) ** -0.5)
    row2 = lambda a: a.reshape(1, -1).astype(F32)

    sizes = [D, D, D, dlora, alora, glora, qlora, kvlora, ROPE_DIM, D, D]
    offs = np.concatenate([[0], np.cumsum(sizes)]).tolist()
    col = lambda i: w_in[:, offs[i]:offs[i + 1]]
    lw_pad, la_pad = _round_up(dlora, LANES), _round_up(alora, LANES)
    padc = lambda a, n: jnp.pad(a, ((0, 0), (0, n - a.shape[1])))
    perm = np.arange(D).reshape(H, HD).T.reshape(-1)
    pc = lambda a: a[..., perm]
    w5 = jnp.concatenate([pc(col(0)), pc(col(1)), pc(col(2)), pc(col(9)), col(10)], axis=1).astype(BF16)
    lora_w = lw_pad + la_pad + glora
    ws = jnp.concatenate([padc(col(3), lw_pad), padc(col(4), la_pad), col(5), col(6), col(7),
                          col(8), _rot_cols(col(8))], axis=1)
    col_cq, col_ckv, col_kr = lora_w, lora_w + qlora, lora_w + qlora + kvlora
    ws = padc(ws, _round_up(ws.shape[1], LANES)).astype(BF16)
    mu = rwkv_mu
    mur, muk, muv = row2(pc(mu[:D])), row2(pc(mu[D:2 * D])), row2(pc(mu[2 * D:3 * D]))
    m0 = 3 * D
    mul = jnp.concatenate([jnp.pad(mu[m0:m0 + dlora], (0, lw_pad - dlora)),
                           jnp.pad(mu[m0 + dlora:m0 + dlora + alora], (0, la_pad - alora)),
                           mu[m0 + dlora + alora:]]).reshape(1, -1)
    w2p = jnp.pad(pc(rwkv_w2), ((0, lw_pad - dlora), (0, 0))).astype(BF16)
    a2p = jnp.pad(pc(rwkv_a2), ((0, la_pad - alora), (0, 0))).astype(BF16)
    prep_consts = (mur, muk, muv, mul, row2(pc(rwkv_w0)), w2p, row2(pc(rwkv_a0)), a2p,
                   pc(rwkv_g2).astype(BF16), row2(pc(rwkv_k_k)), row2(pc(rwkv_k_a)),
                   row2(pc(rwkv_r_k.reshape(-1))), lw_pad, la_pad)

    zq = jnp.zeros((qlora, Hm, QK_PAD - NOPE_DIM - ROPE_DIM), F32)
    q_nope_w, q_rope_w = mla_w_uq[..., :NOPE_DIM], mla_w_uq[..., NOPE_DIM:]
    wqa = jnp.concatenate([q_nope_w, q_rope_w, zq], axis=-1).reshape(qlora, Hm * QK_PAD).astype(BF16)
    wqb = jnp.concatenate([jnp.zeros_like(q_nope_w), _rot_cols(q_rope_w), zq],
                          axis=-1).reshape(qlora, Hm * QK_PAD).astype(BF16)
    zk = jnp.zeros((kvlora, Hm, QK_PAD - NOPE_DIM), F32)
    wk_top = jnp.concatenate([mla_w_uk, zk], axis=-1).reshape(kvlora, Hm * QK_PAD)
    sel = jnp.concatenate([jnp.zeros((ROPE_DIM, NOPE_DIM), F32), jnp.eye(ROPE_DIM, dtype=F32),
                           jnp.zeros((ROPE_DIM, QK_PAD - NOPE_DIM - ROPE_DIM), F32)], axis=-1)
    wk = jnp.concatenate([wk_top, jnp.tile(sel, (1, Hm))], axis=0).astype(BF16)
    wv = mla_w_uv.reshape(kvlora, Hm * V_DIM).astype(BF16)
    w_uk_t = jnp.transpose(mla_w_uk, (1, 2, 0)).astype(BF16)
    w_uv_h = jnp.transpose(mla_w_uv, (1, 0, 2)).astype(BF16)
    wo_mla = w_out.astype(BF16)
    wo_rwkv = w_out[perm, :].astype(BF16)
    rw = jnp.pad(router_w, ((0, 0), (0, LANES - n_experts)))
    rb = jnp.pad(router_b, (0, LANES - n_experts)).reshape(1, LANES)
    w1g, w1l = _w1_prep(expert_w1, tr=512, tc=2048)
    b1g = expert_b1[:, None, 0::2]
    b1l = expert_b1[:, None, 1::2]
    w2e = expert_w2
    b2e = expert_b2[:, None, :]
    nmw = row2(norm_mix_w)

    def project_stage(x2d, positions, tm_mla):
        p5 = _project(x2d, nmw, w5, normalize=True, tm=1024, tn=512)
        ps = _project(x2d, nmw, ws, normalize=True, tm=1024, tn=ws.shape[1])
        qtab, ktab = _rope_tables(positions)
        mla = _mla_prep(ps, qtab, ktab, row2(mla_q_norm_w), row2(mla_kv_norm_w), wqa, wqb, wk, wv,
                        col_cq=col_cq, col_ckv=col_ckv, col_kr=col_kr, qlora=qlora, kvlora=kvlora,
                        n_heads=Hm, hg=min(4, Hm), tm=tm_mla, scale=scale)
        return p5, ps, mla

    merge_consts = (row2(pc(rwkv_ln_w)), row2(pc(rwkv_ln_b)), wo_rwkv, wo_mla,
                    row2(norm_ffn_w), rw, rb)

    xp = x_prompt.reshape(B * T, D)
    p5_p, ps_p, mla_p = project_stage(xp, jnp.arange(T), min(512, T))
    vsplit = max(1, LANES // (B * H))
    assert HD % vsplit == 0 and B * H * vsplit <= LANES
    vr = HD // vsplit
    tb_p = min(64, T)
    zr, zw, zk, zv, za, zb, g_p, bo_p = _rwkv_prep_packed(
        p5_p.reshape(B, T, -1), ps_p.reshape(B, T, -1), prep_consts, B=B, T=T, tt=tb_p, D=D,
        lora_w=lora_w, n_heads=H, vsplit=vsplit)
    s0_p = jnp.zeros((HD, vr, LANES), F32)
    y_pk, s_kl = _rwkv_scan(zr, zw, zk, zv, za, zb, s0_p, tb=tb_p, packed=True)
    nl_p = B * vsplit * H
    wkv_p = s_kl[:, :, :nl_p].reshape(HD, vr, B, vsplit, H).transpose(2, 4, 3, 1, 0).reshape(B, H, HD, HD)

    q_p, kf_p, vf_p, ckv_p, kr_p = mla_p
    o_mla_p = _flash_attention(q_p, kf_p, vf_p, B=B, T=T, n_heads=Hm, tq=1024, tk=512)
    h_p, hn_p, ti_p, tg_p = _merge(
        x_prompt, y_pk, bo_p, g_p, p5_p.reshape(B, T, -1),
        o_mla_p.reshape(B, T, -1), merge_consts, tm=tb_p, n_experts=n_experts, n_heads=H, vsplit=vsplit)

    xs_tm = jnp.swapaxes(x_sample, 0, 1).reshape(Td * Bd, D)
    pos_s = jnp.repeat(past + jnp.arange(Td), Bd)
    p5_s, ps_s, mla_s = project_stage(xs_tm, pos_s, Td * Bd)
    s5 = _project(state_shift, nmw, w5, normalize=False, tm=1024, tn=512).reshape(1, Bd, -1)
    ss = _project(state_shift, nmw, ws, normalize=False, tm=1024, tn=ws.shape[1]).reshape(1, Bd, -1)
    r_s, lw_s, k_s, v_s, al_s, be_s, g_s, bo_s = _rwkv_prep(
        p5_s, ps_s, s5, ss, prep_consts, n_groups=1, n_tiles=Td, tt=Bd, shift=Bd, D=D, lora_w=lora_w,
        n_heads=H)

    def to_kl_s(a):
        return a.reshape(Td, Bd, HD, H).transpose(0, 2, 1, 3).reshape(Td, HD, Bd * H)

    nl_s = Bd * H
    pad_s = _round_up(nl_s, LANES) - nl_s
    pads = lambda a: jnp.pad(a, ((0, 0), (0, 0), (0, pad_s))) if pad_s else a
    s0_s = pads(state_wkv.transpose(3, 2, 0, 1).reshape(HD, HD, nl_s))
    y_sl, s_sl = _rwkv_scan(pads(to_kl_s(r_s)), pads(to_kl_s(lw_s)), pads(to_kl_s(k_s)),
                            pads(to_kl_s(v_s)), pads(to_kl_s(al_s)), pads(to_kl_s(be_s)), s0_s, tb=Td)
    y_s = y_sl[:, :, :nl_s].reshape(Td, HD, Bd, H).transpose(0, 2, 1, 3).reshape(Td * Bd, D)
    wkv_s = s_sl[:, :, :nl_s].reshape(HD, HD, Bd, H).transpose(2, 3, 1, 0)

    q_s, _, _, ckv_s, kr_s = mla_s
    q4 = q_s.reshape(Td, Bd, Hm, QK_PAD)
    qn_h = q4[..., :NOPE_DIM].transpose(2, 1, 0, 3).reshape(Hm, Bd * Td, NOPE_DIM)
    q_lat = _head_matmul(qn_h, w_uk_t)
    q_lat = q_lat.reshape(Hm, Bd, Td, kvlora).transpose(1, 0, 2, 3).reshape(Bd, Hm * Td, kvlora)
    q_rp = q4[..., NOPE_DIM:NOPE_DIM + ROPE_DIM].transpose(1, 2, 0, 3).reshape(Bd, Hm * Td, ROPE_DIM)
    qs = jnp.concatenate([q_lat.astype(BF16), q_rp], axis=-1)
    ckv_s_bt = ckv_s.reshape(Td, Bd, kvlora).transpose(1, 0, 2)
    kr_s_bt = kr_s.reshape(Td, Bd, ROPE_DIM).transpose(1, 0, 2)
    npad = _round_up(Td, 8)
    padn = lambda a: jnp.pad(a, ((0, 0), (0, npad - Td), (0, 0)))
    o_lat = _paged_attention(page_table, qs, padn(ckv_s_bt), padn(kr_s_bt), cache_ckv, cache_krope,
                             pg=16, n_new=Td)
    o_lat_h = o_lat.reshape(Bd, Hm, Td, kvlora).transpose(1, 2, 0, 3).reshape(Hm, Td * Bd, kvlora)
    o_mla_s = _head_matmul(o_lat_h, w_uv_h)
    o_mla_s = o_mla_s.transpose(1, 0, 2).reshape(Td * Bd, Hm * V_DIM)

    g1 = lambda a: a[None]
    h_s, hn_s, ti_s, tg_s = _merge(g1(xs_tm), g1(y_s), g1(bo_s), g1(g_s), g1(p5_s), g1(o_mla_s),
                                   merge_consts, tm=128, n_experts=n_experts, n_heads=H, vsplit=0)
    flat = lambda a: a.reshape(-1, a.shape[-1])
    h_all = jnp.concatenate([flat(h_p), flat(h_s)], axis=0)
    hn_all = jnp.concatenate([flat(hn_p), flat(hn_s)], axis=0)
    ti_all = jnp.concatenate([flat(ti_p), flat(ti_s)], axis=0)[:, :TOP_K]
    tg_all = jnp.concatenate([flat(tg_p), flat(tg_s)], axis=0)
    M = h_all.shape[0]
    bm = 256
    n_half = 2 if F % (2 * LANES) == 0 else 1
    n_blocks = (M * TOP_K + n_experts * (bm - 1)) // bm
    dest, row_tok, sched = _moe_schedule(ti_all, n_experts, bm, n_half, n_blocks)
    xs_rows = hn_all[row_tok]
    y_halves = _experts(sched, xs_rows, w1g, w1l, b1g, b1l, w2e, b2e, bm=bm, n_half=n_half)
    dest_slot_major = dest.reshape(M, TOP_K).T.reshape(-1)
    yg = y_halves[dest_slot_major].reshape(TOP_K, M, n_half * D)
    y_all = _combine(h_all, yg, tg_all, row2(norm_final_w), tm=128, n_half=n_half)

    y_prompt = y_all[:B * T].reshape(B, T, D)
    y_sample = y_all[B * T:].reshape(Td, Bd, D).transpose(1, 0, 2)
    last = jnp.concatenate([x_prompt[:, -1], x_sample[:, -1]], axis=0)
    shift_out = _rmsnorm_rows(last, nmw)
    return (y_prompt, y_sample,
            ckv_p.reshape(B, T // page, page, kvlora), kr_p.reshape(B, T // page, page, ROPE_DIM),
            wkv_p, shift_out[:B],
            ckv_s_bt, kr_s_bt, wkv_s, shift_out[B:])
```

```python
import functools

import numpy as np
import jax
import jax.numpy as jnp
from jax import lax
from jax.experimental import pallas as pl
from jax.experimental.pallas import tpu as pltpu

F32 = jnp.float32
BF16 = jnp.bfloat16

NORM_EPS = 1e-6
GN_EPS = 64e-5
RWKV_HEAD_DIM = 64
NOPE_DIM = 128
ROPE_DIM = 64
V_DIM = 128
QK_PAD = 256
ROPE_THETA = 10000.0
NEG_INF = -1e30
TOP_K = 4
SWIGLU_ALPHA = 1.702
SWIGLU_LIMIT = 7.0
LANES = 128
VMEM_LIMIT = 56 * 1024 * 1024


def _round_up(n, m):
    return (n + m - 1) // m * m


def _pick_tile(n, target, mult):
    if n <= target:
        return n
    t = target // mult * mult
    while n % t:
        t -= mult
    return t


def _cparams(*sem):
    return pltpu.CompilerParams(dimension_semantics=sem, vmem_limit_bytes=VMEM_LIMIT)


def _split3(x):
    hi = x.astype(BF16)
    r1 = x - hi.astype(F32)
    mid = r1.astype(BF16)
    lo = (r1 - mid.astype(F32)).astype(BF16)
    return hi, mid, lo


def _tree_sum(xs):
    xs = list(xs)
    while len(xs) > 1:
        xs = [xs[i] + xs[i + 1] for i in range(0, len(xs) - 1, 2)] + ([xs[-1]] if len(xs) % 2 else [])
    return xs[0]


def _rms(x, w):
    return x * lax.rsqrt(jnp.mean(x * x, axis=-1, keepdims=True) + NORM_EPS) * w


def _proj_kernel(x_ref, nw_ref, w_ref, o_ref, xn_ref, *, normalize):
    @pl.when(pl.program_id(1) == 0)
    def _():
        x = x_ref[...]
        if normalize:
            x = _rms(x, nw_ref[...])
        xn_ref[...] = x.astype(BF16)

    o_ref[...] = jnp.dot(xn_ref[...], w_ref[...], preferred_element_type=F32)


def _project(x, norm_w, w, *, normalize, tm, tn):
    M, D = x.shape
    N = w.shape[1]
    tm = _pick_tile(M, tm, 8)
    tn = _pick_tile(N, tn, LANES)
    return pl.pallas_call(
        functools.partial(_proj_kernel, normalize=normalize),
        grid=(M // tm, N // tn),
        in_specs=[pl.BlockSpec((tm, D), lambda i, j: (i, 0)),
                  pl.BlockSpec((1, D), lambda i, j: (0, 0)),
                  pl.BlockSpec((D, tn), lambda i, j: (0, j))],
        out_specs=pl.BlockSpec((tm, tn), lambda i, j: (i, j)),
        out_shape=jax.ShapeDtypeStruct((M, N), F32),
        scratch_shapes=[pltpu.VMEM((tm, D), BF16)],
        compiler_params=_cparams("parallel", "arbitrary"),
        name="norm_proj",
    )(x, norm_w, w)


def _rmsnorm_rows_kernel(x_ref, w_ref, o_ref):
    o_ref[...] = _rms(x_ref[...], w_ref[...])


def _rmsnorm_rows(x, w):
    return pl.pallas_call(
        _rmsnorm_rows_kernel,
        out_shape=jax.ShapeDtypeStruct(x.shape, F32),
        name="rmsnorm_rows",
    )(x, w)


def _head_sum(x, n_heads):
    d = x.shape[1]
    assert d % LANES == 0 and LANES % n_heads == 0
    s = _tree_sum([x[:, j * LANES:(j + 1) * LANES] for j in range(d // LANES)])
    shift = LANES // 2
    while shift >= n_heads:
        s = s + pltpu.roll(s, shift, axis=1)
        shift //= 2
    return jnp.concatenate([s] * (d // LANES), axis=-1)


def _rwkv_prep_math(r, k, v, lo, w0_ref, w2_ref, a0_ref, a2_ref, g2_ref, kk_ref, ka_ref, rk_ref,
                    *, lw_pad, la_pad, n_heads):
    hw = lo[:, :lw_pad]
    ha = lo[:, lw_pad:lw_pad + la_pad]
    hg = lo[:, lw_pad + la_pad:]
    mm = lambda a, b_ref: jnp.dot(a.astype(BF16), b_ref[...], preferred_element_type=F32)
    w = -jax.nn.softplus(-(w0_ref[...] + mm(jnp.tanh(hw), w2_ref))) - 0.5
    a = jax.nn.sigmoid(a0_ref[...] + mm(ha, a2_ref))
    g = mm(jax.nn.sigmoid(hg), g2_ref)
    kk = k * kk_ref[...]
    kk = kk * lax.rsqrt(jnp.maximum(_head_sum(kk * kk, n_heads), 1e-24))
    k2 = k * (1.0 + (a - 1.0) * ka_ref[...])
    bsum = _head_sum(r * k2 * rk_ref[...], n_heads)
    return r, jnp.exp(-jnp.exp(w)), k2, v, -kk, kk * a, g, bsum * v


def _rwkv_prep_packed_kernel(r_ref, k_ref, v_ref, l_ref,
                             mur_ref, muk_ref, muv_ref, mul_ref, w0_ref, w2_ref, a0_ref, a2_ref,
                             g2_ref, kk_ref, ka_ref, rk_ref,
                             zr_ref, zw_ref, zk_ref, zv_ref, za_ref, zb_ref, go_ref, bo_ref,
                             cr_ref, ck_ref, cv_ref, cl_ref, *, lw_pad, la_pad, n_heads, vsplit):
    t = pl.program_id(0)
    n_seq, tt, D = r_ref.shape
    nk = D // n_heads
    vr = nk // vsplit

    @pl.when(t == 0)
    def _():
        for c_ref in (cr_ref, ck_ref, cv_ref, cl_ref):
            c_ref[...] = jnp.zeros(c_ref.shape, F32)

    def lerp(p, c_ref, b, mu_ref):
        rolled = pltpu.roll(p, 1, axis=0)
        row = lax.broadcasted_iota(jnp.int32, p.shape, 0)
        prev = jnp.where(row == 0, c_ref[b], rolled)
        c_ref[b] = p[tt - 1:, :]
        return p + (prev - p) * mu_ref[...]

    outs = []
    for b in range(n_seq):
        r = lerp(r_ref[b], cr_ref, b, mur_ref)
        k = lerp(k_ref[b], ck_ref, b, muk_ref)
        v = lerp(v_ref[b], cv_ref, b, muv_ref)
        lo = lerp(l_ref[b], cl_ref, b, mul_ref)
        res = _rwkv_prep_math(r, k, v, lo, w0_ref, w2_ref, a0_ref, a2_ref, g2_ref, kk_ref, ka_ref,
                              rk_ref, lw_pad=lw_pad, la_pad=la_pad, n_heads=n_heads)
        go_ref[b] = res[6]
        bo_ref[b] = res[7]
        outs.append(res)

    used = n_seq * vsplit * n_heads
    zpad = [jnp.zeros((tt, LANES - used), F32)] if used < LANES else []
    piece = lambda x, j: x[:, j * n_heads:(j + 1) * n_heads]
    for z_ref, idx in ((zr_ref, 0), (zw_ref, 1), (zk_ref, 2), (za_ref, 4), (zb_ref, 5)):
        for kq in range(nk):
            parts = [piece(outs[b][idx], kq) for b in range(n_seq) for _ in range(vsplit)]
            z_ref[kq * tt:(kq + 1) * tt, :] = jnp.concatenate(parts + zpad, axis=-1)
    for vq in range(vr):
        parts = [piece(outs[b][3], vs * vr + vq) for b in range(n_seq) for vs in range(vsplit)]
        zv_ref[vq * tt:(vq + 1) * tt, :] = jnp.concatenate(parts + zpad, axis=-1)


def _rwkv_prep_packed(p5, ps, consts, *, B, T, tt, D, lora_w, n_heads, vsplit):
    (mur, muk, muv, mul, w0, w2p, a0, a2p, g2, k_k, k_a, r_k, lw_pad, la_pad) = consts
    nk = D // n_heads
    vr = nk // vsplit
    n_tiles = T // tt
    big = lambda c: pl.BlockSpec((B, tt, D), lambda t, c=c: (0, t, c))
    full = lambda a: pl.BlockSpec(a.shape, lambda t: (0,) * a.ndim)
    in_specs = [big(0), big(1), big(2), pl.BlockSpec((B, tt, lora_w), lambda t: (0, t, 0)),
                full(mur), full(muk), full(muv), full(mul), full(w0), full(w2p), full(a0),
                full(a2p), full(g2), full(k_k), full(k_a), full(r_k)]
    zspec = pl.BlockSpec((nk * tt, LANES), lambda t: (t, 0))
    vspec = pl.BlockSpec((vr * tt, LANES), lambda t: (t, 0))
    tok = pl.BlockSpec((B, tt, D), lambda t: (0, t, 0))
    zshape = jax.ShapeDtypeStruct((n_tiles * nk * tt, LANES), F32)
    vshape = jax.ShapeDtypeStruct((n_tiles * vr * tt, LANES), F32)
    tshape = jax.ShapeDtypeStruct((B, T, D), F32)
    return pl.pallas_call(
        functools.partial(_rwkv_prep_packed_kernel, lw_pad=lw_pad, la_pad=la_pad, n_heads=n_heads,
                          vsplit=vsplit),
        grid=(n_tiles,),
        in_specs=in_specs,
        out_specs=[zspec, zspec, zspec, vspec, zspec, zspec, tok, tok],
        out_shape=[zshape, zshape, zshape, vshape, zshape, zshape, tshape, tshape],
        scratch_shapes=[pltpu.VMEM((B, 1, D), F32)] * 3 + [pltpu.VMEM((B, 1, lora_w), F32)],
        compiler_params=_cparams("arbitrary"),
        name="rwkv_prep_packed",
    )(p5, p5, p5, ps, mur, muk, muv, mul, w0, w2p, a0, a2p, g2, k_k, k_a, r_k)


def _rwkv_prep_kernel(r_ref, k_ref, v_ref, l_ref, sr_ref, sk_ref, sv_ref, sl_ref,
                      mur_ref, muk_ref, muv_ref, mul_ref, w0_ref, w2_ref, a0_ref, a2_ref, g2_ref,
                      kk_ref, ka_ref, rk_ref,
                      ro_ref, lw_ref, ko_ref, vo_ref, al_ref, be_ref, go_ref, bo_ref,
                      cr_ref, ck_ref, cv_ref, cl_ref, *, shift, lw_pad, la_pad, n_heads):
    t = pl.program_id(1)

    @pl.when(t == 0)
    def _():
        cr_ref[...] = sr_ref[...]
        ck_ref[...] = sk_ref[...]
        cv_ref[...] = sv_ref[...]
        cl_ref[...] = sl_ref[...]

    def lerp(p_ref, c_ref, mu_ref):
        p = p_ref[...]
        if shift == 1:
            rolled = pltpu.roll(p, 1, axis=0)
            row = lax.broadcasted_iota(jnp.int32, p.shape, 0)
            prev = jnp.where(row == 0, c_ref[...], rolled)
            c_new = p[p.shape[0] - 1:, :]
        else:
            prev = c_ref[...]
            c_new = p
        xx = p + (prev - p) * mu_ref[...]
        c_ref[...] = c_new
        return xx

    r = lerp(r_ref, cr_ref, mur_ref)
    k = lerp(k_ref, ck_ref, muk_ref)
    v = lerp(v_ref, cv_ref, muv_ref)
    lo = lerp(l_ref, cl_ref, mul_ref)
    r, lw, k2, v, al, be, g, bonus = _rwkv_prep_math(
        r, k, v, lo, w0_ref, w2_ref, a0_ref, a2_ref, g2_ref, kk_ref, ka_ref, rk_ref,
        lw_pad=lw_pad, la_pad=la_pad, n_heads=n_heads)
    ro_ref[...] = r
    lw_ref[...] = lw
    ko_ref[...] = k2
    vo_ref[...] = v
    al_ref[...] = al
    be_ref[...] = be
    go_ref[...] = g
    bo_ref[...] = bonus


def _rwkv_prep(p5, ps, shift5, shifts, consts, *, n_groups, n_tiles, tt, shift, D, lora_w, n_heads):
    M = p5.shape[0]
    (mur, muk, muv, mul, w0, w2p, a0, a2p, g2, k_k, k_a, r_k, lw_pad, la_pad) = consts
    row = lambda g, t: (g * n_tiles + t)
    big = lambda c: pl.BlockSpec((tt, D), lambda g, t, c=c: (row(g, t), c))
    sh = lambda c: pl.BlockSpec((None, shift, D), lambda g, t, c=c: (g, 0, c))
    full = lambda a: pl.BlockSpec(a.shape, lambda g, t: (0,) * a.ndim)
    in_specs = [big(0), big(1), big(2),
                pl.BlockSpec((tt, lora_w), lambda g, t: (row(g, t), 0)),
                sh(0), sh(1), sh(2),
                pl.BlockSpec((None, shift, lora_w), lambda g, t: (g, 0, 0)),
                full(mur), full(muk), full(muv), full(mul), full(w0), full(w2p), full(a0),
                full(a2p), full(g2), full(k_k), full(k_a), full(r_k)]
    out_spec = pl.BlockSpec((tt, D), lambda g, t: (row(g, t), 0))
    outs = pl.pallas_call(
        functools.partial(_rwkv_prep_kernel, shift=shift, lw_pad=lw_pad, la_pad=la_pad,
                          n_heads=n_heads),
        grid=(n_groups, n_tiles),
        in_specs=in_specs,
        out_specs=[out_spec] * 8,
        out_shape=[jax.ShapeDtypeStruct((M, D), F32)] * 8,
        scratch_shapes=[pltpu.VMEM((shift, D), F32)] * 3 + [pltpu.VMEM((shift, lora_w), F32)],
        compiler_params=_cparams("parallel", "arbitrary"),
        name="rwkv_prep",
    )(p5, p5, p5, ps, shift5, shift5, shift5, shifts,
      mur, muk, muv, mul, w0, w2p, a0, a2p, g2, k_k, k_a, r_k)
    return outs


def _rwkv_scan_kernel(r_ref, w_ref, k_ref, v_ref, a_ref, b_ref, s0_ref, y_ref, sf_ref, s_ref,
                      *, tb, vr, nk, packed):
    tblk = pl.program_id(1)

    @pl.when(tblk == 0)
    def _():
        s_ref[...] = s0_ref[...]

    n_acc = 8
    if packed:
        row = lambda ref, t, k: jnp.broadcast_to(ref[pl.ds(k * tb + t, 1), :], (vr, LANES))
        tile = lambda ref, t, n: ref[pl.ds(t, n, stride=tb), :]
    else:
        row = lambda ref, t, k: jnp.broadcast_to(ref[t, pl.ds(k, 1), :], (vr, LANES))
        tile = lambda ref, t, n: ref[t]

    def accumulate(parts, k, term):
        parts[k % n_acc] = term if parts[k % n_acc] is None else parts[k % n_acc] + term

    sa_parts = [None] * n_acc
    for k in range(nk):
        accumulate(sa_parts, k, s_ref[k] * row(a_ref, 0, k))
    sa0 = _tree_sum(sa_parts)

    def step(t, sa):
        vt = tile(v_ref, t, vr)
        tn = jnp.minimum(t + 1, tb - 1)
        y_parts = [None] * n_acc
        sa_parts = [None] * n_acc
        for k in range(nk):
            s = s_ref[k] * row(w_ref, t, k) + sa * row(b_ref, t, k) + vt * row(k_ref, t, k)
            s_ref[k] = s
            accumulate(y_parts, k, s * row(r_ref, t, k))
            accumulate(sa_parts, k, s * row(a_ref, tn, k))
        if packed:
            y_ref[pl.ds(t, vr, stride=tb), :] = _tree_sum(y_parts)
        else:
            y_ref[t] = _tree_sum(y_parts)
        return _tree_sum(sa_parts)

    lax.fori_loop(0, tb, step, sa0)

    @pl.when(tblk == pl.num_programs(1) - 1)
    def _():
        sf_ref[...] = s_ref[...]


def _rwkv_scan(r, w, k, v, a, b, s0, *, tb, packed=False):
    K, VR, NL = s0.shape
    if packed:
        T = r.shape[0] // K
        assert NL == LANES and T % tb == 0
        kspec = pl.BlockSpec((K * tb, LANES), lambda n, t: (t, 0))
        vspec = pl.BlockSpec((VR * tb, LANES), lambda n, t: (t, 0))
    else:
        T = r.shape[0]
        tb = min(tb, T)
        assert T % tb == 0 and NL % LANES == 0
        kspec = pl.BlockSpec((tb, K, LANES), lambda n, t: (t, 0, n))
        vspec = pl.BlockSpec((tb, VR, LANES), lambda n, t: (t, 0, n))
    vin = vspec
    sspec = pl.BlockSpec((K, VR, LANES), lambda n, t: (0, 0, n))
    return pl.pallas_call(
        functools.partial(_rwkv_scan_kernel, tb=tb, vr=VR, nk=K, packed=packed),
        grid=(NL // LANES, T // tb),
        in_specs=[kspec, kspec, kspec, vin, kspec, kspec, sspec],
        out_specs=[vspec, sspec],
        out_shape=[jax.ShapeDtypeStruct(v.shape, F32),
                   jax.ShapeDtypeStruct((K, VR, NL), F32)],
        scratch_shapes=[pltpu.VMEM((K, VR, LANES), F32)],
        compiler_params=_cparams("parallel", "arbitrary"),
        name="rwkv_scan",
    )(r, w, k, v, a, b, s0)


def _mla_prep_kernel(cq_ref, ckv_ref, kr_ref, qtab_ref, ktab_ref, qn_ref, kvn_ref,
                     wqa_ref, wqb_ref, wk_ref, wv_ref,
                     q_ref, k_ref, v_ref, ckv_o_ref, kr_o_ref, cqn_ref, ckk_ref, *, hg, scale):
    @pl.when(pl.program_id(1) == 0)
    def _():
        cqn_ref[...] = _rms(cq_ref[...], qn_ref[...]).astype(BF16)
        ckv = _rms(ckv_ref[...], kvn_ref[...])
        ckv_o_ref[...] = ckv
        kr = kr_ref[...]
        ktab = ktab_ref[...]
        krope = kr[:, :ROPE_DIM] * ktab[:, :ROPE_DIM] + kr[:, ROPE_DIM:] * ktab[:, ROPE_DIM:]
        kr_o_ref[...] = krope
        ckk_ref[...] = jnp.concatenate([ckv, krope], axis=-1).astype(BF16)

    cqn = cqn_ref[...]
    qtab = qtab_ref[...]
    cosf = jnp.concatenate([qtab[:, :QK_PAD]] * hg, axis=-1)
    sinf = jnp.concatenate([qtab[:, QK_PAD:]] * hg, axis=-1)
    qa = jnp.dot(cqn, wqa_ref[...], preferred_element_type=F32)
    qb = jnp.dot(cqn, wqb_ref[...], preferred_element_type=F32)
    q_ref[...] = ((qa * cosf + qb * sinf) * scale).astype(BF16)
    ckk = ckk_ref[...]
    k_ref[...] = jnp.dot(ckk, wk_ref[...], preferred_element_type=F32).astype(BF16)
    v_ref[...] = jnp.dot(ckk[:, :ckk.shape[1] - ROPE_DIM], wv_ref[...],
                         preferred_element_type=F32).astype(BF16)


def _mla_prep(ps, qtab, ktab, qn, kvn, wqa, wqb, wk, wv, *, col_cq, col_ckv, col_kr, qlora, kvlora,
              n_heads, hg, tm, scale):
    M = ps.shape[0]
    tm = _pick_tile(M, tm, 8)
    n_tab = qtab.shape[0] // tm
    grid = (M // tm, n_heads // hg)
    tabspec = lambda a: pl.BlockSpec((tm, a.shape[1]), lambda i, j: (i % n_tab, 0))
    full = lambda a: pl.BlockSpec(a.shape, lambda i, j: (0,) * a.ndim)
    in_specs = [pl.BlockSpec((tm, qlora), lambda i, j: (i, col_cq // qlora)),
                pl.BlockSpec((tm, kvlora), lambda i, j: (i, col_ckv // kvlora)),
                pl.BlockSpec((tm, 2 * ROPE_DIM), lambda i, j: (i, col_kr // (2 * ROPE_DIM))),
                tabspec(qtab), tabspec(ktab), full(qn), full(kvn),
                pl.BlockSpec((qlora, hg * QK_PAD), lambda i, j: (0, j)),
                pl.BlockSpec((qlora, hg * QK_PAD), lambda i, j: (0, j)),
                pl.BlockSpec((kvlora + ROPE_DIM, hg * QK_PAD), lambda i, j: (0, j)),
                pl.BlockSpec((kvlora, hg * V_DIM), lambda i, j: (0, j))]
    out_specs = [pl.BlockSpec((tm, hg * QK_PAD), lambda i, j: (i, j)),
                 pl.BlockSpec((tm, hg * QK_PAD), lambda i, j: (i, j)),
                 pl.BlockSpec((tm, hg * V_DIM), lambda i, j: (i, j)),
                 pl.BlockSpec((tm, kvlora), lambda i, j: (i, 0)),
                 pl.BlockSpec((tm, ROPE_DIM), lambda i, j: (i, 0))]
    out_shape = [jax.ShapeDtypeStruct((M, n_heads * QK_PAD), BF16),
                 jax.ShapeDtypeStruct((M, n_heads * QK_PAD), BF16),
                 jax.ShapeDtypeStruct((M, n_heads * V_DIM), BF16),
                 jax.ShapeDtypeStruct((M, kvlora), F32),
                 jax.ShapeDtypeStruct((M, ROPE_DIM), F32)]
    return pl.pallas_call(
        functools.partial(_mla_prep_kernel, hg=hg, scale=scale),
        grid=grid, in_specs=in_specs, out_specs=out_specs, out_shape=out_shape,
        scratch_shapes=[pltpu.VMEM((tm, qlora), BF16), pltpu.VMEM((tm, kvlora + ROPE_DIM), BF16)],
        compiler_params=_cparams("parallel", "arbitrary"),
        name="mla_prep",
    )(ps, ps, ps, qtab, ktab, qn, kvn, wqa, wqb, wk, wv)


def _flash_kernel(q_ref, k_ref, v_ref, o_ref, *, tq, tk):
    i = pl.program_id(2)
    q = q_ref[...]
    n_sub = tq // tk

    def block(j, carry, masked):
        m, l, acc = carry
        start = pl.multiple_of(j * tk, tk)
        kb = k_ref[pl.ds(start, tk), :]
        vb = v_ref[pl.ds(start, tk), :]
        s = lax.dot_general(q, kb, (((1,), (1,)), ((), ())), preferred_element_type=F32)
        if masked:
            qpos = i * tq + lax.broadcasted_iota(jnp.int32, s.shape, 0)
            kpos = j * tk + lax.broadcasted_iota(jnp.int32, s.shape, 1)
            s = jnp.where(kpos <= qpos, s, NEG_INF)
        m_new = jnp.maximum(m, jnp.max(s, axis=-1, keepdims=True))
        p = jnp.exp(s - m_new)
        corr = jnp.exp(m - m_new)
        l = corr * l + jnp.sum(p, axis=-1, keepdims=True)
        acc = corr * acc + jnp.dot(p.astype(BF16), vb, preferred_element_type=F32)
        return m_new, l, acc

    init = (jnp.full((tq, 1), NEG_INF, F32), jnp.zeros((tq, 1), F32),
            jnp.zeros((tq, v_ref.shape[1]), F32))
    carry = lax.fori_loop(0, i * n_sub, lambda j, c: block(j, c, False), init)
    for d in range(n_sub):
        carry = block(i * n_sub + d, carry, True)
    m, l, acc = carry
    o_ref[...] = (acc / l).astype(o_ref.dtype)


def _flash_attention(q, k, v, *, B, T, n_heads, tq, tk):
    tq = min(tq, T)
    tk = min(tk, tq)
    nq = T // tq
    return pl.pallas_call(
        functools.partial(_flash_kernel, tq=tq, tk=tk),
        grid=(B, n_heads, nq),
        in_specs=[pl.BlockSpec((tq, QK_PAD), lambda b, h, i: (b * nq + i, h)),
                  pl.BlockSpec((T, QK_PAD), lambda b, h, i: (b, h)),
                  pl.BlockSpec((T, V_DIM), lambda b, h, i: (b, h))],
        out_specs=pl.BlockSpec((tq, V_DIM), lambda b, h, i: (b * nq + i, h)),
        out_shape=jax.ShapeDtypeStruct((B * T, n_heads * V_DIM), F32),
        compiler_params=_cparams("parallel", "parallel", "arbitrary"),
        name="mla_flash",
    )(q, k, v)


def _bmm_kernel(x_ref, w_ref, o_ref):
    o_ref[...] = jnp.dot(x_ref[...].astype(BF16), w_ref[...], preferred_element_type=F32)


def _head_matmul(x, w):
    H, R, K = x.shape
    N = w.shape[2]
    return pl.pallas_call(
        _bmm_kernel,
        grid=(H,),
        in_specs=[pl.BlockSpec((None, R, K), lambda h: (h, 0, 0)),
                  pl.BlockSpec((None, K, N), lambda h: (h, 0, 0))],
        out_specs=pl.BlockSpec((None, R, N), lambda h: (h, 0, 0)),
        out_shape=jax.ShapeDtypeStruct((H, R, N), F32),
        compiler_params=_cparams("parallel"),
        name="head_matmul",
    )(x, w)


def _paged_kernel(pt_ref, q_ref, cn_ref, kn_ref, ckv_hbm, kr_hbm, o_ref,
                  cbuf, kbuf, csem, ksem, m_ref, l_ref, acc_ref, *, pg, n_new, kvlora):
    b = pl.program_id(0)
    g = pl.program_id(1)
    n_g = pl.num_programs(1)
    step = b * n_g + g
    slot = step % 2

    def page_copies(bb, gg, sl):
        cps = []
        for i in range(pg):
            page = pt_ref[bb, gg * pg + i]
            cps.append(pltpu.make_async_copy(ckv_hbm.at[page], cbuf.at[sl, i], csem.at[sl]))
            cps.append(pltpu.make_async_copy(kr_hbm.at[page], kbuf.at[sl, i], ksem.at[sl]))
        return cps

    @pl.when(step == 0)
    def _():
        for cp in page_copies(0, 0, 0):
            cp.start()

    nxt = step + 1

    @pl.when(nxt < pl.num_programs(0) * n_g)
    def _():
        for cp in page_copies(nxt // n_g, nxt % n_g, 1 - slot):
            cp.start()

    for cp in page_copies(b, g, slot):
        cp.wait()
    ckv_refs = [cbuf.at[slot, i] for i in range(pg)]
    kr_refs = [kbuf.at[slot, i] for i in range(pg)]

    @pl.when(g == 0)
    def _():
        m_ref[...] = jnp.full(m_ref.shape, NEG_INF, F32)
        l_ref[...] = jnp.zeros(l_ref.shape, F32)
        acc_ref[...] = jnp.zeros(acc_ref.shape, F32)

    q = q_ref[...]
    ql = q[:, :kvlora]
    qr = q[:, kvlora:]
    dn = (((1,), (1,)), ((), ()))

    def scores(c_ref, k_ref, k_is_transposed):
        kb = k_ref[...].astype(BF16)
        if k_is_transposed:
            rope = jnp.dot(qr, kb, preferred_element_type=F32)
        else:
            rope = lax.dot_general(qr, kb, dn, preferred_element_type=F32)
        return lax.dot_general(ql, c_ref[...].astype(BF16), dn, preferred_element_type=F32) + rope

    def update(s, value_refs):
        m = m_ref[...]
        m_new = jnp.maximum(m, jnp.max(s, axis=-1, keepdims=True))
        p = jnp.exp(s - m_new).astype(BF16)
        corr = jnp.exp(m - m_new)
        l_ref[...] = corr * l_ref[...] + jnp.sum(p.astype(F32), axis=-1, keepdims=True)
        pv = None
        off = 0
        for c_ref in value_refs:
            n = c_ref.shape[0]
            part = jnp.dot(p[:, off:off + n], c_ref[...].astype(BF16), preferred_element_type=F32)
            pv = part if pv is None else pv + part
            off += n
        acc_ref[...] = corr * acc_ref[...] + pv
        m_ref[...] = m_new

    s_all = jnp.concatenate([scores(c, k, True) for c, k in zip(ckv_refs, kr_refs)], axis=-1)
    update(s_all, ckv_refs)

    @pl.when(g == pl.num_programs(1) - 1)
    def _():
        rows = q.shape[0]
        npad = cn_ref.shape[0]
        tq = lax.broadcasted_iota(jnp.int32, (rows, npad), 0) % n_new
        kj = lax.broadcasted_iota(jnp.int32, (rows, npad), 1)
        update(jnp.where(kj <= tq, scores(cn_ref, kn_ref, False), NEG_INF), [cn_ref])
        o_ref[...] = acc_ref[...] / l_ref[...]


def _paged_attention(page_table, qs, ckv_new, kr_new, cache_ckv, cache_krope, *, pg, n_new):
    Bd, R, QW = qs.shape
    kvlora = cache_ckv.shape[2]
    page = cache_ckv.shape[1]
    n_pages = page_table.shape[1]
    pg = min(pg, n_pages)
    assert n_pages % pg == 0
    npad = ckv_new.shape[1]
    krope_t = jnp.swapaxes(cache_krope, 1, 2)
    in_specs = [pl.BlockSpec((None, R, QW), lambda b, g, pt: (b, 0, 0)),
                pl.BlockSpec((None, npad, kvlora), lambda b, g, pt: (b, 0, 0)),
                pl.BlockSpec((None, npad, ROPE_DIM), lambda b, g, pt: (b, 0, 0)),
                pl.BlockSpec(memory_space=pl.ANY),
                pl.BlockSpec(memory_space=pl.ANY)]
    grid_spec = pltpu.PrefetchScalarGridSpec(
        num_scalar_prefetch=1, grid=(Bd, n_pages // pg), in_specs=in_specs,
        out_specs=pl.BlockSpec((None, R, kvlora), lambda b, g, pt: (b, 0, 0)),
        scratch_shapes=[pltpu.VMEM((2, pg, page, kvlora), F32), pltpu.VMEM((2, pg, ROPE_DIM, page), F32),
                        pltpu.SemaphoreType.DMA((2,)), pltpu.SemaphoreType.DMA((2,)),
                        pltpu.VMEM((R, 1), F32), pltpu.VMEM((R, 1), F32), pltpu.VMEM((R, kvlora), F32)])
    return pl.pallas_call(
        functools.partial(_paged_kernel, pg=pg, n_new=n_new, kvlora=kvlora),
        grid_spec=grid_spec,
        out_shape=jax.ShapeDtypeStruct((Bd, R, kvlora), F32),
        compiler_params=_cparams("arbitrary", "arbitrary"),
        name="mla_paged",
    )(page_table, qs, ckv_new, kr_new, cache_ckv, krope_t)


def _merge_rows(x, y, bo, g, ga, gb, om, lnw_ref, lnb_ref, wor_ref, wom_ref, nf_ref,
                rw_ref, rb_ref, *, n_experts, n_heads):
    inv_n = 1.0 / RWKV_HEAD_DIM
    mean = _head_sum(y, n_heads) * inv_n
    yc = y - mean
    var = _head_sum(yc * yc, n_heads) * inv_n
    yn = yc * lax.rsqrt(var + GN_EPS) * lnw_ref[...] + lnb_ref[...]
    o_rwkv = (yn + bo) * g
    h = (x + jnp.dot((jax.nn.sigmoid(ga) * o_rwkv).astype(BF16), wor_ref[...], preferred_element_type=F32)
         + jnp.dot((jax.nn.sigmoid(gb) * om).astype(BF16), wom_ref[...], preferred_element_type=F32))
    hn = _rms(h, nf_ref[...])

    rw = rw_ref[...]
    w_hi = rw.astype(BF16)
    w_r1 = rw - w_hi.astype(F32)
    w_mid = w_r1.astype(BF16)
    x_hi, x_mid, _ = _split3(hn)
    d = lambda a, b: jnp.dot(a, b, preferred_element_type=F32)
    logits = d(x_hi, w_hi) + (d(x_hi, w_mid) + d(x_mid, w_hi)) + rb_ref[...]
    lane = lax.broadcasted_iota(jnp.int32, logits.shape, 1)
    work = jnp.where(lane < n_experts, logits, -jnp.inf)
    vals, idxs = [], []
    for _ in range(TOP_K):
        mx = jnp.max(work, axis=-1, keepdims=True)
        ix = jnp.min(jnp.where(work == mx, lane, LANES), axis=-1, keepdims=True)
        vals.append(mx)
        idxs.append(ix)
        work = jnp.where(lane == ix, -jnp.inf, work)
    ex = [jnp.exp(vv - vals[0]) for vv in vals]
    den = ex[0] + ex[1] + ex[2] + ex[3]
    ti = jnp.zeros(logits.shape, jnp.int32)
    tg = jnp.zeros(logits.shape, F32)
    for j in range(TOP_K):
        ti = jnp.where(lane == j, idxs[j], ti)
        tg = jnp.where(lane == j, ex[j] / den, tg)
    return h, hn, ti, tg


def _merge_kernel(x_ref, y_ref, bo_ref, g_ref, ga_ref, gb_ref, om_ref, *refs, n_experts, n_heads,
                  vsplit):
    consts, (h_ref, hn_ref, ti_ref, tg_ref) = refs[:-4], refs[-4:]
    n_seq, tm, D = x_ref.shape
    for b in range(n_seq):
        if vsplit:
            vr = D // n_heads // vsplit
            pieces = [None] * (vr * vsplit)
            for vq in range(vr):
                rows = y_ref[vq * tm:(vq + 1) * tm, :]
                for vs in range(vsplit):
                    lo = (b * vsplit + vs) * n_heads
                    pieces[vs * vr + vq] = rows[:, lo:lo + n_heads]
            y = jnp.concatenate(pieces, axis=-1)
        else:
            y = y_ref[b]
        h, hn, ti, tg = _merge_rows(x_ref[b], y, bo_ref[b], g_ref[b], ga_ref[b], gb_ref[b], om_ref[b],
                                    *consts, n_experts=n_experts, n_heads=n_heads)
        h_ref[b] = h
        hn_ref[b] = hn
        ti_ref[b] = ti
        tg_ref[b] = tg


def _merge(x, y, bonus, g, p5, o_mla, consts, *, tm, n_experts, n_heads, vsplit):
    G, R, D = x.shape
    tm = _pick_tile(R, tm, 8)
    blk = lambda c=0: pl.BlockSpec((G, tm, D), lambda i, c=c: (0, i, c))
    full = lambda a: pl.BlockSpec(a.shape, lambda i: (0,) * a.ndim)
    if vsplit:
        vr = D // n_heads // vsplit
        yspec = pl.BlockSpec((tm * vr, LANES), lambda i: (i, 0))
    else:
        yspec = blk()
    in_specs = [blk(), yspec, blk(), blk(), blk(3), blk(4), blk()] + [full(a) for a in consts]
    lane_blk = pl.BlockSpec((G, tm, LANES), lambda i: (0, i, 0))
    out_specs = [blk(), blk(), lane_blk, lane_blk]
    out_shape = [jax.ShapeDtypeStruct((G, R, D), F32), jax.ShapeDtypeStruct((G, R, D), F32),
                 jax.ShapeDtypeStruct((G, R, LANES), jnp.int32),
                 jax.ShapeDtypeStruct((G, R, LANES), F32)]
    return pl.pallas_call(
        functools.partial(_merge_kernel, n_experts=n_experts, n_heads=n_heads, vsplit=vsplit),
        grid=(R // tm,), in_specs=in_specs, out_specs=out_specs, out_shape=out_shape,
        compiler_params=_cparams("parallel"),
        name="merge_router",
    )(x, y, bonus, g, p5, p5, o_mla, *consts)


def _w1_prep_kernel(w_ref, p_ref, g_ref, l_ref):
    perm = p_ref[...]
    half = perm.shape[0] // 2
    for c in range(w_ref.shape[1] // perm.shape[0]):
        chunk = w_ref[:, c * 2 * half:(c + 1) * 2 * half].astype(BF16)
        both = jnp.dot(chunk, perm, preferred_element_type=F32)
        g_ref[:, c * half:(c + 1) * half] = both[:, :half].astype(BF16)
        l_ref[:, c * half:(c + 1) * half] = both[:, half:].astype(BF16)


def _w1_prep(w1, *, tr, tc):
    E, D, F2 = w1.shape
    pw = 2 * LANES
    tr = _pick_tile(D, tr, 8)
    tc = _pick_tile(F2, tc, pw)
    j = np.arange(LANES)
    perm = np.zeros((pw, pw), np.float32)
    perm[2 * j, j] = 1.0
    perm[2 * j + 1, LANES + j] = 1.0
    out = jax.ShapeDtypeStruct((E, D, F2 // 2), BF16)
    return pl.pallas_call(
        _w1_prep_kernel,
        grid=(E, D // tr, F2 // tc),
        in_specs=[pl.BlockSpec((None, tr, tc), lambda e, i, j: (e, i, j)),
                  pl.BlockSpec((pw, pw), lambda e, i, j: (0, 0))],
        out_specs=[pl.BlockSpec((None, tr, tc // 2), lambda e, i, j: (e, i, j))] * 2,
        out_shape=[out, out],
        compiler_params=_cparams("parallel", "parallel", "parallel"),
        name="moe_w1_prep",
    )(w1, jnp.asarray(perm, BF16))


def _expert_kernel(se_ref, sh_ref, sr_ref, sm_ref, x_ref, w1g_ref, w1l_ref, b1g_ref, b1l_ref,
                   w2_ref, b2_ref, o_ref, w2b_ref):
    s = pl.program_id(0)
    mode = sm_ref[s]

    @pl.when(mode == 0)
    def _():
        o_ref[...] = jnp.zeros(o_ref.shape, o_ref.dtype)

    @pl.when(mode == 2)
    def _():
        w2b_ref[...] = w2_ref[...].astype(BF16)

    @pl.when(mode >= 1)
    def _():
        x = x_ref[...]
        hg = jnp.dot(x, w1g_ref[...], preferred_element_type=F32) + b1g_ref[...]
        hl = jnp.dot(x, w1l_ref[...], preferred_element_type=F32) + b1l_ref[...]
        hg = jnp.minimum(hg, SWIGLU_LIMIT)
        hl = jnp.clip(hl, -SWIGLU_LIMIT, SWIGLU_LIMIT)
        act = hg * jax.nn.sigmoid(SWIGLU_ALPHA * hg) * (hl + 1.0)
        y = jnp.dot(act.astype(BF16), w2b_ref[...], preferred_element_type=F32)
        first = (sh_ref[s] == 0).astype(F32)
        o_ref[...] = (y + first * b2_ref[...]).astype(o_ref.dtype)


def _row_gather_kernel(tok_ref, x_hbm, o_ref, buf, sem, *, bm):
    i = pl.program_id(0)
    slot = i % 2

    def row_copy(t, sl, r):
        return pltpu.make_async_copy(x_hbm.at[pl.ds(t, 1)], buf.at[sl, pl.ds(r, 1)], sem.at[sl])

    def start_block(blk, sl):
        def body(r, carry):
            row_copy(tok_ref[blk * bm + r], sl, r).start()
            return carry
        lax.fori_loop(0, bm, body, 0, unroll=8)

    @pl.when(i == 0)
    def _():
        start_block(0, 0)

    @pl.when(i + 1 < pl.num_programs(0))
    def _():
        start_block(i + 1, 1 - slot)

    def wait_row(r, carry):
        row_copy(tok_ref[i * bm + r], slot, r).wait()
        return carry
    lax.fori_loop(0, bm, wait_row, 0, unroll=8)
    o_ref[...] = buf[slot].astype(o_ref.dtype)


def _row_gather(x, tok, *, bm):
    M, D = x.shape
    NR = tok.shape[0]
    assert NR % bm == 0
    grid_spec = pltpu.PrefetchScalarGridSpec(
        num_scalar_prefetch=1, grid=(NR // bm,),
        in_specs=[pl.BlockSpec(memory_space=pl.ANY)],
        out_specs=pl.BlockSpec((bm, D), lambda i, tok: (i, 0)),
        scratch_shapes=[pltpu.VMEM((2, bm, D), F32), pltpu.SemaphoreType.DMA((2,))])
    return pl.pallas_call(
        functools.partial(_row_gather_kernel, bm=bm), grid_spec=grid_spec,
        out_shape=jax.ShapeDtypeStruct((NR, D), BF16),
        compiler_params=_cparams("arbitrary"),
        name="moe_row_gather",
    )(tok, x)


def _experts(sched, xs, w1g, w1l, b1g, b1l, w2, b2, *, bm, n_half):
    se, sh, sr, sm = sched
    NR, D = xs.shape
    E, _, F = w1g.shape
    fh = F // n_half
    S = se.shape[0]
    in_specs = [pl.BlockSpec((bm, D), lambda s, se, sh, sr, sm: (sr[s], 0)),
                pl.BlockSpec((None, D, fh), lambda s, se, sh, sr, sm: (se[s], 0, sh[s])),
                pl.BlockSpec((None, D, fh), lambda s, se, sh, sr, sm: (se[s], 0, sh[s])),
                pl.BlockSpec((None, 1, fh), lambda s, se, sh, sr, sm: (se[s], 0, sh[s])),
                pl.BlockSpec((None, 1, fh), lambda s, se, sh, sr, sm: (se[s], 0, sh[s])),
                pl.BlockSpec((None, fh, D), lambda s, se, sh, sr, sm: (se[s], sh[s], 0)),
                pl.BlockSpec((None, 1, D), lambda s, se, sh, sr, sm: (se[s], 0, 0))]
    grid_spec = pltpu.PrefetchScalarGridSpec(
        num_scalar_prefetch=4, grid=(S,), in_specs=in_specs,
        out_specs=pl.BlockSpec((bm, D), lambda s, se, sh, sr, sm: (sr[s], sh[s])),
        scratch_shapes=[pltpu.VMEM((fh, D), BF16)])
    return pl.pallas_call(
        _expert_kernel, grid_spec=grid_spec,
        out_shape=jax.ShapeDtypeStruct((NR, n_half * D), BF16),
        compiler_params=_cparams("arbitrary"),
        name="moe_experts",
    )(se, sh, sr, sm, xs, w1g, w1l, b1g, b1l, w2, b2)


def _combine_kernel(h_ref, yg_ref, tg_ref, nw_ref, o_ref, *, n_half):
    acc = h_ref[...]
    tg = tg_ref[...]
    D = acc.shape[1]
    for j in range(TOP_K):
        rows = yg_ref[j, :, :D].astype(F32)
        for hh in range(1, n_half):
            rows = rows + yg_ref[j, :, hh * D:(hh + 1) * D].astype(F32)
        acc = acc + rows * tg[:, j:j + 1]
    o_ref[...] = _rms(acc, nw_ref[...])


def _combine(h, yg, tg, nw, *, tm, n_half):
    M, D = h.shape
    tm = _pick_tile(M, tm, 8)
    return pl.pallas_call(
        functools.partial(_combine_kernel, n_half=n_half),
        grid=(M // tm,),
        in_specs=[pl.BlockSpec((tm, D), lambda i: (i, 0)),
                  pl.BlockSpec((TOP_K, tm, n_half * D), lambda i: (0, i, 0)),
                  pl.BlockSpec((tm, LANES), lambda i: (i, 0)),
                  pl.BlockSpec((1, D), lambda i: (0, 0))],
        out_specs=pl.BlockSpec((tm, D), lambda i: (i, 0)),
        out_shape=jax.ShapeDtypeStruct((M, D), F32),
        compiler_params=_cparams("parallel"),
        name="moe_combine",
    )(h, yg, tg, nw)


def _rope_tables(positions):
    inv = ROPE_THETA ** (-jnp.arange(0, ROPE_DIM, 2, dtype=F32) / ROPE_DIM)
    ang = positions.astype(F32)[:, None] * inv[None, :]
    cos, sin = jnp.cos(ang), jnp.sin(ang)
    cos2 = jnp.concatenate([cos, cos], axis=-1)
    sin2 = jnp.concatenate([sin, sin], axis=-1)
    n = positions.shape[0]
    ones = jnp.ones((n, NOPE_DIM), F32)
    z64 = jnp.zeros((n, QK_PAD - NOPE_DIM - ROPE_DIM), F32)
    z128 = jnp.zeros((n, NOPE_DIM), F32)
    qtab = jnp.concatenate([ones, cos2, z64, z128, sin2, z64], axis=-1)
    ktab = jnp.concatenate([cos2, sin2], axis=-1)
    return qtab, ktab


def _rot_cols(w):
    half = ROPE_DIM // 2
    return jnp.concatenate([-w[..., half:], w[..., :half]], axis=-1)


def _moe_schedule(top_idx, n_experts, bm, n_half, n_blocks):
    M = top_idx.shape[0]
    flat_e = top_idx.reshape(-1)
    onehot = (flat_e[:, None] == jnp.arange(n_experts, dtype=jnp.int32)[None, :]).astype(jnp.int32)
    rank = jnp.take_along_axis(jnp.cumsum(onehot, axis=0) - onehot, flat_e[:, None], axis=1)[:, 0]
    counts = jnp.sum(onehot, axis=0)
    nb = (counts + bm - 1) // bm
    blk_end = jnp.cumsum(nb)
    blk_start = blk_end - nb
    dest = blk_start[flat_e] * bm + rank
    flat_tok = jnp.repeat(jnp.arange(M, dtype=jnp.int32), TOP_K)
    row_tok = jnp.zeros((n_blocks * bm,), jnp.int32).at[dest].set(flat_tok)
    n_real = blk_end[-1]
    steps_end = jnp.cumsum(nb * n_half)
    s = jnp.arange(n_blocks * n_half, dtype=jnp.int32)
    real = s < n_real * n_half
    e_of = jnp.minimum(jnp.sum(steps_end[None, :] <= s[:, None], axis=1), n_experts - 1).astype(jnp.int32)
    local = s - (steps_end - nb * n_half)[e_of]
    nb_e = jnp.maximum(nb[e_of], 1)
    h_real = local // nb_e
    r_real = blk_start[e_of] + local % nb_e
    last_e = e_of[jnp.maximum(n_real * n_half - 1, 0)]
    tail = s - n_real * n_half
    n_tail = jnp.maximum(n_blocks - n_real, 1)
    se = jnp.where(real, e_of, last_e).astype(jnp.int32)
    sh = jnp.where(real, h_real, n_half - 1 - tail // n_tail).astype(jnp.int32)
    sr = jnp.where(real, r_real, n_real + tail % n_tail).astype(jnp.int32)
    sm = jnp.where(real, jnp.where(local % nb_e == 0, 2, 1), 0).astype(jnp.int32)
    return dest, row_tok, (se, sh, sr, sm)


def kernel(x_prompt, x_sample, cache_ckv, cache_krope, state_wkv, state_shift, page_table, norm_mix_w, w_in, rwkv_mu, rwkv_w0, rwkv_w2, rwkv_a0, rwkv_a2, rwkv_g2, rwkv_k_k, rwkv_k_a, rwkv_r_k, rwkv_ln_w, rwkv_ln_b, mla_q_norm_w, mla_w_uq, mla_kv_norm_w, mla_w_uk, mla_w_uv, w_out, norm_ffn_w, router_w, router_b, expert_w1, expert_b1, expert_w2, expert_b2, norm_final_w):
    B, T, D = x_prompt.shape
    Bd, Td, _ = x_sample.shape
    H = D // RWKV_HEAD_DIM
    HD = RWKV_HEAD_DIM
    dlora, alora, glora = rwkv_w2.shape[0], rwkv_a2.shape[0], rwkv_g2.shape[0]
    qlora, Hm, _ = mla_w_uq.shape
    kvlora = mla_w_uk.shape[0]
    n_experts = router_w.shape[1]
    F = expert_w2.shape[1]
    page = cache_ckv.shape[1]
    past = page_table.shape[1] * page
    scale = float((NOPE_DIM + ROPE_DIM) ** -0.5)
    row2 = lambda a: a.reshape(1, -1).astype(F32)

    sizes = [D, D, D, dlora, alora, glora, qlora, kvlora, ROPE_DIM, D, D]
    offs = np.concatenate([[0], np.cumsum(sizes)]).tolist()
    col = lambda i: w_in[:, offs[i]:offs[i + 1]]
    lw_pad, la_pad = _round_up(dlora, LANES), _round_up(alora, LANES)
    padc = lambda a, n: jnp.pad(a, ((0, 0), (0, n - a.shape[1])))
    perm = np.arange(D).reshape(H, HD).T.reshape(-1)
    pc = lambda a: a[..., perm]
    w5 = jnp.concatenate([pc(col(0)), pc(col(1)), pc(col(2)), pc(col(9)), col(10)], axis=1).astype(BF16)
    lora_w = lw_pad + la_pad + glora
    ws = jnp.concatenate([padc(col(3), lw_pad), padc(col(4), la_pad), col(5), col(6), col(7),
                          col(8), _rot_cols(col(8))], axis=1)
    col_cq, col_ckv, col_kr = lora_w, lora_w + qlora, lora_w + qlora + kvlora
    ws = padc(ws, _round_up(ws.shape[1], LANES)).astype(BF16)
    mu = rwkv_mu
    mur, muk, muv = row2(pc(mu[:D])), row2(pc(mu[D:2 * D])), row2(pc(mu[2 * D:3 * D]))
    m0 = 3 * D
    mul = jnp.concatenate([jnp.pad(mu[m0:m0 + dlora], (0, lw_pad - dlora)),
                           jnp.pad(mu[m0 + dlora:m0 + dlora + alora], (0, la_pad - alora)),
                           mu[m0 + dlora + alora:]]).reshape(1, -1)
    w2p = jnp.pad(pc(rwkv_w2), ((0, lw_pad - dlora), (0, 0))).astype(BF16)
    a2p = jnp.pad(pc(rwkv_a2), ((0, la_pad - alora), (0, 0))).astype(BF16)
    prep_consts = (mur, muk, muv, mul, row2(pc(rwkv_w0)), w2p, row2(pc(rwkv_a0)), a2p,
                   pc(rwkv_g2).astype(BF16), row2(pc(rwkv_k_k)), row2(pc(rwkv_k_a)),
                   row2(pc(rwkv_r_k.reshape(-1))), lw_pad, la_pad)

    zq = jnp.zeros((qlora, Hm, QK_PAD - NOPE_DIM - ROPE_DIM), F32)
    q_nope_w, q_rope_w = mla_w_uq[..., :NOPE_DIM], mla_w_uq[..., NOPE_DIM:]
    wqa = jnp.concatenate([q_nope_w, q_rope_w, zq], axis=-1).reshape(qlora, Hm * QK_PAD).astype(BF16)
    wqb = jnp.concatenate([jnp.zeros_like(q_nope_w), _rot_cols(q_rope_w), zq],
                          axis=-1).reshape(qlora, Hm * QK_PAD).astype(BF16)
    zk = jnp.zeros((kvlora, Hm, QK_PAD - NOPE_DIM), F32)
    wk_top = jnp.concatenate([mla_w_uk, zk], axis=-1).reshape(kvlora, Hm * QK_PAD)
    sel = jnp.concatenate([jnp.zeros((ROPE_DIM, NOPE_DIM), F32), jnp.eye(ROPE_DIM, dtype=F32),
                           jnp.zeros((ROPE_DIM, QK_PAD - NOPE_DIM - ROPE_DIM), F32)], axis=-1)
    wk = jnp.concatenate([wk_top, jnp.tile(sel, (1, Hm))], axis=0).astype(BF16)
    wv = mla_w_uv.reshape(kvlora, Hm * V_DIM).astype(BF16)
    w_uk_t = jnp.transpose(mla_w_uk, (1, 2, 0)).astype(BF16)
    w_uv_h = jnp.transpose(mla_w_uv, (1, 0, 2)).astype(BF16)
    wo_mla = w_out.astype(BF16)
    wo_rwkv = w_out[perm, :].astype(BF16)
    rw = jnp.pad(router_w, ((0, 0), (0, LANES - n_experts)))
    rb = jnp.pad(router_b, (0, LANES - n_experts)).reshape(1, LANES)
    w1g, w1l = _w1_prep(expert_w1, tr=512, tc=2048)
    b1g = expert_b1[:, None, 0::2]
    b1l = expert_b1[:, None, 1::2]
    w2e = expert_w2
    b2e = expert_b2[:, None, :]
    nmw = row2(norm_mix_w)

    def project_stage(x2d, positions, tm_mla):
        p5 = _project(x2d, nmw, w5, normalize=True, tm=1024, tn=512)
        ps = _project(x2d, nmw, ws, normalize=True, tm=1024, tn=ws.shape[1])
        qtab, ktab = _rope_tables(positions)
        mla = _mla_prep(ps, qtab, ktab, row2(mla_q_norm_w), row2(mla_kv_norm_w), wqa, wqb, wk, wv,
                        col_cq=col_cq, col_ckv=col_ckv, col_kr=col_kr, qlora=qlora, kvlora=kvlora,
                        n_heads=Hm, hg=min(4, Hm), tm=tm_mla, scale=scale)
        return p5, ps, mla

    merge_consts = (row2(pc(rwkv_ln_w)), row2(pc(rwkv_ln_b)), wo_rwkv, wo_mla,
                    row2(norm_ffn_w), rw, rb)

    xp = x_prompt.reshape(B * T, D)
    p5_p, ps_p, mla_p = project_stage(xp, jnp.arange(T), min(512, T))
    vsplit = max(1, LANES // (B * H))
    assert HD % vsplit == 0 and B * H * vsplit <= LANES
    vr = HD // vsplit
    tb_p = min(64, T)
    zr, zw, zk, zv, za, zb, g_p, bo_p = _rwkv_prep_packed(
        p5_p.reshape(B, T, -1), ps_p.reshape(B, T, -1), prep_consts, B=B, T=T, tt=tb_p, D=D,
        lora_w=lora_w, n_heads=H, vsplit=vsplit)
    s0_p = jnp.zeros((HD, vr, LANES), F32)
    y_pk, s_kl = _rwkv_scan(zr, zw, zk, zv, za, zb, s0_p, tb=tb_p, packed=True)
    nl_p = B * vsplit * H
    wkv_p = s_kl[:, :, :nl_p].reshape(HD, vr, B, vsplit, H).transpose(2, 4, 3, 1, 0).reshape(B, H, HD, HD)

    q_p, kf_p, vf_p, ckv_p, kr_p = mla_p
    o_mla_p = _flash_attention(q_p, kf_p, vf_p, B=B, T=T, n_heads=Hm, tq=1024, tk=512)
    h_p, hn_p, ti_p, tg_p = _merge(
        x_prompt, y_pk, bo_p, g_p, p5_p.reshape(B, T, -1),
        o_mla_p.reshape(B, T, -1), merge_consts, tm=tb_p, n_experts=n_experts, n_heads=H, vsplit=vsplit)

    xs_tm = jnp.swapaxes(x_sample, 0, 1).reshape(Td * Bd, D)
    pos_s = jnp.repeat(past + jnp.arange(Td), Bd)
    p5_s, ps_s, mla_s = project_stage(xs_tm, pos_s, Td * Bd)
    s5 = _project(state_shift, nmw, w5, normalize=False, tm=1024, tn=512).reshape(1, Bd, -1)
    ss = _project(state_shift, nmw, ws, normalize=False, tm=1024, tn=ws.shape[1]).reshape(1, Bd, -1)
    r_s, lw_s, k_s, v_s, al_s, be_s, g_s, bo_s = _rwkv_prep(
        p5_s, ps_s, s5, ss, prep_consts, n_groups=1, n_tiles=Td, tt=Bd, shift=Bd, D=D, lora_w=lora_w,
        n_heads=H)

    def to_kl_s(a):
        return a.reshape(Td, Bd, HD, H).transpose(0, 2, 1, 3).reshape(Td, HD, Bd * H)

    nl_s = Bd * H
    pad_s = _round_up(nl_s, LANES) - nl_s
    pads = lambda a: jnp.pad(a, ((0, 0), (0, 0), (0, pad_s))) if pad_s else a
    s0_s = pads(state_wkv.transpose(3, 2, 0, 1).reshape(HD, HD, nl_s))
    y_sl, s_sl = _rwkv_scan(pads(to_kl_s(r_s)), pads(to_kl_s(lw_s)), pads(to_kl_s(k_s)),
                            pads(to_kl_s(v_s)), pads(to_kl_s(al_s)), pads(to_kl_s(be_s)), s0_s, tb=Td)
    y_s = y_sl[:, :, :nl_s].reshape(Td, HD, Bd, H).transpose(0, 2, 1, 3).reshape(Td * Bd, D)
    wkv_s = s_sl[:, :, :nl_s].reshape(HD, HD, Bd, H).transpose(2, 3, 1, 0)

    q_s, _, _, ckv_s, kr_s = mla_s
    q4 = q_s.reshape(Td, Bd, Hm, QK_PAD)
    qn_h = q4[..., :NOPE_DIM].transpose(2, 1, 0, 3).reshape(Hm, Bd * Td, NOPE_DIM)
    q_lat = _head_matmul(qn_h, w_uk_t)
    q_lat = q_lat.reshape(Hm, Bd, Td, kvlora).transpose(1, 0, 2, 3).reshape(Bd, Hm * Td, kvlora)
    q_rp = q4[..., NOPE_DIM:NOPE_DIM + ROPE_DIM].transpose(1, 2, 0, 3).reshape(Bd, Hm * Td, ROPE_DIM)
    qs = jnp.concatenate([q_lat.astype(BF16), q_rp], axis=-1)
    ckv_s_bt = ckv_s.reshape(Td, Bd, kvlora).transpose(1, 0, 2)
    kr_s_bt = kr_s.reshape(Td, Bd, ROPE_DIM).transpose(1, 0, 2)
    npad = _round_up(Td, 8)
    padn = lambda a: jnp.pad(a, ((0, 0), (0, npad - Td), (0, 0)))
    o_lat = _paged_attention(page_table, qs, padn(ckv_s_bt), padn(kr_s_bt), cache_ckv, cache_krope,
                             pg=16, n_new=Td)
    o_lat_h = o_lat.reshape(Bd, Hm, Td, kvlora).transpose(1, 2, 0, 3).reshape(Hm, Td * Bd, kvlora)
    o_mla_s = _head_matmul(o_lat_h, w_uv_h)
    o_mla_s = o_mla_s.transpose(1, 0, 2).reshape(Td * Bd, Hm * V_DIM)

    g1 = lambda a: a[None]
    h_s, hn_s, ti_s, tg_s = _merge(g1(xs_tm), g1(y_s), g1(bo_s), g1(g_s), g1(p5_s), g1(o_mla_s),
                                   merge_consts, tm=128, n_experts=n_experts, n_heads=H, vsplit=0)
    flat = lambda a: a.reshape(-1, a.shape[-1])
    h_all = jnp.concatenate([flat(h_p), flat(h_s)], axis=0)
    hn_all = jnp.concatenate([flat(hn_p), flat(hn_s)], axis=0)
    ti_all = jnp.concatenate([flat(ti_p), flat(ti_s)], axis=0)[:, :TOP_K]
    tg_all = jnp.concatenate([flat(tg_p), flat(tg_s)], axis=0)
    M = h_all.shape[0]
    bm = 256
    n_half = 2 if F % (2 * LANES) == 0 else 1
    n_blocks = (M * TOP_K + n_experts * (bm - 1)) // bm
    dest, row_tok, sched = _moe_schedule(ti_all, n_experts, bm, n_half, n_blocks)
    xs_rows = _row_gather(hn_all, row_tok, bm=bm)
    y_halves = _experts(sched, xs_rows, w1g, w1l, b1g, b1l, w2e, b2e, bm=bm, n_half=n_half)
    dest_slot_major = dest.reshape(M, TOP_K).T.reshape(-1)
    yg = y_halves[dest_slot_major].reshape(TOP_K, M, n_half * D)
    y_all = _combine(h_all, yg, tg_all, row2(norm_final_w), tm=128, n_half=n_half)

    y_prompt = y_all[:B * T].reshape(B, T, D)
    y_sample = y_all[B * T:].reshape(Td, Bd, D).transpose(1, 0, 2)
    last = jnp.concatenate([x_prompt[:, -1], x_sample[:, -1]], axis=0)
    shift_out = _rmsnorm_rows(last, nmw)
    return (y_prompt, y_sample,
            ckv_p.reshape(B, T // page, page, kvlora), kr_p.reshape(B, T // page, page, ROPE_DIM),
            wkv_p, shift_out[:B],
            ckv_s_bt, kr_s_bt, wkv_s, shift_out[B:])
```

```python
import functools

import numpy as np
import jax
import jax.numpy as jnp
from jax import lax
from jax.experimental import pallas as pl
from jax.experimental.pallas import tpu as pltpu

F32 = jnp.float32
BF16 = jnp.bfloat16

NORM_EPS = 1e-6
GN_EPS = 64e-5
RWKV_HEAD_DIM = 64
NOPE_DIM = 128
ROPE_DIM = 64
V_DIM = 128
QK_PAD = 256
ROPE_THETA = 10000.0
NEG_INF = -1e30
TOP_K = 4
SWIGLU_ALPHA = 1.702
SWIGLU_LIMIT = 7.0
LANES = 128
VMEM_LIMIT = 56 * 1024 * 1024


def _round_up(n, m):
    return (n + m - 1) // m * m


def _pick_tile(n, target, mult):
    if n <= target:
        return n
    t = target // mult * mult
    while n % t:
        t -= mult
    return t


def _cparams(*sem):
    return pltpu.CompilerParams(dimension_semantics=sem, vmem_limit_bytes=VMEM_LIMIT)


def _split3(x):
    hi = x.astype(BF16)
    r1 = x - hi.astype(F32)
    mid = r1.astype(BF16)
    lo = (r1 - mid.astype(F32)).astype(BF16)
    return hi, mid, lo


def _tree_sum(xs):
    xs = list(xs)
    while len(xs) > 1:
        xs = [xs[i] + xs[i + 1] for i in range(0, len(xs) - 1, 2)] + ([xs[-1]] if len(xs) % 2 else [])
    return xs[0]


def _rms(x, w):
    return x * lax.rsqrt(jnp.mean(x * x, axis=-1, keepdims=True) + NORM_EPS) * w


def _proj_kernel(x_ref, nw_ref, w_ref, o_ref, xn_ref, *, normalize):
    @pl.when(pl.program_id(1) == 0)
    def _():
        x = x_ref[...]
        if normalize:
            x = _rms(x, nw_ref[...])
        xn_ref[...] = x.astype(BF16)

    o_ref[...] = jnp.dot(xn_ref[...], w_ref[...], preferred_element_type=F32)


def _project(x, norm_w, w, *, normalize, tm, tn):
    M, D = x.shape
    N = w.shape[1]
    tm = _pick_tile(M, tm, 8)
    tn = _pick_tile(N, tn, LANES)
    return pl.pallas_call(
        functools.partial(_proj_kernel, normalize=normalize),
        grid=(M // tm, N // tn),
        in_specs=[pl.BlockSpec((tm, D), lambda i, j: (i, 0)),
                  pl.BlockSpec((1, D), lambda i, j: (0, 0)),
                  pl.BlockSpec((D, tn), lambda i, j: (0, j))],
        out_specs=pl.BlockSpec((tm, tn), lambda i, j: (i, j)),
        out_shape=jax.ShapeDtypeStruct((M, N), F32),
        scratch_shapes=[pltpu.VMEM((tm, D), BF16)],
        compiler_params=_cparams("parallel", "arbitrary"),
        name="norm_proj",
    )(x, norm_w, w)


def _rmsnorm_rows_kernel(x_ref, w_ref, o_ref):
    o_ref[...] = _rms(x_ref[...], w_ref[...])


def _rmsnorm_rows(x, w):
    return pl.pallas_call(
        _rmsnorm_rows_kernel,
        out_shape=jax.ShapeDtypeStruct(x.shape, F32),
        name="rmsnorm_rows",
    )(x, w)


def _head_sum(x, n_heads):
    d = x.shape[1]
    assert d % LANES == 0 and LANES % n_heads == 0
    s = _tree_sum([x[:, j * LANES:(j + 1) * LANES] for j in range(d // LANES)])
    shift = LANES // 2
    while shift >= n_heads:
        s = s + pltpu.roll(s, shift, axis=1)
        shift //= 2
    return jnp.concatenate([s] * (d // LANES), axis=-1)


def _rwkv_prep_math(r, k, v, lo, w0_ref, w2_ref, a0_ref, a2_ref, g2_ref, kk_ref, ka_ref, rk_ref,
                    *, lw_pad, la_pad, n_heads):
    hw = lo[:, :lw_pad]
    ha = lo[:, lw_pad:lw_pad + la_pad]
    hg = lo[:, lw_pad + la_pad:]
    mm = lambda a, b_ref: jnp.dot(a.astype(BF16), b_ref[...], preferred_element_type=F32)
    w = -jax.nn.softplus(-(w0_ref[...] + mm(jnp.tanh(hw), w2_ref))) - 0.5
    a = jax.nn.sigmoid(a0_ref[...] + mm(ha, a2_ref))
    g = mm(jax.nn.sigmoid(hg), g2_ref)
    kk = k * kk_ref[...]
    kk = kk * lax.rsqrt(jnp.maximum(_head_sum(kk * kk, n_heads), 1e-24))
    k2 = k * (1.0 + (a - 1.0) * ka_ref[...])
    bsum = _head_sum(r * k2 * rk_ref[...], n_heads)
    return r, jnp.exp(-jnp.exp(w)), k2, v, -kk, kk * a, g, bsum * v


def _rwkv_prep_packed_kernel(r_ref, k_ref, v_ref, l_ref,
                             mur_ref, muk_ref, muv_ref, mul_ref, w0_ref, w2_ref, a0_ref, a2_ref,
                             g2_ref, kk_ref, ka_ref, rk_ref,
                             zr_ref, zw_ref, zk_ref, zv_ref, za_ref, zb_ref, go_ref, bo_ref,
                             cr_ref, ck_ref, cv_ref, cl_ref, *, lw_pad, la_pad, n_heads, vsplit):
    t = pl.program_id(0)
    n_seq, tt, D = r_ref.shape
    nk = D // n_heads
    vr = nk // vsplit

    @pl.when(t == 0)
    def _():
        for c_ref in (cr_ref, ck_ref, cv_ref, cl_ref):
            c_ref[...] = jnp.zeros(c_ref.shape, F32)

    def lerp(p, c_ref, b, mu_ref):
        rolled = pltpu.roll(p, 1, axis=0)
        row = lax.broadcasted_iota(jnp.int32, p.shape, 0)
        prev = jnp.where(row == 0, c_ref[b], rolled)
        c_ref[b] = p[tt - 1:, :]
        return p + (prev - p) * mu_ref[...]

    outs = []
    for b in range(n_seq):
        r = lerp(r_ref[b], cr_ref, b, mur_ref)
        k = lerp(k_ref[b], ck_ref, b, muk_ref)
        v = lerp(v_ref[b], cv_ref, b, muv_ref)
        lo = lerp(l_ref[b], cl_ref, b, mul_ref)
        res = _rwkv_prep_math(r, k, v, lo, w0_ref, w2_ref, a0_ref, a2_ref, g2_ref, kk_ref, ka_ref,
                              rk_ref, lw_pad=lw_pad, la_pad=la_pad, n_heads=n_heads)
        go_ref[b] = res[6]
        bo_ref[b] = res[7]
        outs.append(res)

    used = n_seq * vsplit * n_heads
    zpad = [jnp.zeros((tt, LANES - used), F32)] if used < LANES else []
    piece = lambda x, j: x[:, j * n_heads:(j + 1) * n_heads]
    for z_ref, idx in ((zr_ref, 0), (zw_ref, 1), (zk_ref, 2), (za_ref, 4), (zb_ref, 5)):
        for kq in range(nk):
            parts = [piece(outs[b][idx], kq) for b in range(n_seq) for _ in range(vsplit)]
            z_ref[kq * tt:(kq + 1) * tt, :] = jnp.concatenate(parts + zpad, axis=-1)
    for vq in range(vr):
        parts = [piece(outs[b][3], vs * vr + vq) for b in range(n_seq) for vs in range(vsplit)]
        zv_ref[vq * tt:(vq + 1) * tt, :] = jnp.concatenate(parts + zpad, axis=-1)


def _rwkv_prep_packed(p5, ps, consts, *, B, T, tt, D, lora_w, n_heads, vsplit):
    (mur, muk, muv, mul, w0, w2p, a0, a2p, g2, k_k, k_a, r_k, lw_pad, la_pad) = consts
    nk = D // n_heads
    vr = nk // vsplit
    n_tiles = T // tt
    big = lambda c: pl.BlockSpec((B, tt, D), lambda t, c=c: (0, t, c))
    full = lambda a: pl.BlockSpec(a.shape, lambda t: (0,) * a.ndim)
    in_specs = [big(0), big(1), big(2), pl.BlockSpec((B, tt, lora_w), lambda t: (0, t, 0)),
                full(mur), full(muk), full(muv), full(mul), full(w0), full(w2p), full(a0),
                full(a2p), full(g2), full(k_k), full(k_a), full(r_k)]
    zspec = pl.BlockSpec((nk * tt, LANES), lambda t: (t, 0))
    vspec = pl.BlockSpec((vr * tt, LANES), lambda t: (t, 0))
    tok = pl.BlockSpec((B, tt, D), lambda t: (0, t, 0))
    zshape = jax.ShapeDtypeStruct((n_tiles * nk * tt, LANES), F32)
    vshape = jax.ShapeDtypeStruct((n_tiles * vr * tt, LANES), F32)
    tshape = jax.ShapeDtypeStruct((B, T, D), F32)
    return pl.pallas_call(
        functools.partial(_rwkv_prep_packed_kernel, lw_pad=lw_pad, la_pad=la_pad, n_heads=n_heads,
                          vsplit=vsplit),
        grid=(n_tiles,),
        in_specs=in_specs,
        out_specs=[zspec, zspec, zspec, vspec, zspec, zspec, tok, tok],
        out_shape=[zshape, zshape, zshape, vshape, zshape, zshape, tshape, tshape],
        scratch_shapes=[pltpu.VMEM((B, 1, D), F32)] * 3 + [pltpu.VMEM((B, 1, lora_w), F32)],
        compiler_params=_cparams("arbitrary"),
        name="rwkv_prep_packed",
    )(p5, p5, p5, ps, mur, muk, muv, mul, w0, w2p, a0, a2p, g2, k_k, k_a, r_k)


def _rwkv_prep_kernel(r_ref, k_ref, v_ref, l_ref, sr_ref, sk_ref, sv_ref, sl_ref,
                      mur_ref, muk_ref, muv_ref, mul_ref, w0_ref, w2_ref, a0_ref, a2_ref, g2_ref,
                      kk_ref, ka_ref, rk_ref,
                      ro_ref, lw_ref, ko_ref, vo_ref, al_ref, be_ref, go_ref, bo_ref,
                      cr_ref, ck_ref, cv_ref, cl_ref, *, shift, lw_pad, la_pad, n_heads):
    t = pl.program_id(1)

    @pl.when(t == 0)
    def _():
        cr_ref[...] = sr_ref[...]
        ck_ref[...] = sk_ref[...]
        cv_ref[...] = sv_ref[...]
        cl_ref[...] = sl_ref[...]

    def lerp(p_ref, c_ref, mu_ref):
        p = p_ref[...]
        if shift == 1:
            rolled = pltpu.roll(p, 1, axis=0)
            row = lax.broadcasted_iota(jnp.int32, p.shape, 0)
            prev = jnp.where(row == 0, c_ref[...], rolled)
            c_new = p[p.shape[0] - 1:, :]
        else:
            prev = c_ref[...]
            c_new = p
        xx = p + (prev - p) * mu_ref[...]
        c_ref[...] = c_new
        return xx

    r = lerp(r_ref, cr_ref, mur_ref)
    k = lerp(k_ref, ck_ref, muk_ref)
    v = lerp(v_ref, cv_ref, muv_ref)
    lo = lerp(l_ref, cl_ref, mul_ref)
    r, lw, k2, v, al, be, g, bonus = _rwkv_prep_math(
        r, k, v, lo, w0_ref, w2_ref, a0_ref, a2_ref, g2_ref, kk_ref, ka_ref, rk_ref,
        lw_pad=lw_pad, la_pad=la_pad, n_heads=n_heads)
    ro_ref[...] = r
    lw_ref[...] = lw
    ko_ref[...] = k2
    vo_ref[...] = v
    al_ref[...] = al
    be_ref[...] = be
    go_ref[...] = g
    bo_ref[...] = bonus


def _rwkv_prep(p5, ps, shift5, shifts, consts, *, n_groups, n_tiles, tt, shift, D, lora_w, n_heads):
    M = p5.shape[0]
    (mur, muk, muv, mul, w0, w2p, a0, a2p, g2, k_k, k_a, r_k, lw_pad, la_pad) = consts
    row = lambda g, t: (g * n_tiles + t)
    big = lambda c: pl.BlockSpec((tt, D), lambda g, t, c=c: (row(g, t), c))
    sh = lambda c: pl.BlockSpec((None, shift, D), lambda g, t, c=c: (g, 0, c))
    full = lambda a: pl.BlockSpec(a.shape, lambda g, t: (0,) * a.ndim)
    in_specs = [big(0), big(1), big(2),
                pl.BlockSpec((tt, lora_w), lambda g, t: (row(g, t), 0)),
                sh(0), sh(1), sh(2),
                pl.BlockSpec((None, shift, lora_w), lambda g, t: (g, 0, 0)),
                full(mur), full(muk), full(muv), full(mul), full(w0), full(w2p), full(a0),
                full(a2p), full(g2), full(k_k), full(k_a), full(r_k)]
    out_spec = pl.BlockSpec((tt, D), lambda g, t: (row(g, t), 0))
    outs = pl.pallas_call(
        functools.partial(_rwkv_prep_kernel, shift=shift, lw_pad=lw_pad, la_pad=la_pad,
                          n_heads=n_heads),
        grid=(n_groups, n_tiles),
        in_specs=in_specs,
        out_specs=[out_spec] * 8,
        out_shape=[jax.ShapeDtypeStruct((M, D), F32)] * 8,
        scratch_shapes=[pltpu.VMEM((shift, D), F32)] * 3 + [pltpu.VMEM((shift, lora_w), F32)],
        compiler_params=_cparams("parallel", "arbitrary"),
        name="rwkv_prep",
    )(p5, p5, p5, ps, shift5, shift5, shift5, shifts,
      mur, muk, muv, mul, w0, w2p, a0, a2p, g2, k_k, k_a, r_k)
    return outs


def _rwkv_scan_kernel(r_ref, w_ref, k_ref, v_ref, a_ref, b_ref, s0_ref, y_ref, sf_ref, s_ref,
                      *, tb, vr, nk, packed):
    tblk = pl.program_id(1)

    @pl.when(tblk == 0)
    def _():
        s_ref[...] = s0_ref[...]

    n_acc = 8
    if packed:
        row = lambda ref, t, k: jnp.broadcast_to(ref[pl.ds(k * tb + t, 1), :], (vr, LANES))
        tile = lambda ref, t, n: ref[pl.ds(t, n, stride=tb), :]
    else:
        row = lambda ref, t, k: jnp.broadcast_to(ref[t, pl.ds(k, 1), :], (vr, LANES))
        tile = lambda ref, t, n: ref[t]

    def accumulate(parts, k, term):
        parts[k % n_acc] = term if parts[k % n_acc] is None else parts[k % n_acc] + term

    sa_parts = [None] * n_acc
    for k in range(nk):
        accumulate(sa_parts, k, s_ref[k] * row(a_ref, 0, k))
    sa0 = _tree_sum(sa_parts)

    def step(t, sa):
        vt = tile(v_ref, t, vr)
        tn = jnp.minimum(t + 1, tb - 1)
        y_parts = [None] * n_acc
        sa_parts = [None] * n_acc
        for k in range(nk):
            s = s_ref[k] * row(w_ref, t, k) + sa * row(b_ref, t, k) + vt * row(k_ref, t, k)
            s_ref[k] = s
            accumulate(y_parts, k, s * row(r_ref, t, k))
            accumulate(sa_parts, k, s * row(a_ref, tn, k))
        if packed:
            y_ref[pl.ds(t, vr, stride=tb), :] = _tree_sum(y_parts)
        else:
            y_ref[t] = _tree_sum(y_parts)
        return _tree_sum(sa_parts)

    lax.fori_loop(0, tb, step, sa0)

    @pl.when(tblk == pl.num_programs(1) - 1)
    def _():
        sf_ref[...] = s_ref[...]


def _rwkv_scan(r, w, k, v, a, b, s0, *, tb, packed=False):
    K, VR, NL = s0.shape
    if packed:
        T = r.shape[0] // K
        assert NL == LANES and T % tb == 0
        kspec = pl.BlockSpec((K * tb, LANES), lambda n, t: (t, 0))
        vspec = pl.BlockSpec((VR * tb, LANES), lambda n, t: (t, 0))
    else:
        T = r.shape[0]
        tb = min(tb, T)
        assert T % tb == 0 and NL % LANES == 0
        kspec = pl.BlockSpec((tb, K, LANES), lambda n, t: (t, 0, n))
        vspec = pl.BlockSpec((tb, VR, LANES), lambda n, t: (t, 0, n))
    vin = vspec
    sspec = pl.BlockSpec((K, VR, LANES), lambda n, t: (0, 0, n))
    return pl.pallas_call(
        functools.partial(_rwkv_scan_kernel, tb=tb, vr=VR, nk=K, packed=packed),
        grid=(NL // LANES, T // tb),
        in_specs=[kspec, kspec, kspec, vin, kspec, kspec, sspec],
        out_specs=[vspec, sspec],
        out_shape=[jax.ShapeDtypeStruct(v.shape, F32),
                   jax.ShapeDtypeStruct((K, VR, NL), F32)],
        scratch_shapes=[pltpu.VMEM((K, VR, LANES), F32)],
        compiler_params=_cparams("parallel", "arbitrary"),
        name="rwkv_scan",
    )(r, w, k, v, a, b, s0)


def _mla_prep_kernel(cq_ref, ckv_ref, kr_ref, qtab_ref, ktab_ref, qn_ref, kvn_ref,
                     wqa_ref, wqb_ref, wk_ref, wv_ref,
                     q_ref, k_ref, v_ref, ckv_o_ref, kr_o_ref, cqn_ref, ckk_ref, *, hg, scale):
    @pl.when(pl.program_id(1) == 0)
    def _():
        cqn_ref[...] = _rms(cq_ref[...], qn_ref[...]).astype(BF16)
        ckv = _rms(ckv_ref[...], kvn_ref[...])
        ckv_o_ref[...] = ckv
        kr = kr_ref[...]
        ktab = ktab_ref[...]
        krope = kr[:, :ROPE_DIM] * ktab[:, :ROPE_DIM] + kr[:, ROPE_DIM:] * ktab[:, ROPE_DIM:]
        kr_o_ref[...] = krope
        ckk_ref[...] = jnp.concatenate([ckv, krope], axis=-1).astype(BF16)

    cqn = cqn_ref[...]
    qtab = qtab_ref[...]
    cosf = jnp.concatenate([qtab[:, :QK_PAD]] * hg, axis=-1)
    sinf = jnp.concatenate([qtab[:, QK_PAD:]] * hg, axis=-1)
    qa = jnp.dot(cqn, wqa_ref[...], preferred_element_type=F32)
    qb = jnp.dot(cqn, wqb_ref[...], preferred_element_type=F32)
    q_ref[...] = ((qa * cosf + qb * sinf) * scale).astype(BF16)
    ckk = ckk_ref[...]
    k_ref[...] = jnp.dot(ckk, wk_ref[...], preferred_element_type=F32).astype(BF16)
    v_ref[...] = jnp.dot(ckk[:, :ckk.shape[1] - ROPE_DIM], wv_ref[...],
                         preferred_element_type=F32).astype(BF16)


def _mla_prep(ps, qtab, ktab, qn, kvn, wqa, wqb, wk, wv, *, col_cq, col_ckv, col_kr, qlora, kvlora,
              n_heads, hg, tm, scale):
    M = ps.shape[0]
    tm = _pick_tile(M, tm, 8)
    n_tab = qtab.shape[0] // tm
    grid = (M // tm, n_heads // hg)
    tabspec = lambda a: pl.BlockSpec((tm, a.shape[1]), lambda i, j: (i % n_tab, 0))
    full = lambda a: pl.BlockSpec(a.shape, lambda i, j: (0,) * a.ndim)
    in_specs = [pl.BlockSpec((tm, qlora), lambda i, j: (i, col_cq // qlora)),
                pl.BlockSpec((tm, kvlora), lambda i, j: (i, col_ckv // kvlora)),
                pl.BlockSpec((tm, 2 * ROPE_DIM), lambda i, j: (i, col_kr // (2 * ROPE_DIM))),
                tabspec(qtab), tabspec(ktab), full(qn), full(kvn),
                pl.BlockSpec((qlora, hg * QK_PAD), lambda i, j: (0, j)),
                pl.BlockSpec((qlora, hg * QK_PAD), lambda i, j: (0, j)),
                pl.BlockSpec((kvlora + ROPE_DIM, hg * QK_PAD), lambda i, j: (0, j)),
                pl.BlockSpec((kvlora, hg * V_DIM), lambda i, j: (0, j))]
    out_specs = [pl.BlockSpec((tm, hg * QK_PAD), lambda i, j: (i, j)),
                 pl.BlockSpec((tm, hg * QK_PAD), lambda i, j: (i, j)),
                 pl.BlockSpec((tm, hg * V_DIM), lambda i, j: (i, j)),
                 pl.BlockSpec((tm, kvlora), lambda i, j: (i, 0)),
                 pl.BlockSpec((tm, ROPE_DIM), lambda i, j: (i, 0))]
    out_shape = [jax.ShapeDtypeStruct((M, n_heads * QK_PAD), BF16),
                 jax.ShapeDtypeStruct((M, n_heads * QK_PAD), BF16),
                 jax.ShapeDtypeStruct((M, n_heads * V_DIM), BF16),
                 jax.ShapeDtypeStruct((M, kvlora), F32),
                 jax.ShapeDtypeStruct((M, ROPE_DIM), F32)]
    return pl.pallas_call(
        functools.partial(_mla_prep_kernel, hg=hg, scale=scale),
        grid=grid, in_specs=in_specs, out_specs=out_specs, out_shape=out_shape,
        scratch_shapes=[pltpu.VMEM((tm, qlora), BF16), pltpu.VMEM((tm, kvlora + ROPE_DIM), BF16)],
        compiler_params=_cparams("parallel", "arbitrary"),
        name="mla_prep",
    )(ps, ps, ps, qtab, ktab, qn, kvn, wqa, wqb, wk, wv)


def _flash_kernel(q_ref, k_ref, v_ref, o_ref, *, tq, tk):
    i = pl.program_id(2)
    q = q_ref[...]
    n_sub = tq // tk

    def block(j, carry, masked):
        m, l, acc = carry
        start = pl.multiple_of(j * tk, tk)
        kb = k_ref[pl.ds(start, tk), :]
        vb = v_ref[pl.ds(start, tk), :]
        s = lax.dot_general(q, kb, (((1,), (1,)), ((), ())), preferred_element_type=F32)
        if masked:
            qpos = i * tq + lax.broadcasted_iota(jnp.int32, s.shape, 0)
            kpos = j * tk + lax.broadcasted_iota(jnp.int32, s.shape, 1)
            s = jnp.where(kpos <= qpos, s, NEG_INF)
        m_new = jnp.maximum(m, jnp.max(s, axis=-1, keepdims=True))
        p = jnp.exp(s - m_new)
        corr = jnp.exp(m - m_new)
        l = corr * l + jnp.sum(p, axis=-1, keepdims=True)
        acc = corr * acc + jnp.dot(p.astype(BF16), vb, preferred_element_type=F32)
        return m_new, l, acc

    init = (jnp.full((tq, 1), NEG_INF, F32), jnp.zeros((tq, 1), F32),
            jnp.zeros((tq, v_ref.shape[1]), F32))
    carry = lax.fori_loop(0, i * n_sub, lambda j, c: block(j, c, False), init)
    for d in range(n_sub):
        carry = block(i * n_sub + d, carry, True)
    m, l, acc = carry
    o_ref[...] = (acc / l).astype(o_ref.dtype)


def _flash_attention(q, k, v, *, B, T, n_heads, tq, tk):
    tq = min(tq, T)
    tk = min(tk, tq)
    nq = T // tq
    return pl.pallas_call(
        functools.partial(_flash_kernel, tq=tq, tk=tk),
        grid=(B, n_heads, nq),
        in_specs=[pl.BlockSpec((tq, QK_PAD), lambda b, h, i: (b * nq + i, h)),
                  pl.BlockSpec((T, QK_PAD), lambda b, h, i: (b, h)),
                  pl.BlockSpec((T, V_DIM), lambda b, h, i: (b, h))],
        out_specs=pl.BlockSpec((tq, V_DIM), lambda b, h, i: (b * nq + i, h)),
        out_shape=jax.ShapeDtypeStruct((B * T, n_heads * V_DIM), F32),
        compiler_params=_cparams("parallel", "parallel", "arbitrary"),
        name="mla_flash",
    )(q, k, v)


def _bmm_kernel(x_ref, w_ref, o_ref):
    o_ref[...] = jnp.dot(x_ref[...].astype(BF16), w_ref[...], preferred_element_type=F32)


def _head_matmul(x, w):
    H, R, K = x.shape
    N = w.shape[2]
    return pl.pallas_call(
        _bmm_kernel,
        grid=(H,),
        in_specs=[pl.BlockSpec((None, R, K), lambda h: (h, 0, 0)),
                  pl.BlockSpec((None, K, N), lambda h: (h, 0, 0))],
        out_specs=pl.BlockSpec((None, R, N), lambda h: (h, 0, 0)),
        out_shape=jax.ShapeDtypeStruct((H, R, N), F32),
        compiler_params=_cparams("parallel"),
        name="head_matmul",
    )(x, w)


def _paged_kernel(pt_ref, q_ref, cn_ref, kn_ref, ckv_hbm, kr_hbm, o_ref,
                  cbuf, kbuf, csem, ksem, m_ref, l_ref, acc_ref, *, pg, n_new, kvlora):
    b = pl.program_id(0)
    g = pl.program_id(1)
    n_g = pl.num_programs(1)
    step = b * n_g + g
    slot = step % 2

    def page_copies(bb, gg, sl):
        cps = []
        for i in range(pg):
            page = pt_ref[bb, gg * pg + i]
            cps.append(pltpu.make_async_copy(ckv_hbm.at[page], cbuf.at[sl, i], csem.at[sl]))
            cps.append(pltpu.make_async_copy(kr_hbm.at[page], kbuf.at[sl, i], ksem.at[sl]))
        return cps

    @pl.when(step == 0)
    def _():
        for cp in page_copies(0, 0, 0):
            cp.start()

    nxt = step + 1

    @pl.when(nxt < pl.num_programs(0) * n_g)
    def _():
        for cp in page_copies(nxt // n_g, nxt % n_g, 1 - slot):
            cp.start()

    for cp in page_copies(b, g, slot):
        cp.wait()
    ckv_refs = [cbuf.at[slot, i] for i in range(pg)]
    kr_refs = [kbuf.at[slot, i] for i in range(pg)]

    @pl.when(g == 0)
    def _():
        m_ref[...] = jnp.full(m_ref.shape, NEG_INF, F32)
        l_ref[...] = jnp.zeros(l_ref.shape, F32)
        acc_ref[...] = jnp.zeros(acc_ref.shape, F32)

    q = q_ref[...]
    ql = q[:, :kvlora]
    qr = q[:, kvlora:]
    dn = (((1,), (1,)), ((), ()))

    def scores(c_ref, k_ref, k_is_transposed):
        kb = k_ref[...].astype(BF16)
        if k_is_transposed:
            rope = jnp.dot(qr, kb, preferred_element_type=F32)
        else:
            rope = lax.dot_general(qr, kb, dn, preferred_element_type=F32)
        return lax.dot_general(ql, c_ref[...].astype(BF16), dn, preferred_element_type=F32) + rope

    def update(s, value_refs):
        m = m_ref[...]
        m_new = jnp.maximum(m, jnp.max(s, axis=-1, keepdims=True))
        p = jnp.exp(s - m_new).astype(BF16)
        corr = jnp.exp(m - m_new)
        l_ref[...] = corr * l_ref[...] + jnp.sum(p.astype(F32), axis=-1, keepdims=True)
        pv = None
        off = 0
        for c_ref in value_refs:
            n = c_ref.shape[0]
            part = jnp.dot(p[:, off:off + n], c_ref[...].astype(BF16), preferred_element_type=F32)
            pv = part if pv is None else pv + part
            off += n
        acc_ref[...] = corr * acc_ref[...] + pv
        m_ref[...] = m_new

    s_all = jnp.concatenate([scores(c, k, True) for c, k in zip(ckv_refs, kr_refs)], axis=-1)
    update(s_all, ckv_refs)

    @pl.when(g == pl.num_programs(1) - 1)
    def _():
        rows = q.shape[0]
        npad = cn_ref.shape[0]
        tq = lax.broadcasted_iota(jnp.int32, (rows, npad), 0) % n_new
        kj = lax.broadcasted_iota(jnp.int32, (rows, npad), 1)
        update(jnp.where(kj <= tq, scores(cn_ref, kn_ref, False), NEG_INF), [cn_ref])
        o_ref[...] = acc_ref[...] / l_ref[...]


def _paged_attention(page_table, qs, ckv_new, kr_new, cache_ckv, cache_krope, *, pg, n_new):
    Bd, R, QW = qs.shape
    kvlora = cache_ckv.shape[2]
    page = cache_ckv.shape[1]
    n_pages = page_table.shape[1]
    pg = min(pg, n_pages)
    assert n_pages % pg == 0
    npad = ckv_new.shape[1]
    krope_t = jnp.swapaxes(cache_krope, 1, 2)
    in_specs = [pl.BlockSpec((None, R, QW), lambda b, g, pt: (b, 0, 0)),
                pl.BlockSpec((None, npad, kvlora), lambda b, g, pt: (b, 0, 0)),
                pl.BlockSpec((None, npad, ROPE_DIM), lambda b, g, pt: (b, 0, 0)),
                pl.BlockSpec(memory_space=pl.ANY),
                pl.BlockSpec(memory_space=pl.ANY)]
    grid_spec = pltpu.PrefetchScalarGridSpec(
        num_scalar_prefetch=1, grid=(Bd, n_pages // pg), in_specs=in_specs,
        out_specs=pl.BlockSpec((None, R, kvlora), lambda b, g, pt: (b, 0, 0)),
        scratch_shapes=[pltpu.VMEM((2, pg, page, kvlora), F32), pltpu.VMEM((2, pg, ROPE_DIM, page), F32),
                        pltpu.SemaphoreType.DMA((2,)), pltpu.SemaphoreType.DMA((2,)),
                        pltpu.VMEM((R, 1), F32), pltpu.VMEM((R, 1), F32), pltpu.VMEM((R, kvlora), F32)])
    return pl.pallas_call(
        functools.partial(_paged_kernel, pg=pg, n_new=n_new, kvlora=kvlora),
        grid_spec=grid_spec,
        out_shape=jax.ShapeDtypeStruct((Bd, R, kvlora), F32),
        compiler_params=_cparams("arbitrary", "arbitrary"),
        name="mla_paged",
    )(page_table, qs, ckv_new, kr_new, cache_ckv, krope_t)


def _merge_rows(x, y, bo, g, ga, gb, om, lnw_ref, lnb_ref, wor_ref, wom_ref, nf_ref,
                rw_ref, rb_ref, *, n_experts, n_heads):
    inv_n = 1.0 / RWKV_HEAD_DIM
    mean = _head_sum(y, n_heads) * inv_n
    yc = y - mean
    var = _head_sum(yc * yc, n_heads) * inv_n
    yn = yc * lax.rsqrt(var + GN_EPS) * lnw_ref[...] + lnb_ref[...]
    o_rwkv = (yn + bo) * g
    h = (x + jnp.dot((jax.nn.sigmoid(ga) * o_rwkv).astype(BF16), wor_ref[...], preferred_element_type=F32)
         + jnp.dot((jax.nn.sigmoid(gb) * om).astype(BF16), wom_ref[...], preferred_element_type=F32))
    hn = _rms(h, nf_ref[...])

    rw = rw_ref[...]
    w_hi = rw.astype(BF16)
    w_r1 = rw - w_hi.astype(F32)
    w_mid = w_r1.astype(BF16)
    x_hi, x_mid, _ = _split3(hn)
    d = lambda a, b: jnp.dot(a, b, preferred_element_type=F32)
    logits = d(x_hi, w_hi) + (d(x_hi, w_mid) + d(x_mid, w_hi)) + rb_ref[...]
    lane = lax.broadcasted_iota(jnp.int32, logits.shape, 1)
    work = jnp.where(lane < n_experts, logits, -jnp.inf)
    vals, idxs = [], []
    for _ in range(TOP_K):
        mx = jnp.max(work, axis=-1, keepdims=True)
        ix = jnp.min(jnp.where(work == mx, lane, LANES), axis=-1, keepdims=True)
        vals.append(mx)
        idxs.append(ix)
        work = jnp.where(lane == ix, -jnp.inf, work)
    ex = [jnp.exp(vv - vals[0]) for vv in vals]
    den = ex[0] + ex[1] + ex[2] + ex[3]
    ti = jnp.zeros(logits.shape, jnp.int32)
    tg = jnp.zeros(logits.shape, F32)
    for j in range(TOP_K):
        ti = jnp.where(lane == j, idxs[j], ti)
        tg = jnp.where(lane == j, ex[j] / den, tg)
    return h, hn, ti, tg


def _merge_kernel(x_ref, y_ref, bo_ref, g_ref, ga_ref, gb_ref, om_ref, *refs, n_experts, n_heads,
                  vsplit):
    consts, (h_ref, hn_ref, ti_ref, tg_ref) = refs[:-4], refs[-4:]
    n_seq, tm, D = x_ref.shape
    for b in range(n_seq):
        if vsplit:
            vr = D // n_heads // vsplit
            pieces = [None] * (vr * vsplit)
            for vq in range(vr):
                rows = y_ref[vq * tm:(vq + 1) * tm, :]
                for vs in range(vsplit):
                    lo = (b * vsplit + vs) * n_heads
                    pieces[vs * vr + vq] = rows[:, lo:lo + n_heads]
            y = jnp.concatenate(pieces, axis=-1)
        else:
            y = y_ref[b]
        h, hn, ti, tg = _merge_rows(x_ref[b], y, bo_ref[b], g_ref[b], ga_ref[b], gb_ref[b], om_ref[b],
                                    *consts, n_experts=n_experts, n_heads=n_heads)
        h_ref[b] = h
        hn_ref[b] = hn
        ti_ref[b] = ti
        tg_ref[b] = tg


def _merge(x, y, bonus, g, p5, o_mla, consts, *, tm, n_experts, n_heads, vsplit):
    G, R, D = x.shape
    tm = _pick_tile(R, tm, 8)
    blk = lambda c=0: pl.BlockSpec((G, tm, D), lambda i, c=c: (0, i, c))
    full = lambda a: pl.BlockSpec(a.shape, lambda i: (0,) * a.ndim)
    if vsplit:
        vr = D // n_heads // vsplit
        yspec = pl.BlockSpec((tm * vr, LANES), lambda i: (i, 0))
    else:
        yspec = blk()
    in_specs = [blk(), yspec, blk(), blk(), blk(3), blk(4), blk()] + [full(a) for a in consts]
    lane_blk = pl.BlockSpec((G, tm, LANES), lambda i: (0, i, 0))
    out_specs = [blk(), blk(), lane_blk, lane_blk]
    out_shape = [jax.ShapeDtypeStruct((G, R, D), F32), jax.ShapeDtypeStruct((G, R, D), F32),
                 jax.ShapeDtypeStruct((G, R, LANES), jnp.int32),
                 jax.ShapeDtypeStruct((G, R, LANES), F32)]
    return pl.pallas_call(
        functools.partial(_merge_kernel, n_experts=n_experts, n_heads=n_heads, vsplit=vsplit),
        grid=(R // tm,), in_specs=in_specs, out_specs=out_specs, out_shape=out_shape,
        compiler_params=_cparams("parallel"),
        name="merge_router",
    )(x, y, bonus, g, p5, p5, o_mla, *consts)


def _w1_prep_kernel(w_ref, p_ref, g_ref, l_ref):
    perm = p_ref[...]
    half = perm.shape[0] // 2
    for c in range(w_ref.shape[1] // perm.shape[0]):
        chunk = w_ref[:, c * 2 * half:(c + 1) * 2 * half].astype(BF16)
        both = jnp.dot(chunk, perm, preferred_element_type=F32)
        g_ref[:, c * half:(c + 1) * half] = both[:, :half].astype(BF16)
        l_ref[:, c * half:(c + 1) * half] = both[:, half:].astype(BF16)


def _w1_prep(w1, *, tr, tc):
    E, D, F2 = w1.shape
    pw = 2 * LANES
    tr = _pick_tile(D, tr, 8)
    tc = _pick_tile(F2, tc, pw)
    j = np.arange(LANES)
    perm = np.zeros((pw, pw), np.float32)
    perm[2 * j, j] = 1.0
    perm[2 * j + 1, LANES + j] = 1.0
    out = jax.ShapeDtypeStruct((E, D, F2 // 2), BF16)
    return pl.pallas_call(
        _w1_prep_kernel,
        grid=(E, D // tr, F2 // tc),
        in_specs=[pl.BlockSpec((None, tr, tc), lambda e, i, j: (e, i, j)),
                  pl.BlockSpec((pw, pw), lambda e, i, j: (0, 0))],
        out_specs=[pl.BlockSpec((None, tr, tc // 2), lambda e, i, j: (e, i, j))] * 2,
        out_shape=[out, out],
        compiler_params=_cparams("parallel", "parallel", "parallel"),
        name="moe_w1_prep",
    )(w1, jnp.asarray(perm, BF16))


def _expert_kernel(se_ref, sh_ref, sr_ref, sm_ref, x_ref, w1g_ref, w1l_ref, b1g_ref, b1l_ref,
                   w2_ref, b2_ref, o_ref, w2b_ref):
    s = pl.program_id(0)
    mode = sm_ref[s]

    @pl.when(mode == 0)
    def _():
        o_ref[...] = jnp.zeros(o_ref.shape, o_ref.dtype)

    @pl.when(mode == 2)
    def _():
        w2b_ref[...] = w2_ref[...].astype(BF16)

    @pl.when(mode >= 1)
    def _():
        x = x_ref[...]
        hg = jnp.dot(x, w1g_ref[...], preferred_element_type=F32) + b1g_ref[...]
        hl = jnp.dot(x, w1l_ref[...], preferred_element_type=F32) + b1l_ref[...]
        hg = jnp.minimum(hg, SWIGLU_LIMIT)
        hl = jnp.clip(hl, -SWIGLU_LIMIT, SWIGLU_LIMIT)
        act = hg * jax.nn.sigmoid(SWIGLU_ALPHA * hg) * (hl + 1.0)
        y = jnp.dot(act.astype(BF16), w2b_ref[...], preferred_element_type=F32)
        first = (sh_ref[s] == 0).astype(F32)
        o_ref[...] = (y + first * b2_ref[...]).astype(o_ref.dtype)


def _row_gather_kernel(tok_ref, x_hbm, o_ref, buf, sem, *, bm):
    i = pl.program_id(0)
    slot = i % 2

    def row_copy(t, sl, r):
        return pltpu.make_async_copy(x_hbm.at[pl.ds(t, 1)], buf.at[sl, pl.ds(r, 1)], sem.at[sl])

    def start_block(blk, sl):
        def body(r, carry):
            row_copy(tok_ref[blk * bm + r], sl, r).start()
            return carry
        lax.fori_loop(0, bm, body, 0, unroll=8)

    @pl.when(i == 0)
    def _():
        start_block(0, 0)

    @pl.when(i + 1 < pl.num_programs(0))
    def _():
        start_block(i + 1, 1 - slot)

    def wait_row(r, carry):
        row_copy(tok_ref[i * bm + r], slot, r).wait()
        return carry
    lax.fori_loop(0, bm, wait_row, 0, unroll=8)
    o_ref[...] = buf[slot].astype(o_ref.dtype)


def _row_gather(x, tok, *, bm):
    M, D = x.shape
    NR = tok.shape[0]
    assert NR % bm == 0
    grid_spec = pltpu.PrefetchScalarGridSpec(
        num_scalar_prefetch=1, grid=(NR // bm,),
        in_specs=[pl.BlockSpec(memory_space=pl.ANY)],
        out_specs=pl.BlockSpec((bm, D), lambda i, tok: (i, 0)),
        scratch_shapes=[pltpu.VMEM((2, bm, D), F32), pltpu.SemaphoreType.DMA((2,))])
    return pl.pallas_call(
        functools.partial(_row_gather_kernel, bm=bm), grid_spec=grid_spec,
        out_shape=jax.ShapeDtypeStruct((NR, D), BF16),
        compiler_params=_cparams("arbitrary"),
        name="moe_row_gather",
    )(tok, x)


def _experts(sched, xs, w1g, w1l, b1g, b1l, w2, b2, *, bm, n_half):
    se, sh, sr, sm = sched
    NR, D = xs.shape
    E, _, F = w1g.shape
    fh = F // n_half
    S = se.shape[0]
    in_specs = [pl.BlockSpec((bm, D), lambda s, se, sh, sr, sm: (sr[s], 0)),
                pl.BlockSpec((None, D, fh), lambda s, se, sh, sr, sm: (se[s], 0, sh[s])),
                pl.BlockSpec((None, D, fh), lambda s, se, sh, sr, sm: (se[s], 0, sh[s])),
                pl.BlockSpec((None, 1, fh), lambda s, se, sh, sr, sm: (se[s], 0, sh[s])),
                pl.BlockSpec((None, 1, fh), lambda s, se, sh, sr, sm: (se[s], 0, sh[s])),
                pl.BlockSpec((None, fh, D), lambda s, se, sh, sr, sm: (se[s], sh[s], 0)),
                pl.BlockSpec((None, 1, D), lambda s, se, sh, sr, sm: (se[s], 0, 0))]
    grid_spec = pltpu.PrefetchScalarGridSpec(
        num_scalar_prefetch=4, grid=(S,), in_specs=in_specs,
        out_specs=pl.BlockSpec((bm, D), lambda s, se, sh, sr, sm: (sr[s], sh[s])),
        scratch_shapes=[pltpu.VMEM((fh, D), BF16)])
    return pl.pallas_call(
        _expert_kernel, grid_spec=grid_spec,
        out_shape=jax.ShapeDtypeStruct((NR, n_half * D), BF16),
        compiler_params=_cparams("arbitrary"),
        name="moe_experts",
    )(se, sh, sr, sm, xs, w1g, w1l, b1g, b1l, w2, b2)


def _combine_kernel(h_ref, yg_ref, tg_ref, nw_ref, o_ref, *, n_half):
    acc = h_ref[...]
    tg = tg_ref[...]
    D = acc.shape[1]
    for j in range(TOP_K):
        rows = yg_ref[j, :, :D].astype(F32)
        for hh in range(1, n_half):
            rows = rows + yg_ref[j, :, hh * D:(hh + 1) * D].astype(F32)
        acc = acc + rows * tg[:, j:j + 1]
    o_ref[...] = _rms(acc, nw_ref[...])


def _combine(h, yg, tg, nw, *, tm, n_half):
    M, D = h.shape
    tm = _pick_tile(M, tm, 8)
    return pl.pallas_call(
        functools.partial(_combine_kernel, n_half=n_half),
        grid=(M // tm,),
        in_specs=[pl.BlockSpec((tm, D), lambda i: (i, 0)),
                  pl.BlockSpec((TOP_K, tm, n_half * D), lambda i: (0, i, 0)),
                  pl.BlockSpec((tm, LANES), lambda i: (i, 0)),
                  pl.BlockSpec((1, D), lambda i: (0, 0))],
        out_specs=pl.BlockSpec((tm, D), lambda i: (i, 0)),
        out_shape=jax.ShapeDtypeStruct((M, D), F32),
        compiler_params=_cparams("parallel"),
        name="moe_combine",
    )(h, yg, tg, nw)


def _rope_tables(positions):
    inv = ROPE_THETA ** (-jnp.arange(0, ROPE_DIM, 2, dtype=F32) / ROPE_DIM)
    ang = positions.astype(F32)[:, None] * inv[None, :]
    cos, sin = jnp.cos(ang), jnp.sin(ang)
    cos2 = jnp.concatenate([cos, cos], axis=-1)
    sin2 = jnp.concatenate([sin, sin], axis=-1)
    n = positions.shape[0]
    ones = jnp.ones((n, NOPE_DIM), F32)
    z64 = jnp.zeros((n, QK_PAD - NOPE_DIM - ROPE_DIM), F32)
    z128 = jnp.zeros((n, NOPE_DIM), F32)
    qtab = jnp.concatenate([ones, cos2, z64, z128, sin2, z64], axis=-1)
    ktab = jnp.concatenate([cos2, sin2], axis=-1)
    return qtab, ktab


def _rot_cols(w):
    half = ROPE_DIM // 2
    return jnp.concatenate([-w[..., half:], w[..., :half]], axis=-1)


def _moe_schedule(top_idx, n_experts, bm, n_half, n_blocks):
    M = top_idx.shape[0]
    flat_e = top_idx.reshape(-1)
    onehot = (flat_e[:, None] == jnp.arange(n_experts, dtype=jnp.int32)[None, :]).astype(jnp.int32)
    rank = jnp.take_along_axis(jnp.cumsum(onehot, axis=0) - onehot, flat_e[:, None], axis=1)[:, 0]
    counts = jnp.sum(onehot, axis=0)
    nb = (counts + bm - 1) // bm
    blk_end = jnp.cumsum(nb)
    blk_start = blk_end - nb
    dest = blk_start[flat_e] * bm + rank
    flat_tok = jnp.repeat(jnp.arange(M, dtype=jnp.int32), TOP_K)
    row_tok = jnp.zeros((n_blocks * bm,), jnp.int32).at[dest].set(flat_tok)
    n_real = blk_end[-1]
    steps_end = jnp.cumsum(nb * n_half)
    s = jnp.arange(n_blocks * n_half, dtype=jnp.int32)
    real = s < n_real * n_half
    e_of = jnp.minimum(jnp.sum(steps_end[None, :] <= s[:, None], axis=1), n_experts - 1).astype(jnp.int32)
    local = s - (steps_end - nb * n_half)[e_of]
    nb_e = jnp.maximum(nb[e_of], 1)
    h_real = local // nb_e
    r_real = blk_start[e_of] + local % nb_e
    last_e = e_of[jnp.maximum(n_real * n_half - 1, 0)]
    tail = s - n_real * n_half
    n_tail = jnp.maximum(n_blocks - n_real, 1)
    se = jnp.where(real, e_of, last_e).astype(jnp.int32)
    sh = jnp.where(real, h_real, n_half - 1 - tail // n_tail).astype(jnp.int32)
    sr = jnp.where(real, r_real, n_real + tail % n_tail).astype(jnp.int32)
    sm = jnp.where(real, jnp.where(local % nb_e == 0, 2, 1), 0).astype(jnp.int32)
    return dest, row_tok, (se, sh, sr, sm)


def kernel(x_prompt, x_sample, cache_ckv, cache_krope, state_wkv, state_shift, page_table, norm_mix_w, w_in, rwkv_mu, rwkv_w0, rwkv_w2, rwkv_a0, rwkv_a2, rwkv_g2, rwkv_k_k, rwkv_k_a, rwkv_r_k, rwkv_ln_w, rwkv_ln_b, mla_q_norm_w, mla_w_uq, mla_kv_norm_w, mla_w_uk, mla_w_uv, w_out, norm_ffn_w, router_w, router_b, expert_w1, expert_b1, expert_w2, expert_b2, norm_final_w):
    B, T, D = x_prompt.shape
    Bd, Td, _ = x_sample.shape
    H = D // RWKV_HEAD_DIM
    HD = RWKV_HEAD_DIM
    dlora, alora, glora = rwkv_w2.shape[0], rwkv_a2.shape[0], rwkv_g2.shape[0]
    qlora, Hm, _ = mla_w_uq.shape
    kvlora = mla_w_uk.shape[0]
    n_experts = router_w.shape[1]
    F = expert_w2.shape[1]
    page = cache_ckv.shape[1]
    past = page_table.shape[1] * page
    scale = float((NOPE_DIM + ROPE_DIM) ** -0.5)
    row2 = lambda a: a.reshape(1, -1).astype(F32)

    sizes = [D, D, D, dlora, alora, glora, qlora, kvlora, ROPE_DIM, D, D]
    offs = np.concatenate([[0], np.cumsum(sizes)]).tolist()
    col = lambda i: w_in[:, offs[i]:offs[i + 1]]
    lw_pad, la_pad = _round_up(dlora, LANES), _round_up(alora, LANES)
    padc = lambda a, n: jnp.pad(a, ((0, 0), (0, n - a.shape[1])))
    perm = np.arange(D).reshape(H, HD).T.reshape(-1)
    pc = lambda a: a[..., perm]
    w5 = jnp.concatenate([pc(col(0)), pc(col(1)), pc(col(2)), pc(col(9)), col(10)], axis=1).astype(BF16)
    lora_w = lw_pad + la_pad + glora
    ws = jnp.concatenate([padc(col(3), lw_pad), padc(col(4), la_pad), col(5), col(6), col(7),
                          col(8), _rot_cols(col(8))], axis=1)
    col_cq, col_ckv, col_kr = lora_w, lora_w + qlora, lora_w + qlora + kvlora
    ws = padc(ws, _round_up(ws.shape[1], LANES)).astype(BF16)
    mu = rwkv_mu
    mur, muk, muv = row2(pc(mu[:D])), row2(pc(mu[D:2 * D])), row2(pc(mu[2 * D:3 * D]))
    m0 = 3 * D
    mul = jnp.concatenate([jnp.pad(mu[m0:m0 + dlora], (0, lw_pad - dlora)),
                           jnp.pad(mu[m0 + dlora:m0 + dlora + alora], (0, la_pad - alora)),
                           mu[m0 + dlora + alora:]]).reshape(1, -1)
    w2p = jnp.pad(pc(rwkv_w2), ((0, lw_pad - dlora), (0, 0))).astype(BF16)
    a2p = jnp.pad(pc(rwkv_a2), ((0, la_pad - alora), (0, 0))).astype(BF16)
    prep_consts = (mur, muk, muv, mul, row2(pc(rwkv_w0)), w2p, row2(pc(rwkv_a0)), a2p,
                   pc(rwkv_g2).astype(BF16), row2(pc(rwkv_k_k)), row2(pc(rwkv_k_a)),
                   row2(pc(rwkv_r_k.reshape(-1))), lw_pad, la_pad)

    zq = jnp.zeros((qlora, Hm, QK_PAD - NOPE_DIM - ROPE_DIM), F32)
    q_nope_w, q_rope_w = mla_w_uq[..., :NOPE_DIM], mla_w_uq[..., NOPE_DIM:]
    wqa = jnp.concatenate([q_nope_w, q_rope_w, zq], axis=-1).reshape(qlora, Hm * QK_PAD).astype(BF16)
    wqb = jnp.concatenate([jnp.zeros_like(q_nope_w), _rot_cols(q_rope_w), zq],
                          axis=-1).reshape(qlora, Hm * QK_PAD).astype(BF16)
    zk = jnp.zeros((kvlora, Hm, QK_PAD - NOPE_DIM), F32)
    wk_top = jnp.concatenate([mla_w_uk, zk], axis=-1).reshape(kvlora, Hm * QK_PAD)
    sel = jnp.concatenate([jnp.zeros((ROPE_DIM, NOPE_DIM), F32), jnp.eye(ROPE_DIM, dtype=F32),
                           jnp.zeros((ROPE_DIM, QK_PAD - NOPE_DIM - ROPE_DIM), F32)], axis=-1)
    wk = jnp.concatenate([wk_top, jnp.tile(sel, (1, Hm))], axis=0).astype(BF16)
    wv = mla_w_uv.reshape(kvlora, Hm * V_DIM).astype(BF16)
    w_uk_t = jnp.transpose(mla_w_uk, (1, 2, 0)).astype(BF16)
    w_uv_h = jnp.transpose(mla_w_uv, (1, 0, 2)).astype(BF16)
    wo_mla = w_out.astype(BF16)
    wo_rwkv = w_out[perm, :].astype(BF16)
    rw = jnp.pad(router_w, ((0, 0), (0, LANES - n_experts)))
    rb = jnp.pad(router_b, (0, LANES - n_experts)).reshape(1, LANES)
    w1g, w1l = _w1_prep(expert_w1, tr=512, tc=2048)
    b1g = expert_b1[:, None, 0::2]
    b1l = expert_b1[:, None, 1::2]
    w2e = expert_w2
    b2e = expert_b2[:, None, :]
    nmw = row2(norm_mix_w)

    def project_stage(x2d, positions, tm_mla):
        p5 = _project(x2d, nmw, w5, normalize=True, tm=1024, tn=512)
        ps = _project(x2d, nmw, ws, normalize=True, tm=1024, tn=ws.shape[1])
        qtab, ktab = _rope_tables(positions)
        mla = _mla_prep(ps, qtab, ktab, row2(mla_q_norm_w), row2(mla_kv_norm_w), wqa, wqb, wk, wv,
                        col_cq=col_cq, col_ckv=col_ckv, col_kr=col_kr, qlora=qlora, kvlora=kvlora,
                        n_heads=Hm, hg=min(4, Hm), tm=tm_mla, scale=scale)
        return p5, ps, mla

    merge_consts = (row2(pc(rwkv_ln_w)), row2(pc(rwkv_ln_b)), wo_rwkv, wo_mla,
                    row2(norm_ffn_w), rw, rb)

    xp = x_prompt.reshape(B * T, D)
    p5_p, ps_p, mla_p = project_stage(xp, jnp.arange(T), min(512, T))
    vsplit = max(1, LANES // (B * H))
    assert HD % vsplit == 0 and B * H * vsplit <= LANES
    vr = HD // vsplit
    tb_p = min(64, T)
    zr, zw, zk, zv, za, zb, g_p, bo_p = _rwkv_prep_packed(
        p5_p.reshape(B, T, -1), ps_p.reshape(B, T, -1), prep_consts, B=B, T=T, tt=tb_p, D=D,
        lora_w=lora_w, n_heads=H, vsplit=vsplit)
    s0_p = jnp.zeros((HD, vr, LANES), F32)
    y_pk, s_kl = _rwkv_scan(zr, zw, zk, zv, za, zb, s0_p, tb=tb_p, packed=True)
    nl_p = B * vsplit * H
    wkv_p = s_kl[:, :, :nl_p].reshape(HD, vr, B, vsplit, H).transpose(2, 4, 3, 1, 0).reshape(B, H, HD, HD)

    q_p, kf_p, vf_p, ckv_p, kr_p = mla_p
    o_mla_p = _flash_attention(q_p, kf_p, vf_p, B=B, T=T, n_heads=Hm, tq=1024, tk=1024)
    h_p, hn_p, ti_p, tg_p = _merge(
        x_prompt, y_pk, bo_p, g_p, p5_p.reshape(B, T, -1),
        o_mla_p.reshape(B, T, -1), merge_consts, tm=tb_p, n_experts=n_experts, n_heads=H, vsplit=vsplit)

    xs_tm = jnp.swapaxes(x_sample, 0, 1).reshape(Td * Bd, D)
    pos_s = jnp.repeat(past + jnp.arange(Td), Bd)
    p5_s, ps_s, mla_s = project_stage(xs_tm, pos_s, Td * Bd)
    s5 = _project(state_shift, nmw, w5, normalize=False, tm=1024, tn=512).reshape(1, Bd, -1)
    ss = _project(state_shift, nmw, ws, normalize=False, tm=1024, tn=ws.shape[1]).reshape(1, Bd, -1)
    r_s, lw_s, k_s, v_s, al_s, be_s, g_s, bo_s = _rwkv_prep(
        p5_s, ps_s, s5, ss, prep_consts, n_groups=1, n_tiles=Td, tt=Bd, shift=Bd, D=D, lora_w=lora_w,
        n_heads=H)

    def to_kl_s(a):
        return a.reshape(Td, Bd, HD, H).transpose(0, 2, 1, 3).reshape(Td, HD, Bd * H)

    nl_s = Bd * H
    pad_s = _round_up(nl_s, LANES) - nl_s
    pads = lambda a: jnp.pad(a, ((0, 0), (0, 0), (0, pad_s))) if pad_s else a
    s0_s = pads(state_wkv.transpose(3, 2, 0, 1).reshape(HD, HD, nl_s))
    y_sl, s_sl = _rwkv_scan(pads(to_kl_s(r_s)), pads(to_kl_s(lw_s)), pads(to_kl_s(k_s)),
                            pads(to_kl_s(v_s)), pads(to_kl_s(al_s)), pads(to_kl_s(be_s)), s0_s, tb=Td)
    y_s = y_sl[:, :, :nl_s].reshape(Td, HD, Bd, H).transpose(0, 2, 1, 3).reshape(Td * Bd, D)
    wkv_s = s_sl[:, :, :nl_s].reshape(HD, HD, Bd, H).transpose(2, 3, 1, 0)

    q_s, _, _, ckv_s, kr_s = mla_s
    q4 = q_s.reshape(Td, Bd, Hm, QK_PAD)
    qn_h = q4[..., :NOPE_DIM].transpose(2, 1, 0, 3).reshape(Hm, Bd * Td, NOPE_DIM)
    q_lat = _head_matmul(qn_h, w_uk_t)
    q_lat = q_lat.reshape(Hm, Bd, Td, kvlora).transpose(1, 0, 2, 3).reshape(Bd, Hm * Td, kvlora)
    q_rp = q4[..., NOPE_DIM:NOPE_DIM + ROPE_DIM].transpose(1, 2, 0, 3).reshape(Bd, Hm * Td, ROPE_DIM)
    qs = jnp.concatenate([q_lat.astype(BF16), q_rp], axis=-1)
    ckv_s_bt = ckv_s.reshape(Td, Bd, kvlora).transpose(1, 0, 2)
    kr_s_bt = kr_s.reshape(Td, Bd, ROPE_DIM).transpose(1, 0, 2)
    npad = _round_up(Td, 8)
    padn = lambda a: jnp.pad(a, ((0, 0), (0, npad - Td), (0, 0)))
    o_lat = _paged_attention(page_table, qs, padn(ckv_s_bt), padn(kr_s_bt), cache_ckv, cache_krope,
                             pg=64, n_new=Td)
    o_lat_h = o_lat.reshape(Bd, Hm, Td, kvlora).transpose(1, 2, 0, 3).reshape(Hm, Td * Bd, kvlora)
    o_mla_s = _head_matmul(o_lat_h, w_uv_h)
    o_mla_s = o_mla_s.transpose(1, 0, 2).reshape(Td * Bd, Hm * V_DIM)

    g1 = lambda a: a[None]
    h_s, hn_s, ti_s, tg_s = _merge(g1(xs_tm), g1(y_s), g1(bo_s), g1(g_s), g1(p5_s), g1(o_mla_s),
                                   merge_consts, tm=128, n_experts=n_experts, n_heads=H, vsplit=0)
    flat = lambda a: a.reshape(-1, a.shape[-1])
    h_all = jnp.concatenate([flat(h_p), flat(h_s)], axis=0)
    hn_all = jnp.concatenate([flat(hn_p), flat(hn_s)], axis=0)
    ti_all = jnp.concatenate([flat(ti_p), flat(ti_s)], axis=0)[:, :TOP_K]
    tg_all = jnp.concatenate([flat(tg_p), flat(tg_s)], axis=0)
    M = h_all.shape[0]
    bm = 256
    n_half = 2 if F % (2 * LANES) == 0 else 1
    n_blocks = (M * TOP_K + n_experts * (bm - 1)) // bm
    dest, row_tok, sched = _moe_schedule(ti_all, n_experts, bm, n_half, n_blocks)
    xs_rows = _row_gather(hn_all, row_tok, bm=bm)
    y_halves = _experts(sched, xs_rows, w1g, w1l, b1g, b1l, w2e, b2e, bm=bm, n_half=n_half)
    dest_slot_major = dest.reshape(M, TOP_K).T.reshape(-1)
    yg = y_halves[dest_slot_major].reshape(TOP_K, M, n_half * D)
    y_all = _combine(h_all, yg, tg_all, row2(norm_final_w), tm=128, n_half=n_half)

    y_prompt = y_all[:B * T].reshape(B, T, D)
    y_sample = y_all[B * T:].reshape(Td, Bd, D).transpose(1, 0, 2)
    last = jnp.concatenate([x_prompt[:, -1], x_sample[:, -1]], axis=0)
    shift_out = _rmsnorm_rows(last, nmw)
    return (y_prompt, y_sample,
            ckv_p.reshape(B, T // page, page, kvlora), kr_p.reshape(B, T // page, page, ROPE_DIM),
            wkv_p, shift_out[:B],
            ckv_s_bt, kr_s_bt, wkv_s, shift_out[B:])
```
